```python
import math
import jax, jax.numpy as jnp
from jax import lax
import numpy as np

D_MODEL = 1024
BATCH = 1
SEQ = 16384
DEPTH = 4
DEC_BATCH = 4
DEC_SEQ = 8192
PAST_LEN = 128

N_Q_HEADS = 8
N_KV_HEADS = 2
HEAD_DIM = 64
D_ATTN = N_Q_HEADS * HEAD_DIM
D_KV = N_KV_HEADS * HEAD_DIM
D_QKV = D_ATTN + 2 * D_KV
WINDOW = 128
BLOCK = 128
ROPE_THETA = 500000.0
ROPE_DIM = HEAD_DIM // 4
D_HYENA = 512
HYENA_GROUPS = 8
SHORT_CONV = 3
FILTER_EMB = 33
FILTER_HID = 64
FAST_DECAY_PCT = 0.3
SLOW_DECAY_PCT = 1.5
DECAY_TARGET = 1e-2
MAX_DECAY = math.log(DECAY_TARGET) / FAST_DECAY_PCT
MIN_DECAY = math.log(DECAY_TARGET) / SLOW_DECAY_PCT
D_MIX = D_ATTN + D_HYENA
D_IN = D_QKV + 3 * D_HYENA
D_FF = 2816
FFN_CONV = 3
D_PLE = 256
EPS = 1e-6

kernel_name = 'hybrid_hyena_swa_encoder'


def _rmsnorm(x, g):
    xf = x.astype(jnp.float32)
    y = xf * lax.rsqrt(jnp.mean(xf * xf, axis=-1, keepdims=True) + EPS)
    return (y * g.astype(jnp.float32)).astype(x.dtype)


def _dwconv3(x, w, b):
    xp = jnp.pad(x, ((0, 0), (1, 1), (0, 0)))
    return xp[:, :-2] * w[0] + xp[:, 1:-1] * w[1] + xp[:, 2:] * w[2] + b


def _rope_tables(L):
    inv = ROPE_THETA ** (-jnp.arange(0, ROPE_DIM, 2, dtype=jnp.float32) / ROPE_DIM)
    ang = jnp.arange(L, dtype=jnp.float32)[:, None] * inv[None]
    return jnp.cos(ang), jnp.sin(ang)


def _partial_rope(x, cos, sin):
    half = ROPE_DIM // 2
    xf = x.astype(jnp.float32)
    x1 = xf[..., :half]
    x2 = xf[..., half:ROPE_DIM]
    c = cos[None, :, None, :]
    s = sin[None, :, None, :]
    out = jnp.concatenate([x1 * c - x2 * s, x2 * c + x1 * s, xf[..., ROPE_DIM:]], axis=-1)
    return out.astype(x.dtype)


def _window_attention(q, k, v, sink):
    B, L = q.shape[0], q.shape[1]
    nb = L // BLOCK
    G = N_Q_HEADS // N_KV_HEADS
    qb = q.reshape(B, nb, BLOCK, N_KV_HEADS, G, HEAD_DIM)

    def band(t):
        tp = jnp.pad(t, ((0, 0), (BLOCK, BLOCK), (0, 0), (0, 0)))
        tp = tp.reshape(B, nb + 2, BLOCK, N_KV_HEADS, HEAD_DIM)
        return jnp.concatenate([tp[:, :-2], tp[:, 1:-1], tp[:, 2:]], axis=2)

    kb = band(k)
    vb = band(v)
    s = jnp.einsum('bnqhgd,bnkhd->bnhgqk', qb, kb,
                   preferred_element_type=jnp.float32) * (HEAD_DIM ** -0.5)
    qpos = jnp.arange(nb)[:, None, None] * BLOCK + jnp.arange(BLOCK)[None, :, None]
    kpos = jnp.arange(nb)[:, None, None] * BLOCK - BLOCK + jnp.arange(3 * BLOCK)[None, None, :]
    valid = (jnp.abs(kpos - qpos) <= WINDOW) & (kpos >= 0) & (kpos < L)
    s = jnp.where(valid[None, :, None, None], s, -1e30)
    sk = sink.astype(jnp.float32).reshape(N_KV_HEADS, G)[None, None, :, :, None, None]
    m = jnp.maximum(jnp.max(s, axis=-1, keepdims=True), sk)
    e = jnp.exp(s - m)
    pr = e / (jnp.sum(e, axis=-1, keepdims=True) + jnp.exp(sk - m))
    o = jnp.einsum('bnhgqk,bnkhd->bnqhgd', pr.astype(v.dtype), vb)
    return o.reshape(B, L, D_ATTN)


def _hyena_filter(L, w1, b1, fr1, w2, b2, fr2, w3):
    f32 = jnp.float32
    t = jnp.linspace(0.0, 1.0, L, dtype=f32)[:, None]
    bands = (FILTER_EMB - 1) // 2
    w = 2.0 * math.pi * jnp.arange(L, dtype=f32)[:, None] / L
    f = jnp.linspace(1e-4, bands - 1, bands, dtype=f32)[None]
    z = jnp.concatenate([t, jnp.cos(f * w), -jnp.sin(f * w)], axis=-1)
    hdn = jnp.sin(fr1.astype(f32) * (z @ w1.astype(f32) + b1.astype(f32)))
    hdn = jnp.sin(fr2.astype(f32) * (hdn @ w2.astype(f32) + b2.astype(f32)))
    h = (hdn @ w3.astype(f32)).reshape(L, 2, D_HYENA)
    deltas = jnp.linspace(MIN_DECAY, MAX_DECAY, D_HYENA, dtype=f32)
    decay = jnp.exp(-t * jnp.abs(deltas)[None])
    h = h * decay[:, None, :]
    return h[:, 0], h[:, 1]


def _bidir_long_conv(v, h_fwd, h_bwd, d_bias):
    B, L, C = v.shape
    f32 = jnp.float32
    k_full = jnp.concatenate([h_fwd, jnp.zeros((1, C), f32), h_bwd[:0:-1]], axis=0)
    kf = jnp.fft.rfft(k_full, n=2 * L, axis=0)
    vf32 = v.astype(f32)
    vf = jnp.fft.rfft(vf32, n=2 * L, axis=1)
    y = jnp.fft.irfft(vf * kf[None], n=2 * L, axis=1)[:, :L]
    return y + vf32 * d_bias.astype(f32)


def _trunk(x, p, weights):
    (rms_mix, w_in, q_norm, k_norm, sink, w_short, b_short,
     filt_w1, filt_b1, filt_freq1, filt_w2, filt_b2, filt_freq2, filt_w3, hyena_bias,
     norm_attn_out, norm_hyena_out, w_out, rms_ffn, w_up, w_ffconv, b_ffconv, w_down,
     w_ple_gate, w_ple_proj) = weights
    B, L, _ = x.shape
    cos, sin = _rope_tables(L)
    h = x
    for i in range(DEPTH):
        n = _rmsnorm(h, rms_mix[i])
        z = n @ w_in[i]
        q = z[..., :D_ATTN].reshape(B, L, N_Q_HEADS, HEAD_DIM)
        k = z[..., D_ATTN:D_ATTN + D_KV].reshape(B, L, N_KV_HEADS, HEAD_DIM)
        v = z[..., D_ATTN + D_KV:D_QKV].reshape(B, L, N_KV_HEADS, HEAD_DIM)
        q = _partial_rope(_rmsnorm(q, q_norm[i]), cos, sin)
        k = _partial_rope(_rmsnorm(k, k_norm[i]), cos, sin)
        attn = _window_attention(q, k, v, sink[i])
        u = _dwconv3(z[..., D_QKV:], w_short[i], b_short[i])
        x0, x1, hv = jnp.split(u, 3, axis=-1)
        hf, hb = _hyena_filter(L, filt_w1[i], filt_b1[i], filt_freq1[i],
                               filt_w2[i], filt_b2[i], filt_freq2[i], filt_w3[i])
        hy = x0.astype(jnp.float32) * _bidir_long_conv(x1 * hv, hf, hb, hyena_bias[i])
        hy = hy.astype(h.dtype)
        mix = jnp.concatenate([_rmsnorm(attn, norm_attn_out[i]),
                               _rmsnorm(hy, norm_hyena_out[i])], axis=-1)
        h = h + mix @ w_out[i]
        n2 = _rmsnorm(h, rms_ffn[i])
        uu = _dwconv3(n2 @ w_up[i], w_ffconv[i], b_ffconv[i])
        a, g = jnp.split(uu, 2, axis=-1)
        h = h + (jax.nn.silu(g) * a) @ w_down[i]
        h = h + jax.nn.sigmoid(h @ w_ple_gate[i]) * (p[i] @ w_ple_proj[i])
    return h


def setup_inputs(seed: int = 0) -> dict:
    key = jax.random.key(seed)
    ks = jax.random.split(key, 32)
    f32 = jnp.float32

    def nrm(k, shape, scale):
        return jax.random.normal(k, shape, f32) * scale

    def gain(k, shape):
        return 1.0 + 0.05 * jax.random.normal(k, shape, f32)

    return {
        'x_prompt': nrm(ks[0], (BATCH, SEQ, D_MODEL), 1.0),
        'x_sample': nrm(ks[1], (DEC_BATCH, DEC_SEQ, D_MODEL), 1.0),
        'p_prompt': nrm(ks[2], (DEPTH, BATCH, SEQ, D_PLE), 1.0),
        'p_sample': nrm(ks[3], (DEPTH, DEC_BATCH, DEC_SEQ, D_PLE), 1.0),
        'rms_mix': gain(ks[4], (DEPTH, D_MODEL)),
        'w_in': nrm(ks[5], (DEPTH, D_MODEL, D_IN), D_MODEL ** -0.5),
        'q_norm': gain(ks[6], (DEPTH, HEAD_DIM)),
        'k_norm': gain(ks[7], (DEPTH, HEAD_DIM)),
        'sink': nrm(ks[8], (DEPTH, N_Q_HEADS), 0.5),
        'w_short': nrm(ks[9], (DEPTH, SHORT_CONV, 3 * D_HYENA), SHORT_CONV ** -0.5),
        'b_short': nrm(ks[10], (DEPTH, 3 * D_HYENA), 0.02),
        'filt_w1': nrm(ks[11], (DEPTH, FILTER_EMB, FILTER_HID), FILTER_EMB ** -0.5),
        'filt_b1': nrm(ks[12], (DEPTH, FILTER_HID), 0.1),
        'filt_freq1': gain(ks[13], (DEPTH, FILTER_HID)),
        'filt_w2': nrm(ks[14], (DEPTH, FILTER_HID, FILTER_HID), FILTER_HID ** -0.5),
        'filt_b2': nrm(ks[15], (DEPTH, FILTER_HID), 0.1),
        'filt_freq2': gain(ks[16], (DEPTH, FILTER_HID)),
        'filt_w3': nrm(ks[17], (DEPTH, FILTER_HID, 2 * D_HYENA), FILTER_HID ** -0.5),
        'hyena_bias': nrm(ks[18], (DEPTH, D_HYENA), 0.5),
        'norm_attn_out': gain(ks[19], (DEPTH, D_ATTN)),
        'norm_hyena_out': gain(ks[20], (DEPTH, D_HYENA)),
        'w_out': nrm(ks[21], (DEPTH, D_MIX, D_MODEL), D_MIX ** -0.5),
        'rms_ffn': gain(ks[22], (DEPTH, D_MODEL)),
        'w_up': nrm(ks[23], (DEPTH, D_MODEL, 2 * D_FF), D_MODEL ** -0.5),
        'w_ffconv': nrm(ks[24], (DEPTH, FFN_CONV, 2 * D_FF), FFN_CONV ** -0.5),
        'b_ffconv': nrm(ks[25], (DEPTH, 2 * D_FF), 0.02),
        'w_down': nrm(ks[26], (DEPTH, D_FF, D_MODEL), D_FF ** -0.5),
        'w_ple_gate': nrm(ks[27], (DEPTH, D_MODEL, D_MODEL), D_MODEL ** -0.5),
        'w_ple_proj': nrm(ks[28], (DEPTH, D_PLE, D_MODEL), D_PLE ** -0.5),
    }


def reference(x_prompt, x_sample, p_prompt, p_sample, rms_mix, w_in, q_norm, k_norm, sink,
              w_short, b_short, filt_w1, filt_b1, filt_freq1, filt_w2, filt_b2, filt_freq2,
              filt_w3, hyena_bias, norm_attn_out, norm_hyena_out, w_out, rms_ffn, w_up,
              w_ffconv, b_ffconv, w_down, w_ple_gate, w_ple_proj):
    weights = (rms_mix, w_in, q_norm, k_norm, sink, w_short, b_short,
               filt_w1, filt_b1, filt_freq1, filt_w2, filt_b2, filt_freq2, filt_w3, hyena_bias,
               norm_attn_out, norm_hyena_out, w_out, rms_ffn, w_up, w_ffconv, b_ffconv, w_down,
               w_ple_gate, w_ple_proj)
    y_prompt = _trunk(x_prompt, p_prompt, weights)
    y_sample = _trunk(x_sample, p_sample, weights)
    return (y_prompt, y_sample)
```

```python
import functools
import math

import numpy as np
import jax
import jax.numpy as jnp
from jax import lax
from jax.experimental import pallas as pl
from jax.experimental.pallas import tpu as pltpu

F32 = jnp.float32
BF16 = jnp.bfloat16

D_MODEL = 1024
DEPTH = 4
N_Q_HEADS = 8
N_KV_HEADS = 2
HEAD_DIM = 64
D_ATTN = N_Q_HEADS * HEAD_DIM
D_KV = N_KV_HEADS * HEAD_DIM
D_QKV = D_ATTN + 2 * D_KV
WINDOW = 128
BLOCK = 128
ROPE_THETA = 500000.0
ROPE_DIM = HEAD_DIM // 4
D_HYENA = 512
FILTER_EMB = 33
FILTER_HID = 64
FAST_DECAY_PCT = 0.3
SLOW_DECAY_PCT = 1.5
DECAY_TARGET = 1e-2
MAX_DECAY = math.log(DECAY_TARGET) / FAST_DECAY_PCT
MIN_DECAY = math.log(DECAY_TARGET) / SLOW_DECAY_PCT
D_IN = D_QKV + 3 * D_HYENA
D_FF = 2816
D_PLE = 256
EPS = 1e-6

LANES = 128
SUBLANES = 8
BF16_ROWS = 16
FFT_N2 = 64
FFT_MG = 16
FFT_KB = 8
VMEM_LIMIT = 52 * 1024 * 1024


def _cparams(n_axes):
    return pltpu.CompilerParams(dimension_semantics=("arbitrary",) * n_axes,
                                vmem_limit_bytes=VMEM_LIMIT)


def _rms(x, g):
    ms = jnp.mean(x * x, axis=-1, keepdims=True)
    return x * lax.rsqrt(ms + EPS) * g


def _shift_rows(z, prev_row, next_row):
    m = z.shape[0]
    rid = lax.broadcasted_iota(jnp.int32, z.shape, 0)
    zp = jnp.where(rid == 0, prev_row, pltpu.roll(z, 1, axis=0))
    zn = jnp.where(rid == m - 1, next_row, pltpu.roll(z, m - 1, axis=0))
    return zp, zn


def _inproj_kernel(xm_ref, xp_ref, xn_ref, g_ref, w_ref, qg_ref, kg_ref, bd_ref,
                   rc_ref, rs1_ref, rs2_ref, ws_ref, bs_ref,
                   q_ref, k_ref, v_ref, x0_ref, vv_ref, *, tm, tiles_per_seq):
    i = pl.program_id(0)
    first = (i % tiles_per_seq) == 0
    last = (i % tiles_per_seq) == tiles_per_seq - 1
    xp = jnp.where(first, 0.0, xp_ref[...])
    xn = jnp.where(last, 0.0, xn_ref[...])
    x = jnp.concatenate([xm_ref[...], xp, xn], axis=0)
    n = _rms(x, g_ref[...]).astype(BF16)
    z = jnp.dot(n, w_ref[...], preferred_element_type=F32)

    bd = bd_ref[...]
    rc, rs1, rs2 = rc_ref[...], rs1_ref[...], rs2_ref[...]

    def norm_rope(zs, gain, bds):
        ms = jnp.dot((zs * zs).astype(BF16), bds, preferred_element_type=F32)
        y = zs * lax.rsqrt(ms + EPS) * gain
        outs = []
        for s in range(y.shape[1] // LANES):
            ys = y[:, s * LANES:(s + 1) * LANES]
            outs.append(ys * rc + pltpu.roll(ys, LANES - ROPE_DIM // 2, axis=1) * rs1
                        + pltpu.roll(ys, ROPE_DIM // 2, axis=1) * rs2)
        return outs

    qs = norm_rope(z[:tm, :D_ATTN], qg_ref[...], bd)
    for s, qv in enumerate(qs):
        q_ref[:, s * LANES:(s + 1) * LANES] = qv.astype(BF16)
    ks = norm_rope(z[:tm, D_ATTN:D_ATTN + D_KV], kg_ref[...], bd[:D_KV, :D_KV])
    k_ref[...] = ks[0].astype(BF16)
    v_ref[...] = z[:tm, D_ATTN + D_KV:D_QKV].astype(BF16)

    zh = z[:, D_QKV:]
    zc = zh[:tm]
    zp, zn = _shift_rows(zc, zh[tm + SUBLANES - 1:tm + SUBLANES], zh[tm + SUBLANES:tm + SUBLANES + 1])
    u = zp * ws_ref[0:1, :] + zc * ws_ref[1:2, :] + zn * ws_ref[2:3, :] + bs_ref[...]
    x0_ref[...] = u[:, :D_HYENA].astype(BF16)
    vv_ref[...] = (u[:, D_HYENA:2 * D_HYENA] * u[:, 2 * D_HYENA:]).astype(BF16)


def _inproj(x, g, w, qg, kg, bd, rc, rs1, rs2, ws, bs, *, seq_len, tm):
    T = x.shape[0]
    tiles_per_seq = seq_len // tm
    r8 = tm // SUBLANES
    nb8 = T // SUBLANES
    const = lambda i: (0, 0)
    in_specs = [
        pl.BlockSpec((tm, D_MODEL), lambda i: (i, 0)),
        pl.BlockSpec((SUBLANES, D_MODEL), lambda i: (jnp.maximum(i * r8 - 1, 0), 0)),
        pl.BlockSpec((SUBLANES, D_MODEL), lambda i: (jnp.minimum((i + 1) * r8, nb8 - 1), 0)),
        pl.BlockSpec((1, D_MODEL), const),
        pl.BlockSpec((D_MODEL, D_IN), const),
        pl.BlockSpec((1, D_ATTN), const),
        pl.BlockSpec((1, D_KV), const),
        pl.BlockSpec((D_ATTN, D_ATTN), const),
        pl.BlockSpec((tm, LANES), lambda i: (i % tiles_per_seq, 0)),
        pl.BlockSpec((tm, LANES), lambda i: (i % tiles_per_seq, 0)),
        pl.BlockSpec((tm, LANES), lambda i: (i % tiles_per_seq, 0)),
        pl.BlockSpec((3, 3 * D_HYENA), const),
        pl.BlockSpec((1, 3 * D_HYENA), const),
    ]
    out_specs = [
        pl.BlockSpec((tm, D_ATTN), lambda i: (i, 0)),
        pl.BlockSpec((tm, D_KV), lambda i: (i, 0)),
        pl.BlockSpec((tm, D_KV), lambda i: (i, 0)),
        pl.BlockSpec((tm, D_HYENA), lambda i: (i, 0)),
        pl.BlockSpec((tm, D_HYENA), lambda i: (i, 0)),
    ]
    out_shape = [
        jax.ShapeDtypeStruct((T, D_ATTN), BF16),
        jax.ShapeDtypeStruct((T, D_KV), BF16),
        jax.ShapeDtypeStruct((T, D_KV), BF16),
        jax.ShapeDtypeStruct((T, D_HYENA), BF16),
        jax.ShapeDtypeStruct((T, D_HYENA), BF16),
    ]
    return pl.pallas_call(
        functools.partial(_inproj_kernel, tm=tm, tiles_per_seq=tiles_per_seq),
        grid=(T // tm,), in_specs=in_specs, out_specs=out_specs, out_shape=out_shape,
        compiler_params=_cparams(1), name="inproj",
    )(x, x, x, g, w, qg, kg, bd, rc, rs1, rs2, ws, bs)


def _attn_kernel(sink_ref, q_ref, km_ref, kp_ref, kn_ref, vm_ref, vp_ref, vn_ref, g_ref,
                 o_ref, kbuf, vbuf, obuf, *, tq, tiles_per_seq):
    i = pl.program_id(0)
    first = (i % tiles_per_seq) == 0
    last = (i % tiles_per_seq) == tiles_per_seq - 1
    nblk = tq // BLOCK
    kbuf[0:BLOCK] = kp_ref[...]
    kbuf[BLOCK:BLOCK + tq] = km_ref[...]
    kbuf[BLOCK + tq:] = kn_ref[...]
    vbuf[0:BLOCK] = vp_ref[...]
    vbuf[BLOCK:BLOCK + tq] = vm_ref[...]
    vbuf[BLOCK + tq:] = vn_ref[...]
    r = lax.broadcasted_iota(jnp.int32, (BLOCK, 3 * BLOCK), 0)
    c = lax.broadcasted_iota(jnp.int32, (BLOCK, 3 * BLOCK), 1)
    d = c - BLOCK - r
    band = (d >= -WINDOW) & (d <= WINDOW)
    G = N_Q_HEADS // N_KV_HEADS
    for j in range(nblk):
        valid = band
        if j == 0:
            valid = valid & (c >= jnp.where(first, BLOCK, 0))
        if j == nblk - 1:
            valid = valid & (c < jnp.where(last, 2 * BLOCK, 3 * BLOCK))
        for hk in range(N_KV_HEADS):
            kb = kbuf[j * BLOCK:(j + 3) * BLOCK, hk * HEAD_DIM:(hk + 1) * HEAD_DIM]
            vb = vbuf[j * BLOCK:(j + 3) * BLOCK, hk * HEAD_DIM:(hk + 1) * HEAD_DIM]
            for gq in range(G):
                h = hk * G + gq
                qh = q_ref[j * BLOCK:(j + 1) * BLOCK, h * HEAD_DIM:(h + 1) * HEAD_DIM]
                s = lax.dot_general(qh, kb, (((1,), (1,)), ((), ())), preferred_element_type=F32)
                s = jnp.where(valid, s, -1e30)
                sk = sink_ref[h]
                m = jnp.maximum(jnp.max(s, axis=-1, keepdims=True), sk)
                e = jnp.exp(s - m)
                den = jnp.sum(e, axis=-1, keepdims=True) + jnp.exp(sk - m)
                o = jnp.dot(e.astype(BF16), vb, preferred_element_type=F32)
                obuf[:, h * HEAD_DIM:(h + 1) * HEAD_DIM] = o / den
        o_ref[j * BLOCK:(j + 1) * BLOCK, :] = _rms(obuf[...], g_ref[...]).astype(BF16)


def _attn(sink, q, k, v, g, *, seq_len, tq):
    T = q.shape[0]
    tiles_per_seq = seq_len // tq
    rb = tq // BLOCK
    nbb = T // BLOCK
    main = lambda i, s: (i, 0)
    prev = lambda i, s: (jnp.maximum(i * rb - 1, 0), 0)
    nxt = lambda i, s: (jnp.minimum((i + 1) * rb, nbb - 1), 0)
    grid_spec = pltpu.PrefetchScalarGridSpec(
        num_scalar_prefetch=1, grid=(T // tq,),
        in_specs=[
            pl.BlockSpec((tq, D_ATTN), main),
            pl.BlockSpec((tq, D_KV), main), pl.BlockSpec((BLOCK, D_KV), prev), pl.BlockSpec((BLOCK, D_KV), nxt),
            pl.BlockSpec((tq, D_KV), main), pl.BlockSpec((BLOCK, D_KV), prev), pl.BlockSpec((BLOCK, D_KV), nxt),
            pl.BlockSpec((1, D_ATTN), lambda i, s: (0, 0)),
        ],
        out_specs=pl.BlockSpec((tq, D_ATTN), main),
        scratch_shapes=[pltpu.VMEM((tq + 2 * BLOCK, D_KV), BF16), pltpu.VMEM((tq + 2 * BLOCK, D_KV), BF16),
                        pltpu.VMEM((BLOCK, D_ATTN), F32)],
    )
    return pl.pallas_call(
        functools.partial(_attn_kernel, tq=tq, tiles_per_seq=tiles_per_seq),
        grid_spec=grid_spec, out_shape=jax.ShapeDtypeStruct((T, D_ATTN), BF16),
        compiler_params=_cparams(1), name="attn",
    )(sink, q, k, k, k, v, v, v, g)


def _fft_sizes(seq_len):
    n1h = seq_len // FFT_N2
    k1p = n1h + SUBLANES
    return n1h, k1p


@functools.lru_cache(maxsize=None)
def _dft_consts(seq_len):
    n1h, k1p = _fft_sizes(seq_len)
    n1 = 2 * n1h
    k = np.arange(k1p)[:, None]
    valid = (k <= n1h)
    th = 2.0 * np.pi * ((k * np.arange(n1)[None, :]) % n1) / n1
    w1 = np.concatenate([np.cos(th) * valid, -np.sin(th) * valid], axis=0)
    scale = np.where((k == 0) | (k == n1h), 1.0, 2.0) * valid / n1
    thh = th[:, :n1h]
    winv = np.concatenate([np.cos(thh) * scale, -np.sin(thh) * scale], axis=0).T
    a = 2.0 * np.pi * ((np.arange(FFT_N2)[:, None] * np.arange(FFT_N2)[None, :]) % FFT_N2) / FFT_N2
    cm, sm = np.cos(a), np.sin(a)
    g = np.block([[cm, sm], [-sm, cm]])
    gi = np.block([[cm, -sm], [sm, cm]]) / FFT_N2
    return w1, winv, g, gi


def _twiddle_tables(seq_len):
    n1h, k1p = _fft_sizes(seq_len)
    n = 2 * seq_len
    k = jnp.arange(k1p, dtype=jnp.int32)[:, None]
    m = jnp.arange(FFT_N2, dtype=jnp.int32)[None, :]
    ph = ((k * m) % n).astype(F32) * (2.0 * math.pi / n)
    tc = jnp.broadcast_to(jnp.cos(ph)[:, :, None], (k1p, FFT_N2, LANES))
    ts = jnp.broadcast_to(jnp.sin(ph)[:, :, None], (k1p, FFT_N2, LANES))
    return tc, ts


def _fft1_kernel(w_ref, tc_ref, ts_ref, x_ref, o_ref, *, k1p):
    a = jnp.einsum('kn,nmc->kmc', w_ref[...], x_ref[...], preferred_element_type=F32)
    ar, ai = a[:k1p], a[k1p:]
    c, s = tc_ref[...], ts_ref[...]
    o_ref[0] = (ar * c + ai * s).astype(BF16)
    o_ref[1] = (ai * c - ar * s).astype(BF16)


def _fft1(w1, tc, ts, x, *, k1p):
    B, n1in, _, C = x.shape
    grid = (FFT_N2 // FFT_MG, B, C // LANES)
    return pl.pallas_call(
        functools.partial(_fft1_kernel, k1p=k1p),
        grid=grid,
        in_specs=[
            pl.BlockSpec((2 * k1p, n1in), lambda m, b, j: (0, 0)),
            pl.BlockSpec((k1p, FFT_MG, LANES), lambda m, b, j: (0, m, 0)),
            pl.BlockSpec((k1p, FFT_MG, LANES), lambda m, b, j: (0, m, 0)),
            pl.BlockSpec((None, n1in, FFT_MG, LANES), lambda m, b, j: (b, 0, m, j)),
        ],
        out_specs=pl.BlockSpec((None, 2, k1p, FFT_MG, LANES), lambda m, b, j: (b, 0, 0, m, j)),
        out_shape=jax.ShapeDtypeStruct((B, 2, k1p, FFT_N2, C), BF16),
        compiler_params=_cparams(3), name="fft1",
    )(w1, tc, ts, x)


def _fspec_kernel(g_ref, a_ref, o_ref):
    for j in range(FFT_KB):
        x = jnp.concatenate([a_ref[0, j], a_ref[1, j]], axis=0)
        b = jnp.dot(g_ref[...], x, preferred_element_type=F32)
        o_ref[0, j] = b[:FFT_N2]
        o_ref[1, j] = b[FFT_N2:]


def _fspec(g, a):
    _, k1p, _, C = a.shape
    spec = pl.BlockSpec((2, FFT_KB, FFT_N2, C), lambda kb: (0, kb, 0, 0))
    return pl.pallas_call(
        _fspec_kernel, grid=(k1p // FFT_KB,),
        in_specs=[pl.BlockSpec((2 * FFT_N2, 2 * FFT_N2), lambda kb: (0, 0)), spec],
        out_specs=spec, out_shape=jax.ShapeDtypeStruct(a.shape, F32),
        compiler_params=_cparams(1), name="fspec",
    )(g, a)


def _fft2_kernel(g_ref, gi_ref, a_ref, h_ref, o_ref):
    for j in range(FFT_KB):
        x = jnp.concatenate([a_ref[0, j], a_ref[1, j]], axis=0)
        b = jnp.dot(g_ref[...], x, preferred_element_type=F32)
        br, bi = b[:FFT_N2], b[FFT_N2:]
        hr, hi = h_ref[0, j], h_ref[1, j]
        y = jnp.concatenate([br * hr - bi * hi, br * hi + bi * hr], axis=0).astype(BF16)
        p = jnp.dot(gi_ref[...], y, preferred_element_type=F32)
        o_ref[0, j] = p[:FFT_N2].astype(BF16)
        o_ref[1, j] = p[FFT_N2:].astype(BF16)


def _fft2(g, gi, a, hspec):
    B, _, k1p, _, C = a.shape
    return pl.pallas_call(
        _fft2_kernel, grid=(k1p // FFT_KB, B),
        in_specs=[
            pl.BlockSpec((2 * FFT_N2, 2 * FFT_N2), lambda kb, b: (0, 0)),
            pl.BlockSpec((2 * FFT_N2, 2 * FFT_N2), lambda kb, b: (0, 0)),
            pl.BlockSpec((None, 2, FFT_KB, FFT_N2, C), lambda kb, b: (b, 0, kb, 0, 0)),
            pl.BlockSpec((2, FFT_KB, FFT_N2, C), lambda kb, b: (0, kb, 0, 0)),
        ],
        out_specs=pl.BlockSpec((None, 2, FFT_KB, FFT_N2, C), lambda kb, b: (b, 0, kb, 0, 0)),
        out_shape=jax.ShapeDtypeStruct(a.shape, BF16),
        compiler_params=_cparams(2), name="fft2",
    )(g, gi, a, hspec)


def _ifft1_kernel(w_ref, tc_ref, ts_ref, a_ref, o_ref):
    pr = a_ref[0].astype(F32)
    pi = a_ref[1].astype(F32)
    c, s = tc_ref[...], ts_ref[...]
    a = jnp.concatenate([(pr * c - pi * s).astype(BF16), (pi * c + pr * s).astype(BF16)], axis=0)
    o_ref[...] = jnp.einsum('nk,kmc->nmc', w_ref[...], a, preferred_element_type=F32).astype(BF16)


def _ifft1(winv, tc, ts, a):
    B, _, k1p, _, C = a.shape
    n1h = winv.shape[0]
    return pl.pallas_call(
        _ifft1_kernel, grid=(FFT_N2 // FFT_MG, B, C // LANES),
        in_specs=[
            pl.BlockSpec((n1h, 2 * k1p), lambda m, b, j: (0, 0)),
            pl.BlockSpec((k1p, FFT_MG, LANES), lambda m, b, j: (0, m, 0)),
            pl.BlockSpec((k1p, FFT_MG, LANES), lambda m, b, j: (0, m, 0)),
            pl.BlockSpec((None, 2, k1p, FFT_MG, LANES), lambda m, b, j: (b, 0, 0, m, j)),
        ],
        out_specs=pl.BlockSpec((None, n1h, FFT_MG, LANES), lambda m, b, j: (b, 0, m, j)),
        out_shape=jax.ShapeDtypeStruct((B, n1h, FFT_N2, C), BF16),
        compiler_params=_cparams(3), name="ifft1",
    )(winv, tc, ts, a)


def _split_dot(a, b):
    ah = a.astype(BF16)
    al = (a - ah.astype(F32)).astype(BF16)
    bh = b.astype(BF16)
    bl = (b - bh.astype(F32)).astype(BF16)
    d = functools.partial(jnp.dot, preferred_element_type=F32)
    return d(ah, bh) + d(ah, bl) + d(al, bh)


def _filt_kernel(fr_ref, w1_ref, b1_ref, q1_ref, w2_ref, b2_ref, q2_ref, w3_ref, dl_ref, db_ref,
                 o_ref, *, seq_len, tl):
    i = pl.program_id(0)
    L = seq_len

    def positions(width):
        n = i * tl + lax.broadcasted_iota(jnp.int32, (tl, width), 0)
        return n, jnp.where(n < L, n, 2 * L - n).astype(F32)

    n1, pos = positions(LANES)
    lane = lax.broadcasted_iota(jnp.int32, (tl, LANES), 1)
    bands = (FILTER_EMB - 1) // 2
    t = pos / (L - 1)
    arg = fr_ref[...] * (2.0 * math.pi * pos / L)
    feat = jnp.where(lane == 0, t,
                     jnp.where(lane <= bands, jnp.cos(arg),
                               jnp.where(lane <= 2 * bands, -jnp.sin(arg), 0.0)))
    h = jnp.sin(q1_ref[...] * (_split_dot(feat, w1_ref[...]) + b1_ref[...]))
    h = jnp.sin(q2_ref[...] * (_split_dot(h, w2_ref[...]) + b2_ref[...]))
    hh = _split_dot(h, w3_ref[...])
    nw, posw = positions(D_HYENA)
    decay = jnp.exp(-(posw / (L - 1)) * dl_ref[...])
    val = jnp.where(nw == L, 0.0, hh * decay)
    val = val + jnp.where(nw == 0, db_ref[...], 0.0)
    o_ref[...] = val.astype(BF16)


def _filt(fr, w1, b1, q1, w2, b2, q2, w3, dl, db, *, seq_len, tl):
    half = seq_len // tl
    const = lambda i: (0, 0)
    return pl.pallas_call(
        functools.partial(_filt_kernel, seq_len=seq_len, tl=tl),
        grid=(2 * half,),
        in_specs=[
            pl.BlockSpec((1, LANES), const),
            pl.BlockSpec((LANES, FILTER_HID), const), pl.BlockSpec((1, FILTER_HID), const),
            pl.BlockSpec((1, FILTER_HID), const),
            pl.BlockSpec((FILTER_HID, FILTER_HID), const), pl.BlockSpec((1, FILTER_HID), const),
            pl.BlockSpec((1, FILTER_HID), const),
            pl.BlockSpec((FILTER_HID, D_HYENA), lambda i: (0, i // half)),
            pl.BlockSpec((1, D_HYENA), const), pl.BlockSpec((1, D_HYENA), const),
        ],
        out_specs=pl.BlockSpec((tl, D_HYENA), lambda i: (i, 0)),
        out_shape=jax.ShapeDtypeStruct((2 * seq_len, D_HYENA), BF16),
        compiler_params=_cparams(1), name="filt",
    )(fr, w1, b1, q1, w2, b2, q2, w3, dl, db)


FF_CHUNK = D_FF // 2


def _mixffn_kernel(hm_ref, hp_ref, hn_ref, am_ref, ap_ref, an_ref, xm_ref, xp_ref, xn_ref,
                   ym_ref, yp_ref, yn_ref, gh_ref, wo_ref, gf_ref, wu_ref, wc_ref, bc_ref,
                   h1_ref, act_ref, *, tm, tiles_per_seq):
    i = pl.program_id(0)
    first = (i % tiles_per_seq) == 0
    last = (i % tiles_per_seq) == tiles_per_seq - 1

    def stack(m_ref, p_ref, n_ref):
        p = p_ref[...]
        n = n_ref[...]
        p = jnp.where(first, jnp.zeros_like(p), p)
        n = jnp.where(last, jnp.zeros_like(n), n)
        return jnp.concatenate([m_ref[...], p, n], axis=0)

    h = stack(hm_ref, hp_ref, hn_ref)
    a = stack(am_ref, ap_ref, an_ref)
    hy = stack(xm_ref, xp_ref, xn_ref).astype(F32) * stack(ym_ref, yp_ref, yn_ref).astype(F32)
    hyn = _rms(hy, gh_ref[...]).astype(BF16)
    h1 = (h + jnp.dot(a, wo_ref[:D_ATTN, :], preferred_element_type=F32)
          + jnp.dot(hyn, wo_ref[D_ATTN:, :], preferred_element_type=F32))
    h1_ref[...] = h1[:tm]
    n2 = _rms(h1, gf_ref[...]).astype(BF16)
    pidx = tm + BF16_ROWS - 1
    nidx = tm + BF16_ROWS

    def conv(c0):
        u = jnp.dot(n2, wu_ref[:, c0:c0 + FF_CHUNK], preferred_element_type=F32)
        uc = u[:tm]
        up, un = _shift_rows(uc, u[pidx:pidx + 1], u[nidx:nidx + 1])
        return (up * wc_ref[0:1, c0:c0 + FF_CHUNK] + uc * wc_ref[1:2, c0:c0 + FF_CHUNK]
                + un * wc_ref[2:3, c0:c0 + FF_CHUNK] + bc_ref[:, c0:c0 + FF_CHUNK])

    for cidx in range(D_FF // FF_CHUNK):
        c0 = cidx * FF_CHUNK
        ua = conv(c0)
        ug = conv(D_FF + c0)
        act_ref[:, c0:c0 + FF_CHUNK] = (ug * jax.nn.sigmoid(ug) * ua).astype(BF16)


def _mixffn(h, attn, x0, y, gh, wo, gf, wu, wc, bc, *, seq_len, tm):
    T = h.shape[0]
    tiles_per_seq = seq_len // tm
    r16 = tm // BF16_ROWS
    nb16 = T // BF16_ROWS
    main = lambda i: (i, 0)
    prev = lambda i: (jnp.maximum(i * r16 - 1, 0), 0)
    nxt = lambda i: (jnp.minimum((i + 1) * r16, nb16 - 1), 0)
    const = lambda i: (0, 0)

    def trio(width):
        return [pl.BlockSpec((tm, width), main), pl.BlockSpec((BF16_ROWS, width), prev),
                pl.BlockSpec((BF16_ROWS, width), nxt)]

    in_specs = (trio(D_MODEL) + trio(D_ATTN) + trio(D_HYENA) + trio(D_HYENA) + [
        pl.BlockSpec((1, D_HYENA), const),
        pl.BlockSpec((D_MODEL, D_MODEL), const),
        pl.BlockSpec((1, D_MODEL), const),
        pl.BlockSpec((D_MODEL, 2 * D_FF), const),
        pl.BlockSpec((3, 2 * D_FF), const),
        pl.BlockSpec((1, 2 * D_FF), const),
    ])
    return pl.pallas_call(
        functools.partial(_mixffn_kernel, tm=tm, tiles_per_seq=tiles_per_seq),
        grid=(T // tm,), in_specs=in_specs,
        out_specs=[pl.BlockSpec((tm, D_MODEL), main), pl.BlockSpec((tm, D_FF), main)],
        out_shape=[jax.ShapeDtypeStruct((T, D_MODEL), F32), jax.ShapeDtypeStruct((T, D_FF), BF16)],
        compiler_params=_cparams(1), name="mixffn",
    )(h, h, h, attn, attn, attn, x0, x0, x0, y, y, y, gh, wo, gf, wu, wc, bc)


def _down_kernel(h_ref, act_ref, p_ref, wd_ref, wg_ref, wp_ref, o_ref):
    h2 = h_ref[...] + jnp.dot(act_ref[...], wd_ref[...], preferred_element_type=F32)
    gate = jax.nn.sigmoid(jnp.dot(h2.astype(BF16), wg_ref[...], preferred_element_type=F32))
    pp = jnp.dot(p_ref[...].astype(BF16), wp_ref[...], preferred_element_type=F32)
    o_ref[...] = h2 + gate * pp


def _down(h1, act, p, wd, wg, wp, *, tm):
    T = h1.shape[0]
    main = lambda i: (i, 0)
    const = lambda i: (0, 0)
    return pl.pallas_call(
        _down_kernel, grid=(T // tm,),
        in_specs=[pl.BlockSpec((tm, D_MODEL), main), pl.BlockSpec((tm, D_FF), main),
                  pl.BlockSpec((tm, D_PLE), main),
                  pl.BlockSpec((D_FF, D_MODEL), const), pl.BlockSpec((D_MODEL, D_MODEL), const),
                  pl.BlockSpec((D_PLE, D_MODEL), const)],
        out_specs=pl.BlockSpec((tm, D_MODEL), main),
        out_shape=jax.ShapeDtypeStruct((T, D_MODEL), F32),
        compiler_params=_cparams(1), name="down",
    )(h1, act, p, wd, wg, wp)


def _rope_slabs(seq_len):
    half = ROPE_DIM // 2
    inv = ROPE_THETA ** (-jnp.arange(0, ROPE_DIM, 2, dtype=F32) / ROPE_DIM)
    ang = jnp.arange(seq_len, dtype=F32)[:, None] * inv[None]
    cos, sin = jnp.cos(ang), jnp.sin(ang)
    ones = jnp.ones((seq_len, HEAD_DIM - ROPE_DIM), F32)
    zeros = jnp.zeros((seq_len, HEAD_DIM - ROPE_DIM), F32)
    zh = jnp.zeros((seq_len, half), F32)
    rc = jnp.concatenate([cos, cos, ones], axis=1)
    rs1 = jnp.concatenate([-sin, zh, zeros], axis=1)
    rs2 = jnp.concatenate([zh, sin, zeros], axis=1)
    rep = LANES // HEAD_DIM
    return tuple(jnp.tile(a, (1, rep)) for a in (rc, rs1, rs2))


def _group_consts(seq_len):
    n1h, k1p = _fft_sizes(seq_len)
    w1, winv, g, gi = _dft_consts(seq_len)
    tc, ts = _twiddle_tables(seq_len)
    return dict(
        seq_len=seq_len, n1h=n1h, k1p=k1p,
        w1_full=jnp.asarray(w1, BF16), w1_half=jnp.asarray(w1[:, :n1h], BF16),
        winv=jnp.asarray(winv, BF16), g=jnp.asarray(g, BF16), gi=jnp.asarray(gi, BF16),
        tc=tc, ts=ts, rope=_rope_slabs(seq_len),
    )


def _layer_weights(i, rms_mix, w_in, q_norm, k_norm, sink, w_short, b_short, filt_w1, filt_b1, filt_freq1,
                   filt_w2, filt_b2, filt_freq2, filt_w3, hyena_bias, norm_attn_out, norm_hyena_out,
                   w_out, rms_ffn, w_up, w_ffconv, b_ffconv, w_down, w_ple_gate, w_ple_proj):
    rep_q = D_ATTN // HEAD_DIM
    rep_k = D_KV // HEAD_DIM
    bands = (FILTER_EMB - 1) // 2
    freqs = jnp.linspace(1e-4, bands - 1, bands, dtype=F32)
    fr = jnp.zeros((1, LANES), F32).at[0, 1:1 + bands].set(freqs).at[0, 1 + bands:1 + 2 * bands].set(freqs)
    w1p = jnp.zeros((LANES, FILTER_HID), F32).at[:FILTER_EMB].set(filt_w1[i])
    deltas = jnp.abs(jnp.linspace(MIN_DECAY, MAX_DECAY, D_HYENA, dtype=F32))
    return dict(
        g_mix=rms_mix[i][None], w_in=w_in[i].astype(BF16),
        qg=(jnp.tile(q_norm[i], rep_q) * (HEAD_DIM ** -0.5))[None], kg=jnp.tile(k_norm[i], rep_k)[None],
        sink=sink[i], ws=w_short[i], bs=b_short[i][None],
        fr=fr, fw1=w1p, fb1=filt_b1[i][None], fq1=filt_freq1[i][None],
        fw2=filt_w2[i], fb2=filt_b2[i][None], fq2=filt_freq2[i][None], fw3=filt_w3[i],
        deltas=deltas[None], dbias=hyena_bias[i][None],
        g_attn=norm_attn_out[i][None], g_hy=norm_hyena_out[i][None],
        w_out=w_out[i].astype(BF16), g_ffn=rms_ffn[i][None], w_up=w_up[i].astype(BF16),
        wc=w_ffconv[i], bc=b_ffconv[i][None], w_down=w_down[i].astype(BF16),
        w_gate=w_ple_gate[i].astype(BF16), w_proj=w_ple_proj[i].astype(BF16),
    )


def _layer(h, p, lw, gc, bd, batch, *, tm, tq, tl):
    L = gc["seq_len"]
    n1h, k1p = gc["n1h"], gc["k1p"]
    rc, rs1, rs2 = gc["rope"]
    q, k, v, x0, vv = _inproj(h, lw["g_mix"], lw["w_in"], lw["qg"], lw["kg"], bd, rc, rs1, rs2,
                              lw["ws"], lw["bs"], seq_len=L, tm=tm)
    attn = _attn(lw["sink"], q, k, v, lw["g_attn"], seq_len=L, tq=tq)
    kfull = _filt(lw["fr"], lw["fw1"], lw["fb1"], lw["fq1"], lw["fw2"], lw["fb2"], lw["fq2"], lw["fw3"],
                  lw["deltas"], lw["dbias"], seq_len=L, tl=tl)
    fa = _fft1(gc["w1_full"], gc["tc"], gc["ts"], kfull.reshape(1, 2 * n1h, FFT_N2, D_HYENA), k1p=k1p)
    hspec = _fspec(gc["g"], fa[0])
    a = _fft1(gc["w1_half"], gc["tc"], gc["ts"], vv.reshape(batch, n1h, FFT_N2, D_HYENA), k1p=k1p)
    a = _fft2(gc["g"], gc["gi"], a, hspec)
    y = _ifft1(gc["winv"], gc["tc"], gc["ts"], a).reshape(batch * L, D_HYENA)
    h1, act = _mixffn(h, attn, x0, y, lw["g_hy"], lw["w_out"], lw["g_ffn"], lw["w_up"], lw["wc"], lw["bc"],
                      seq_len=L, tm=tm)
    return _down(h1, act, p, lw["w_down"], lw["w_gate"], lw["w_proj"], tm=tm)


def _trunk(x, p, weights, *, tm=256, tq=512, tl=512):
    batch, L, _ = x.shape
    gc = _group_consts(L)
    bd = jnp.asarray(np.kron(np.eye(D_ATTN // HEAD_DIM), np.full((HEAD_DIM, HEAD_DIM), 1.0 / HEAD_DIM)), BF16)
    h = x.reshape(batch * L, D_MODEL)
    for i in range(DEPTH):
        lw = _layer_weights(i, *weights)
        h = _layer(h, p[i].reshape(batch * L, D_PLE), lw, gc, bd, batch, tm=tm, tq=tq, tl=tl)
    return h.reshape(batch, L, D_MODEL)


def kernel(x_prompt, x_sample, p_prompt, p_sample, rms_mix, w_in, q_norm, k_norm, sink, w_short, b_short, filt_w1, filt_b1, filt_freq1, filt_w2, filt_b2, filt_freq2, filt_w3, hyena_bias, norm_attn_out, norm_hyena_out, w_out, rms_ffn, w_up, w_ffconv, b_ffconv, w_down, w_ple_gate, w_ple_proj):
    weights = (rms_mix, w_in, q_norm, k_norm, sink, w_short, b_short, filt_w1, filt_b1, filt_freq1,
               filt_w2, filt_b2, filt_freq2, filt_w3, hyena_bias, norm_attn_out, norm_hyena_out,
               w_out, rms_ffn, w_up, w_ffconv, b_ffconv, w_down, w_ple_gate, w_ple_proj)
    y_prompt = _trunk(x_prompt, p_prompt, weights)
    y_sample = _trunk(x_sample, p_sample, weights)
    return (y_prompt, y_sample)
```

```python
import functools
import math

import numpy as np
import jax
import jax.numpy as jnp
from jax import lax
from jax.experimental import pallas as pl
from jax.experimental.pallas import tpu as pltpu

F32 = jnp.float32
BF16 = jnp.bfloat16

D_MODEL = 1024
DEPTH = 4
N_Q_HEADS = 8
N_KV_HEADS = 2
HEAD_DIM = 64
D_ATTN = N_Q_HEADS * HEAD_DIM
D_KV = N_KV_HEADS * HEAD_DIM
D_QKV = D_ATTN + 2 * D_KV
WINDOW = 128
BLOCK = 128
ROPE_THETA = 500000.0
ROPE_DIM = HEAD_DIM // 4
D_HYENA = 512
FILTER_EMB = 33
FILTER_HID = 64
FAST_DECAY_PCT = 0.3
SLOW_DECAY_PCT = 1.5
DECAY_TARGET = 1e-2
MAX_DECAY = math.log(DECAY_TARGET) / FAST_DECAY_PCT
MIN_DECAY = math.log(DECAY_TARGET) / SLOW_DECAY_PCT
D_IN = D_QKV + 3 * D_HYENA
D_FF = 2816
D_PLE = 256
EPS = 1e-6
MASKED = -1e30

LANES = 128
SUBLANES = 8
BF16_ROWS = 16
FFT_N2 = 64
FFT_MG = 16
FFT_KB = 8
VMEM_LIMIT = 56 * 1024 * 1024


def _cparams(n_axes):
    return pltpu.CompilerParams(dimension_semantics=("arbitrary",) * n_axes,
                                vmem_limit_bytes=VMEM_LIMIT)


def _resident(shape):
    return pl.BlockSpec(shape, lambda *_: (0,) * len(shape), pipeline_mode=pl.Buffered(1))


def _rms(x, g):
    ms = jnp.mean(x * x, axis=-1, keepdims=True)
    return x * lax.rsqrt(ms + EPS) * g


def _conv3(u, tm, w_ref, b_ref, c0, c1):
    rows = u.shape[0]
    up = pltpu.roll(u, 1, axis=0)[:tm]
    un = pltpu.roll(u, rows - 1, axis=0)[:tm]
    return up * w_ref[0:1, c0:c1] + u[:tm] * w_ref[1:2, c0:c1] + un * w_ref[2:3, c0:c1] + b_ref[:, c0:c1]


def _inproj_kernel(xm_ref, xp_ref, xn_ref, g_ref, w_ref, qg_ref, kg_ref, bd_ref,
                   rc_ref, rs1_ref, rs2_ref, ws_ref, bs_ref,
                   q_ref, kd_ref, va_ref, x0_ref, vv_ref, *, tm, tiles_per_seq):
    i = pl.program_id(0)
    first = (i % tiles_per_seq) == 0
    last = (i % tiles_per_seq) == tiles_per_seq - 1
    xp = jnp.where(first, 0.0, xp_ref[...])
    xn = jnp.where(last, 0.0, xn_ref[...])
    x = jnp.concatenate([xm_ref[...], xn, xp], axis=0)
    n = _rms(x, g_ref[...]).astype(BF16)
    z = jnp.dot(n, w_ref[...], preferred_element_type=F32)

    bd = bd_ref[...]
    rc, rs1, rs2 = rc_ref[...], rs1_ref[...], rs2_ref[...]

    def norm_rope(zs, gain, bds):
        ms = jnp.dot((zs * zs).astype(BF16), bds, preferred_element_type=F32)
        y = zs * lax.rsqrt(ms + EPS) * gain
        outs = []
        for s in range(y.shape[1] // LANES):
            ys = y[:, s * LANES:(s + 1) * LANES]
            outs.append(ys * rc + pltpu.roll(ys, LANES - ROPE_DIM // 2, axis=1) * rs1
                        + pltpu.roll(ys, ROPE_DIM // 2, axis=1) * rs2)
        return outs

    qs = norm_rope(z[:tm, :D_ATTN], qg_ref[...], bd)
    for s, qv in enumerate(qs):
        q_ref[:, s * LANES:(s + 1) * LANES] = qv.astype(BF16)
    k = norm_rope(z[:tm, D_ATTN:D_ATTN + D_KV], kg_ref[...], bd[:D_KV, :D_KV])[0]
    v = z[:tm, D_ATTN + D_KV:D_QKV]
    low = lax.broadcasted_iota(jnp.int32, (tm, LANES), 1) < HEAD_DIM
    kr = pltpu.roll(k, HEAD_DIM, axis=1)
    vr = pltpu.roll(v, HEAD_DIM, axis=1)
    kd_ref[:, :LANES] = jnp.where(low, k, kr).astype(BF16)
    kd_ref[:, LANES:] = jnp.where(low, kr, k).astype(BF16)
    va_ref[:, :LANES] = jnp.where(low, v, 1.0).astype(BF16)
    va_ref[:, LANES:] = jnp.where(low, vr, 1.0).astype(BF16)

    u = _conv3(z[:, D_QKV:], tm, ws_ref, bs_ref, 0, 3 * D_HYENA)
    x0_ref[...] = u[:, :D_HYENA].astype(BF16)
    vv_ref[...] = (u[:, D_HYENA:2 * D_HYENA] * u[:, 2 * D_HYENA:]).astype(BF16)


def _inproj(x, g, w, qg, kg, bd, rc, rs1, rs2, ws, bs, *, seq_len, tm):
    T = x.shape[0]
    tiles_per_seq = seq_len // tm
    r8 = tm // SUBLANES
    nb8 = T // SUBLANES
    rope_spec = pl.BlockSpec((tm, LANES), lambda i: (i % tiles_per_seq, 0))
    in_specs = [
        pl.BlockSpec((tm, D_MODEL), lambda i: (i, 0)),
        pl.BlockSpec((SUBLANES, D_MODEL), lambda i: (jnp.maximum(i * r8 - 1, 0), 0)),
        pl.BlockSpec((SUBLANES, D_MODEL), lambda i: (jnp.minimum((i + 1) * r8, nb8 - 1), 0)),
        _resident((1, D_MODEL)),
        _resident((D_MODEL, D_IN)),
        _resident((1, D_ATTN)),
        _resident((1, D_KV)),
        _resident((D_ATTN, D_ATTN)),
        rope_spec, rope_spec, rope_spec,
        _resident((3, 3 * D_HYENA)),
        _resident((1, 3 * D_HYENA)),
    ]
    widths = (D_ATTN, 2 * D_KV, 2 * D_KV, D_HYENA, D_HYENA)
    out_specs = [pl.BlockSpec((tm, wd), lambda i: (i, 0)) for wd in widths]
    out_shape = [jax.ShapeDtypeStruct((T, wd), BF16) for wd in widths]
    return pl.pallas_call(
        functools.partial(_inproj_kernel, tm=tm, tiles_per_seq=tiles_per_seq),
        grid=(T // tm,), in_specs=in_specs, out_specs=out_specs, out_shape=out_shape,
        compiler_params=_cparams(1), name="inproj",
    )(x, x, x, g, w, qg, kg, bd, rc, rs1, rs2, ws, bs)


def _attn_kernel(sink_ref, q_ref, km_ref, kp_ref, kn_ref, vm_ref, vp_ref, vn_ref, g_ref,
                 o_ref, kbuf, vbuf, *, tq, tiles_per_seq):
    i = pl.program_id(0)
    first = (i % tiles_per_seq) == 0
    last = (i % tiles_per_seq) == tiles_per_seq - 1
    nblk = tq // BLOCK
    G = N_Q_HEADS // N_KV_HEADS
    kbuf[0:BLOCK] = kp_ref[...]
    kbuf[BLOCK:BLOCK + tq] = km_ref[...]
    kbuf[BLOCK + tq:] = kn_ref[...]
    vbuf[0:BLOCK] = vp_ref[...]
    vbuf[BLOCK:BLOCK + tq] = vm_ref[...]
    vbuf[BLOCK + tq:] = vn_ref[...]
    r = lax.broadcasted_iota(jnp.int32, (BLOCK, 3 * BLOCK), 0)
    c = lax.broadcasted_iota(jnp.int32, (BLOCK, 3 * BLOCK), 1)
    d = c - BLOCK - r
    band = jnp.where((d >= -WINDOW) & (d <= WINDOW), 0.0, MASKED)
    low = lax.broadcasted_iota(jnp.int32, (BLOCK, LANES), 1) < HEAD_DIM
    mlo = jnp.where(low, 1.0, 0.0).astype(BF16)
    mhi = jnp.where(low, 0.0, 1.0).astype(BF16)
    rid = lax.broadcasted_iota(jnp.int32, (G * BLOCK, 1), 0)
    for j in range(nblk):
        bias = band
        if j == 0:
            bias = jnp.where(first & (c < BLOCK), MASKED, bias)
        if j == nblk - 1:
            bias = jnp.where(last & (c >= 2 * BLOCK), MASKED, bias)
        slabs = []
        for hk in range(N_KV_HEADS):
            kb = kbuf[j * BLOCK:(j + 3) * BLOCK, hk * LANES:(hk + 1) * LANES]
            vb = vbuf[j * BLOCK:(j + 3) * BLOCK, hk * LANES:(hk + 1) * LANES]
            c0 = hk * G * HEAD_DIM
            qa = q_ref[j * BLOCK:(j + 1) * BLOCK, c0:c0 + LANES]
            qb = q_ref[j * BLOCK:(j + 1) * BLOCK, c0 + LANES:c0 + 2 * LANES]
            ql = jnp.concatenate([qa * mlo, qa * mhi, qb * mlo, qb * mhi], axis=0)
            s = lax.dot_general(ql, kb, (((1,), (1,)), ((), ())), preferred_element_type=F32)
            s = (s.reshape(G, BLOCK, 3 * BLOCK) + bias[None]).reshape(G * BLOCK, 3 * BLOCK)
            sk = jnp.where(rid < BLOCK, sink_ref[hk * G],
                           jnp.where(rid < 2 * BLOCK, sink_ref[hk * G + 1],
                                     jnp.where(rid < 3 * BLOCK, sink_ref[hk * G + 2], sink_ref[hk * G + 3])))
            m = jnp.maximum(jnp.max(s, axis=-1, keepdims=True), sk)
            e = jnp.exp((s - m).astype(BF16))
            out = jnp.dot(e, vb, preferred_element_type=F32)
            den = pltpu.roll(out, HEAD_DIM, axis=1) + jnp.exp(sk - m)
            res = out / den
            for pair in range(G // 2):
                ev = res[(2 * pair) * BLOCK:(2 * pair + 1) * BLOCK]
                od = res[(2 * pair + 1) * BLOCK:(2 * pair + 2) * BLOCK]
                slabs.append(jnp.where(low, ev, pltpu.roll(od, HEAD_DIM, axis=1)))
        o = jnp.concatenate(slabs, axis=1)
        o_ref[j * BLOCK:(j + 1) * BLOCK, :] = _rms(o, g_ref[...]).astype(BF16)


def _attn(sink, q, kd, va, g, *, seq_len, tq):
    T = q.shape[0]
    tiles_per_seq = seq_len // tq
    rb = tq // BLOCK
    nbb = T // BLOCK
    main = lambda i, s: (i, 0)
    prev = lambda i, s: (jnp.maximum(i * rb - 1, 0), 0)
    nxt = lambda i, s: (jnp.minimum((i + 1) * rb, nbb - 1), 0)
    kvw = 2 * D_KV
    grid_spec = pltpu.PrefetchScalarGridSpec(
        num_scalar_prefetch=1, grid=(T // tq,),
        in_specs=[
            pl.BlockSpec((tq, D_ATTN), main),
            pl.BlockSpec((tq, kvw), main), pl.BlockSpec((BLOCK, kvw), prev), pl.BlockSpec((BLOCK, kvw), nxt),
            pl.BlockSpec((tq, kvw), main), pl.BlockSpec((BLOCK, kvw), prev), pl.BlockSpec((BLOCK, kvw), nxt),
            pl.BlockSpec((1, D_ATTN), lambda i, s: (0, 0)),
        ],
        out_specs=pl.BlockSpec((tq, D_ATTN), main),
        scratch_shapes=[pltpu.VMEM((tq + 2 * BLOCK, kvw), BF16), pltpu.VMEM((tq + 2 * BLOCK, kvw), BF16)],
    )
    return pl.pallas_call(
        functools.partial(_attn_kernel, tq=tq, tiles_per_seq=tiles_per_seq),
        grid_spec=grid_spec, out_shape=jax.ShapeDtypeStruct((T, D_ATTN), BF16),
        compiler_params=_cparams(1), name="attn",
    )(sink, q, kd, kd, kd, va, va, va, g)


def _fft_sizes(seq_len):
    n1h = seq_len // FFT_N2
    k1p = n1h + SUBLANES
    return n1h, k1p


@functools.lru_cache(maxsize=None)
def _dft_consts(seq_len):
    n1h, k1p = _fft_sizes(seq_len)
    n1 = 2 * n1h
    k = np.arange(k1p)[:, None]
    valid = (k <= n1h)
    th = 2.0 * np.pi * ((k * np.arange(n1)[None, :]) % n1) / n1
    w1 = np.concatenate([np.cos(th) * valid, -np.sin(th) * valid], axis=0)
    scale = np.where((k == 0) | (k == n1h), 1.0, 2.0) * valid / n1
    thh = th[:, :n1h]
    winv = np.concatenate([np.cos(thh) * scale, -np.sin(thh) * scale], axis=0).T
    a = 2.0 * np.pi * ((np.arange(FFT_N2)[:, None] * np.arange(FFT_N2)[None, :]) % FFT_N2) / FFT_N2
    cm, sm = np.cos(a), np.sin(a)
    g = np.block([[cm, sm], [-sm, cm]])
    gi = np.block([[cm, -sm], [sm, cm]]) / FFT_N2
    gs = np.concatenate([g[:, FFT_N2:], g[:, :FFT_N2]], axis=1)
    gis = np.concatenate([gi[FFT_N2:], gi[:FFT_N2]], axis=0)
    return w1, winv, g, gs, gi, gis


def _twiddle_tables(seq_len):
    n1h, k1p = _fft_sizes(seq_len)
    n = 2 * seq_len
    k = jnp.arange(k1p, dtype=jnp.int32)[:, None]
    m = jnp.arange(FFT_N2, dtype=jnp.int32)[None, :]
    ph = ((k * m) % n).astype(F32) * (2.0 * math.pi / n)
    cs, sn = jnp.cos(ph), jnp.sin(ph)
    fc = jnp.concatenate([cs, cs], axis=1)
    fs = jnp.concatenate([-sn, sn], axis=1)
    tc = jnp.broadcast_to(cs[:, :, None], (k1p, FFT_N2, LANES))
    ts = jnp.broadcast_to(sn[:, :, None], (k1p, FFT_N2, LANES))
    return fc, fs, tc, ts


def _fft1_kernel(w_ref, x_ref, o_ref, *, k1p):
    a = jnp.einsum('kn,nmc->kmc', w_ref[...], x_ref[...], preferred_element_type=F32)
    o_ref[0] = a[:k1p].astype(BF16)
    o_ref[1] = a[k1p:].astype(BF16)


def _fft1(w1, x, *, k1p):
    B, n1in, _, C = x.shape
    grid = (FFT_N2 // FFT_MG, B, C // LANES)
    return pl.pallas_call(
        functools.partial(_fft1_kernel, k1p=k1p),
        grid=grid,
        in_specs=[
            _resident((2 * k1p, n1in)),
            pl.BlockSpec((None, n1in, FFT_MG, LANES), lambda m, b, j: (b, 0, m, j)),
        ],
        out_specs=pl.BlockSpec((None, 2, k1p, FFT_MG, LANES), lambda m, b, j: (b, 0, 0, m, j)),
        out_shape=jax.ShapeDtypeStruct((B, 2, k1p, FFT_N2, C), BF16),
        compiler_params=_cparams(3), name="fft1",
    )(w1, x)


def _fwd_matrix(g_ref, gs_ref, fc_ref, fs_ref, j):
    return (g_ref[...] * fc_ref[j:j + 1, :] + gs_ref[...] * fs_ref[j:j + 1, :]).astype(BF16)


def _fspec_kernel(g_ref, gs_ref, fc_ref, fs_ref, a_ref, o_ref):
    for j in range(FFT_KB):
        x = jnp.concatenate([a_ref[0, j], a_ref[1, j]], axis=0)
        b = jnp.dot(_fwd_matrix(g_ref, gs_ref, fc_ref, fs_ref, j), x, preferred_element_type=F32)
        o_ref[0, j] = b[:FFT_N2]
        o_ref[1, j] = b[FFT_N2:]


def _fspec(g, gs, fc, fs, a):
    _, k1p, _, C = a.shape
    spec = pl.BlockSpec((2, FFT_KB, FFT_N2, C), lambda kb: (0, kb, 0, 0))
    row = pl.BlockSpec((FFT_KB, 2 * FFT_N2), lambda kb: (kb, 0))
    return pl.pallas_call(
        _fspec_kernel, grid=(k1p // FFT_KB,),
        in_specs=[_resident((2 * FFT_N2, 2 * FFT_N2)), _resident((2 * FFT_N2, 2 * FFT_N2)), row, row, spec],
        out_specs=spec, out_shape=jax.ShapeDtypeStruct(a.shape, F32),
        compiler_params=_cparams(1), name="fspec",
    )(g, gs, fc, fs, a)


def _fft2_kernel(g_ref, gs_ref, gi_ref, gis_ref, fc_ref, fs_ref, tc_ref, ts_ref, a_ref, h_ref, o_ref):
    for j in range(FFT_KB):
        x = jnp.concatenate([a_ref[0, j], a_ref[1, j]], axis=0)
        b = jnp.dot(_fwd_matrix(g_ref, gs_ref, fc_ref, fs_ref, j), x, preferred_element_type=F32)
        br, bi = b[:FFT_N2], b[FFT_N2:]
        hr, hi = h_ref[0, j], h_ref[1, j]
        y = jnp.concatenate([br * hr - bi * hi, br * hi + bi * hr], axis=0).astype(BF16)
        tc, ts = tc_ref[j], ts_ref[j]
        mi = (gi_ref[...] * jnp.concatenate([tc, tc], axis=0)
              + gis_ref[...] * jnp.concatenate([-ts, ts], axis=0)).astype(BF16)
        p = jnp.dot(mi, y, preferred_element_type=F32)
        o_ref[0, j] = p[:FFT_N2].astype(BF16)
        o_ref[1, j] = p[FFT_N2:].astype(BF16)


def _fft2(g, gs, gi, gis, fc, fs, tc, ts, a, hspec):
    B, _, k1p, _, C = a.shape
    mat = _resident((2 * FFT_N2, 2 * FFT_N2))
    row = pl.BlockSpec((FFT_KB, 2 * FFT_N2), lambda kb, b: (kb, 0))
    col = pl.BlockSpec((FFT_KB, FFT_N2, LANES), lambda kb, b: (kb, 0, 0))
    return pl.pallas_call(
        _fft2_kernel, grid=(k1p // FFT_KB, B),
        in_specs=[
            mat, mat, mat, mat, row, row, col, col,
            pl.BlockSpec((None, 2, FFT_KB, FFT_N2, C), lambda kb, b: (b, 0, kb, 0, 0)),
            pl.BlockSpec((2, FFT_KB, FFT_N2, C), lambda kb, b: (0, kb, 0, 0)),
        ],
        out_specs=pl.BlockSpec((None, 2, FFT_KB, FFT_N2, C), lambda kb, b: (b, 0, kb, 0, 0)),
        out_shape=jax.ShapeDtypeStruct(a.shape, BF16),
        compiler_params=_cparams(2), name="fft2",
    )(g, gs, gi, gis, fc, fs, tc, ts, a, hspec)


def _ifft1_kernel(w_ref, a_ref, o_ref):
    a = jnp.concatenate([a_ref[0], a_ref[1]], axis=0)
    o_ref[...] = jnp.einsum('nk,kmc->nmc', w_ref[...], a, preferred_element_type=F32).astype(BF16)


def _ifft1(winv, a):
    B, _, k1p, _, C = a.shape
    n1h = winv.shape[0]
    return pl.pallas_call(
        _ifft1_kernel, grid=(FFT_N2 // FFT_MG, B, C // LANES),
        in_specs=[
            _resident((n1h, 2 * k1p)),
            pl.BlockSpec((None, 2, k1p, FFT_MG, LANES), lambda m, b, j: (b, 0, 0, m, j)),
        ],
        out_specs=pl.BlockSpec((None, n1h, FFT_MG, LANES), lambda m, b, j: (b, 0, m, j)),
        out_shape=jax.ShapeDtypeStruct((B, n1h, FFT_N2, C), BF16),
        compiler_params=_cparams(3), name="ifft1",
    )(winv, a)


def _split_dot(a, b):
    ah = a.astype(BF16)
    al = (a - ah.astype(F32)).astype(BF16)
    bh = b.astype(BF16)
    bl = (b - bh.astype(F32)).astype(BF16)
    d = functools.partial(jnp.dot, preferred_element_type=F32)
    return d(ah, bh) + d(ah, bl) + d(al, bh)


def _filt_kernel(ft_ref, w1a_ref, w1b_ref, b1_ref, q1_ref, w2_ref, b2_ref, q2_ref, w3a_ref, w3b_ref,
                 dl_ref, db_ref, o_ref, *, seq_len, tl):
    i = pl.program_id(0)
    L = seq_len
    hl = tl // 2
    pre = _split_dot(ft_ref[:hl, :], w1a_ref[...]) + _split_dot(ft_ref[hl:, :], w1b_ref[...])
    h = jnp.sin(q1_ref[...] * (pre + b1_ref[...]))
    h = jnp.sin(q2_ref[...] * (_split_dot(h, w2_ref[...]) + b2_ref[...]))
    for half, w3_ref in enumerate((w3a_ref, w3b_ref)):
        hh = _split_dot(h, w3_ref[...])
        n = i * tl + half * hl + lax.broadcasted_iota(jnp.int32, (hl, D_HYENA), 0)
        pos = jnp.where(n < L, n, 2 * L - n).astype(F32)
        decay = jnp.exp(-(pos / (L - 1)) * dl_ref[...])
        val = jnp.where(n == L, 0.0, hh * decay)
        val = val + jnp.where(n == 0, db_ref[...], 0.0)
        o_ref[half * hl:(half + 1) * hl, :] = val.astype(BF16)


def _filt(feat, w1a, w1b, b1, q1, w2, b2, q2, w3a, w3b, dl, db, *, seq_len, tl):
    half = seq_len // tl
    return pl.pallas_call(
        functools.partial(_filt_kernel, seq_len=seq_len, tl=tl),
        grid=(2 * half,),
        in_specs=[
            pl.BlockSpec((tl, LANES), lambda i: (i, 0)),
            _resident((LANES, LANES)), _resident((LANES, LANES)), _resident((1, LANES)), _resident((1, LANES)),
            _resident((LANES, LANES)), _resident((1, LANES)), _resident((1, LANES)),
            pl.BlockSpec((LANES, D_HYENA), lambda i: (0, i // half)),
            pl.BlockSpec((LANES, D_HYENA), lambda i: (0, i // half)),
            _resident((1, D_HYENA)), _resident((1, D_HYENA)),
        ],
        out_specs=pl.BlockSpec((tl, D_HYENA), lambda i: (i, 0)),
        out_shape=jax.ShapeDtypeStruct((2 * seq_len, D_HYENA), BF16),
        compiler_params=_cparams(1), name="filt",
    )(feat, w1a, w1b, b1, q1, w2, b2, q2, w3a, w3b, dl, db)


def _filter_features(seq_len):
    L = seq_len
    bands = (FILTER_EMB - 1) // 2
    n = jnp.arange(2 * L, dtype=jnp.int32)
    pos = jnp.where(n < L, n, 2 * L - n).astype(F32)[:, None]
    t = pos / (L - 1)
    w = 2.0 * math.pi * pos / L
    f = jnp.linspace(1e-4, bands - 1, bands, dtype=F32)[None]
    pad = jnp.zeros((2 * L, LANES - FILTER_EMB), F32)
    return jnp.concatenate([t, jnp.cos(f * w), -jnp.sin(f * w), pad], axis=1)


FF_CHUNK = D_FF // 2


def _mixffn_kernel(hm_ref, hp_ref, hn_ref, am_ref, ap_ref, an_ref, xm_ref, xp_ref, xn_ref,
                   ym_ref, yp_ref, yn_ref, gh_ref, wo_ref, gf_ref, wu_ref, wc_ref, bc_ref,
                   h1_ref, act_ref, *, tm, tiles_per_seq):
    i = pl.program_id(0)
    first = (i % tiles_per_seq) == 0
    last = (i % tiles_per_seq) == tiles_per_seq - 1

    def stack(m_ref, p_ref, n_ref):
        p = p_ref[...]
        n = n_ref[...]
        p = jnp.where(first, jnp.zeros_like(p), p)
        n = jnp.where(last, jnp.zeros_like(n), n)
        return jnp.concatenate([m_ref[...], n, p], axis=0)

    h = stack(hm_ref, hp_ref, hn_ref)
    a = stack(am_ref, ap_ref, an_ref)
    hy = stack(xm_ref, xp_ref, xn_ref).astype(F32) * stack(ym_ref, yp_ref, yn_ref).astype(F32)
    hyn = _rms(hy, gh_ref[...]).astype(BF16)
    h1 = (h + jnp.dot(a, wo_ref[:D_ATTN, :], preferred_element_type=F32)
          + jnp.dot(hyn, wo_ref[D_ATTN:, :], preferred_element_type=F32))
    h1_ref[...] = h1[:tm]
    n2 = _rms(h1, gf_ref[...]).astype(BF16)

    def conv(c0):
        u = jnp.dot(n2, wu_ref[:, c0:c0 + FF_CHUNK], preferred_element_type=F32)
        return _conv3(u, tm, wc_ref, bc_ref, c0, c0 + FF_CHUNK)

    for cidx in range(D_FF // FF_CHUNK):
        c0 = cidx * FF_CHUNK
        ua = conv(c0)
        ug = conv(D_FF + c0)
        act_ref[:, c0:c0 + FF_CHUNK] = (ug * jax.nn.sigmoid(ug) * ua).astype(BF16)


def _mixffn(h, attn, x0, y, gh, wo, gf, wu, wc, bc, *, seq_len, tm):
    T = h.shape[0]
    tiles_per_seq = seq_len // tm
    r16 = tm // BF16_ROWS
    nb16 = T // BF16_ROWS
    main = lambda i: (i, 0)
    prev = lambda i: (jnp.maximum(i * r16 - 1, 0), 0)
    nxt = lambda i: (jnp.minimum((i + 1) * r16, nb16 - 1), 0)

    def trio(width):
        return [pl.BlockSpec((tm, width), main), pl.BlockSpec((BF16_ROWS, width), prev),
                pl.BlockSpec((BF16_ROWS, width), nxt)]

    in_specs = (trio(D_MODEL) + trio(D_ATTN) + trio(D_HYENA) + trio(D_HYENA) + [
        _resident((1, D_HYENA)),
        _resident((D_MODEL, D_MODEL)),
        _resident((1, D_MODEL)),
        _resident((D_MODEL, 2 * D_FF)),
        _resident((3, 2 * D_FF)),
        _resident((1, 2 * D_FF)),
    ])
    return pl.pallas_call(
        functools.partial(_mixffn_kernel, tm=tm, tiles_per_seq=tiles_per_seq),
        grid=(T // tm,), in_specs=in_specs,
        out_specs=[pl.BlockSpec((tm, D_MODEL), main), pl.BlockSpec((tm, D_FF), main)],
        out_shape=[jax.ShapeDtypeStruct((T, D_MODEL), F32), jax.ShapeDtypeStruct((T, D_FF), BF16)],
        compiler_params=_cparams(1), name="mixffn",
    )(h, h, h, attn, attn, attn, x0, x0, x0, y, y, y, gh, wo, gf, wu, wc, bc)


def _down_kernel(h_ref, act_ref, p_ref, wd_ref, wg_ref, wp_ref, o_ref):
    h2 = h_ref[...] + jnp.dot(act_ref[...], wd_ref[...], preferred_element_type=F32)
    gate = jax.nn.sigmoid(jnp.dot(h2.astype(BF16), wg_ref[...], preferred_element_type=F32))
    pp = jnp.dot(p_ref[...].astype(BF16), wp_ref[...], preferred_element_type=F32)
    o_ref[...] = h2 + gate * pp


def _down(h1, act, p, wd, wg, wp, *, tm, p_tile0):
    T = h1.shape[0]
    main = lambda i: (i, 0)
    return pl.pallas_call(
        _down_kernel, grid=(T // tm,),
        in_specs=[pl.BlockSpec((tm, D_MODEL), main), pl.BlockSpec((tm, D_FF), main),
                  pl.BlockSpec((tm, D_PLE), lambda i: (p_tile0 + i, 0)),
                  _resident((D_FF, D_MODEL)), _resident((D_MODEL, D_MODEL)), _resident((D_PLE, D_MODEL))],
        out_specs=pl.BlockSpec((tm, D_MODEL), main),
        out_shape=jax.ShapeDtypeStruct((T, D_MODEL), F32),
        compiler_params=_cparams(1), name="down",
    )(h1, act, p, wd, wg, wp)


def _rope_slabs(seq_len):
    half = ROPE_DIM // 2
    inv = ROPE_THETA ** (-jnp.arange(0, ROPE_DIM, 2, dtype=F32) / ROPE_DIM)
    ang = jnp.arange(seq_len, dtype=F32)[:, None] * inv[None]
    cos, sin = jnp.cos(ang), jnp.sin(ang)
    ones = jnp.ones((seq_len, HEAD_DIM - ROPE_DIM), F32)
    zeros = jnp.zeros((seq_len, HEAD_DIM - ROPE_DIM), F32)
    zh = jnp.zeros((seq_len, half), F32)
    rc = jnp.concatenate([cos, cos, ones], axis=1)
    rs1 = jnp.concatenate([-sin, zh, zeros], axis=1)
    rs2 = jnp.concatenate([zh, sin, zeros], axis=1)
    rep = LANES // HEAD_DIM
    return tuple(jnp.tile(a, (1, rep)) for a in (rc, rs1, rs2))


def _group_consts(seq_len):
    n1h, k1p = _fft_sizes(seq_len)
    w1, winv, g, gs, gi, gis = _dft_consts(seq_len)
    fc, fs, tc, ts = _twiddle_tables(seq_len)
    return dict(
        seq_len=seq_len, n1h=n1h, k1p=k1p,
        w1_full=jnp.asarray(w1, BF16), w1_half=jnp.asarray(w1[:, :n1h], BF16),
        winv=jnp.asarray(winv, BF16),
        g=jnp.asarray(g, F32), gs=jnp.asarray(gs, F32), gi=jnp.asarray(gi, F32), gis=jnp.asarray(gis, F32),
        fc=fc, fs=fs, tc=tc, ts=ts, rope=_rope_slabs(seq_len), feat=_filter_features(seq_len),
    )


def _pair(v):
    return jnp.concatenate([v, v])[None]


def _layer_weights(i, rms_mix, w_in, q_norm, k_norm, sink, w_short, b_short, filt_w1, filt_b1, filt_freq1,
                   filt_w2, filt_b2, filt_freq2, filt_w3, hyena_bias, norm_attn_out, norm_hyena_out,
                   w_out, rms_ffn, w_up, w_ffconv, b_ffconv, w_down, w_ple_gate, w_ple_proj):
    rep_q = D_ATTN // HEAD_DIM
    rep_k = D_KV // HEAD_DIM
    hid = FILTER_HID
    w1p = jnp.zeros((LANES, hid), F32).at[:FILTER_EMB].set(filt_w1[i])
    zpad = jnp.zeros((LANES, hid), F32)
    zsq = jnp.zeros((hid, hid), F32)
    z3 = jnp.zeros((hid, 2 * D_HYENA), F32)
    deltas = jnp.abs(jnp.linspace(MIN_DECAY, MAX_DECAY, D_HYENA, dtype=F32))
    return dict(
        g_mix=rms_mix[i][None], w_in=w_in[i].astype(BF16),
        qg=(jnp.tile(q_norm[i], rep_q) * (HEAD_DIM ** -0.5))[None], kg=jnp.tile(k_norm[i], rep_k)[None],
        sink=sink[i], ws=w_short[i], bs=b_short[i][None],
        fw1a=jnp.concatenate([w1p, zpad], axis=1), fw1b=jnp.concatenate([zpad, w1p], axis=1),
        fb1=_pair(filt_b1[i]), fq1=_pair(filt_freq1[i]),
        fw2=jnp.block([[filt_w2[i], zsq], [zsq, filt_w2[i]]]),
        fb2=_pair(filt_b2[i]), fq2=_pair(filt_freq2[i]),
        fw3a=jnp.concatenate([filt_w3[i], z3], axis=0), fw3b=jnp.concatenate([z3, filt_w3[i]], axis=0),
        deltas=deltas[None], dbias=hyena_bias[i][None],
        g_attn=norm_attn_out[i][None], g_hy=norm_hyena_out[i][None],
        w_out=w_out[i].astype(BF16), g_ffn=rms_ffn[i][None], w_up=w_up[i].astype(BF16),
        wc=w_ffconv[i], bc=b_ffconv[i][None], w_down=w_down[i].astype(BF16),
        w_gate=w_ple_gate[i].astype(BF16), w_proj=w_ple_proj[i].astype(BF16),
    )


def _layer(h, p_all, layer, lw, gc, bd, batch, *, tm, tq, tl):
    L = gc["seq_len"]
    T = batch * L
    n1h, k1p = gc["n1h"], gc["k1p"]
    rc, rs1, rs2 = gc["rope"]
    q, kd, va, x0, vv = _inproj(h, lw["g_mix"], lw["w_in"], lw["qg"], lw["kg"], bd, rc, rs1, rs2,
                                lw["ws"], lw["bs"], seq_len=L, tm=tm)
    attn = _attn(lw["sink"], q, kd, va, lw["g_attn"], seq_len=L, tq=tq)
    kfull = _filt(gc["feat"], lw["fw1a"], lw["fw1b"], lw["fb1"], lw["fq1"], lw["fw2"], lw["fb2"], lw["fq2"],
                  lw["fw3a"], lw["fw3b"], lw["deltas"], lw["dbias"], seq_len=L, tl=tl)
    fa = _fft1(gc["w1_full"], kfull.reshape(1, 2 * n1h, FFT_N2, D_HYENA), k1p=k1p)
    hspec = _fspec(gc["g"], gc["gs"], gc["fc"], gc["fs"], fa[0])
    a = _fft1(gc["w1_half"], vv.reshape(batch, n1h, FFT_N2, D_HYENA), k1p=k1p)
    a = _fft2(gc["g"], gc["gs"], gc["gi"], gc["gis"], gc["fc"], gc["fs"], gc["tc"], gc["ts"], a, hspec)
    y = _ifft1(gc["winv"], a).reshape(T, D_HYENA)
    h1, act = _mixffn(h, attn, x0, y, lw["g_hy"], lw["w_out"], lw["g_ffn"], lw["w_up"], lw["wc"], lw["bc"],
                      seq_len=L, tm=tm)
    return _down(h1, act, p_all, lw["w_down"], lw["w_gate"], lw["w_proj"], tm=tm, p_tile0=layer * (T // tm))


def _trunk(x, p, weights, *, tm=512, tq=512, tl=512):
    batch, L, _ = x.shape
    gc = _group_consts(L)
    bd = jnp.asarray(np.kron(np.eye(D_ATTN // HEAD_DIM), np.full((HEAD_DIM, HEAD_DIM), 1.0 / HEAD_DIM)), BF16)
    h = x.reshape(batch * L, D_MODEL)
    p_all = p.reshape(DEPTH * batch * L, D_PLE)
    for i in range(DEPTH):
        lw = _layer_weights(i, *weights)
        h = _layer(h, p_all, i, lw, gc, bd, batch, tm=tm, tq=tq, tl=tl)
    return h.reshape(batch, L, D_MODEL)


def kernel(x_prompt, x_sample, p_prompt, p_sample, rms_mix, w_in, q_norm, k_norm, sink, w_short, b_short, filt_w1, filt_b1, filt_freq1, filt_w2, filt_b2, filt_freq2, filt_w3, hyena_bias, norm_attn_out, norm_hyena_out, w_out, rms_ffn, w_up, w_ffconv, b_ffconv, w_down, w_ple_gate, w_ple_proj):
    weights = (rms_mix, w_in, q_norm, k_norm, sink, w_short, b_short, filt_w1, filt_b1, filt_freq1,
               filt_w2, filt_b2, filt_freq2, filt_w3, hyena_bias, norm_attn_out, norm_hyena_out,
               w_out, rms_ffn, w_up, w_ffconv, b_ffconv, w_down, w_ple_gate, w_ple_proj)
    y_prompt = _trunk(x_prompt, p_prompt, weights)
    y_sample = _trunk(x_sample, p_sample, weights)
    return (y_prompt, y_sample)
```

```python
import functools
import math

import numpy as np
import jax
import jax.numpy as jnp
from jax import lax
from jax.experimental import pallas as pl
from jax.experimental.pallas import tpu as pltpu

F32 = jnp.float32
BF16 = jnp.bfloat16

D_MODEL = 1024
DEPTH = 4
N_Q_HEADS = 8
N_KV_HEADS = 2
HEAD_DIM = 64
D_ATTN = N_Q_HEADS * HEAD_DIM
D_KV = N_KV_HEADS * HEAD_DIM
D_QKV = D_ATTN + 2 * D_KV
WINDOW = 128
BLOCK = 128
ROPE_THETA = 500000.0
ROPE_DIM = HEAD_DIM // 4
D_HYENA = 512
FILTER_EMB = 33
FILTER_HID = 64
FAST_DECAY_PCT = 0.3
SLOW_DECAY_PCT = 1.5
DECAY_TARGET = 1e-2
MAX_DECAY = math.log(DECAY_TARGET) / FAST_DECAY_PCT
MIN_DECAY = math.log(DECAY_TARGET) / SLOW_DECAY_PCT
D_IN = D_QKV + 3 * D_HYENA
D_FF = 2816
D_PLE = 256
EPS = 1e-6
MASKED = -1e30

LANES = 128
SUBLANES = 8
BF16_ROWS = 16
FFT_N2 = 64
FFT_MG = 16
FFT_KB = 8
VMEM_LIMIT = 56 * 1024 * 1024


def _cparams(n_axes):
    return pltpu.CompilerParams(dimension_semantics=("arbitrary",) * n_axes,
                                vmem_limit_bytes=VMEM_LIMIT)


def _resident(shape):
    return pl.BlockSpec(shape, lambda *_: (0,) * len(shape), pipeline_mode=pl.Buffered(1))


def _rms(x, g):
    ms = jnp.mean(x * x, axis=-1, keepdims=True)
    return x * lax.rsqrt(ms + EPS) * g


def _conv3(u, tm, w_ref, b_ref, c0, c1):
    rows = u.shape[0]
    up = pltpu.roll(u, 1, axis=0)[:tm]
    un = pltpu.roll(u, rows - 1, axis=0)[:tm]
    return up * w_ref[0:1, c0:c1] + u[:tm] * w_ref[1:2, c0:c1] + un * w_ref[2:3, c0:c1] + b_ref[:, c0:c1]


def _inproj_kernel(xm_ref, xp_ref, xn_ref, g_ref, w_ref, qg_ref, kg_ref, bd_ref,
                   rc_ref, rs1_ref, rs2_ref, ws_ref, bs_ref,
                   q_ref, kd_ref, va_ref, x0_ref, vv_ref, *, tm, tiles_per_seq):
    i = pl.program_id(0)
    first = (i % tiles_per_seq) == 0
    last = (i % tiles_per_seq) == tiles_per_seq - 1
    xp = jnp.where(first, 0.0, xp_ref[...])
    xn = jnp.where(last, 0.0, xn_ref[...])
    x = jnp.concatenate([xm_ref[...], xn, xp], axis=0)
    n = _rms(x, g_ref[...]).astype(BF16)
    z = jnp.dot(n, w_ref[...], preferred_element_type=F32)

    bd = bd_ref[...]
    rc, rs1, rs2 = rc_ref[...], rs1_ref[...], rs2_ref[...]

    def norm_rope(zs, gain, bds):
        ms = jnp.dot((zs * zs).astype(BF16), bds, preferred_element_type=F32)
        y = zs * lax.rsqrt(ms + EPS) * gain
        outs = []
        for s in range(y.shape[1] // LANES):
            ys = y[:, s * LANES:(s + 1) * LANES]
            outs.append(ys * rc + pltpu.roll(ys, LANES - ROPE_DIM // 2, axis=1) * rs1
                        + pltpu.roll(ys, ROPE_DIM // 2, axis=1) * rs2)
        return outs

    qs = norm_rope(z[:tm, :D_ATTN], qg_ref[...], bd)
    for s, qv in enumerate(qs):
        q_ref[:, s * LANES:(s + 1) * LANES] = qv.astype(BF16)
    k = norm_rope(z[:tm, D_ATTN:D_ATTN + D_KV], kg_ref[...], bd[:D_KV, :D_KV])[0]
    v = z[:tm, D_ATTN + D_KV:D_QKV]
    low = lax.broadcasted_iota(jnp.int32, (tm, LANES), 1) < HEAD_DIM
    kr = pltpu.roll(k, HEAD_DIM, axis=1)
    vr = pltpu.roll(v, HEAD_DIM, axis=1)
    kd_ref[:, :LANES] = jnp.where(low, k, kr).astype(BF16)
    kd_ref[:, LANES:] = jnp.where(low, kr, k).astype(BF16)
    va_ref[:, :LANES] = jnp.where(low, v, 1.0).astype(BF16)
    va_ref[:, LANES:] = jnp.where(low, vr, 1.0).astype(BF16)

    u = _conv3(z[:, D_QKV:], tm, ws_ref, bs_ref, 0, 3 * D_HYENA)
    x0_ref[...] = u[:, :D_HYENA].astype(BF16)
    vv_ref[...] = (u[:, D_HYENA:2 * D_HYENA] * u[:, 2 * D_HYENA:]).astype(BF16)


def _inproj(x, g, w, qg, kg, bd, rc, rs1, rs2, ws, bs, *, seq_len, tm):
    T = x.shape[0]
    tiles_per_seq = seq_len // tm
    r8 = tm // SUBLANES
    nb8 = T // SUBLANES
    rope_spec = pl.BlockSpec((tm, LANES), lambda i: (i % tiles_per_seq, 0))
    in_specs = [
        pl.BlockSpec((tm, D_MODEL), lambda i: (i, 0)),
        pl.BlockSpec((SUBLANES, D_MODEL), lambda i: (jnp.maximum(i * r8 - 1, 0), 0)),
        pl.BlockSpec((SUBLANES, D_MODEL), lambda i: (jnp.minimum((i + 1) * r8, nb8 - 1), 0)),
        _resident((1, D_MODEL)),
        _resident((D_MODEL, D_IN)),
        _resident((1, D_ATTN)),
        _resident((1, D_KV)),
        _resident((D_ATTN, D_ATTN)),
        rope_spec, rope_spec, rope_spec,
        _resident((3, 3 * D_HYENA)),
        _resident((1, 3 * D_HYENA)),
    ]
    widths = (D_ATTN, 2 * D_KV, 2 * D_KV, D_HYENA, D_HYENA)
    out_specs = [pl.BlockSpec((tm, wd), lambda i: (i, 0)) for wd in widths]
    out_shape = [jax.ShapeDtypeStruct((T, wd), BF16) for wd in widths]
    return pl.pallas_call(
        functools.partial(_inproj_kernel, tm=tm, tiles_per_seq=tiles_per_seq),
        grid=(T // tm,), in_specs=in_specs, out_specs=out_specs, out_shape=out_shape,
        compiler_params=_cparams(1), name="inproj",
    )(x, x, x, g, w, qg, kg, bd, rc, rs1, rs2, ws, bs)


def _attn_kernel(sink_ref, q_ref, km_ref, kp_ref, kn_ref, vm_ref, vp_ref, vn_ref, g_ref,
                 o_ref, kbuf, vbuf, *, tq, tiles_per_seq):
    i = pl.program_id(0)
    first = (i % tiles_per_seq) == 0
    last = (i % tiles_per_seq) == tiles_per_seq - 1
    nblk = tq // BLOCK
    G = N_Q_HEADS // N_KV_HEADS
    kbuf[0:BLOCK] = kp_ref[...]
    kbuf[BLOCK:BLOCK + tq] = km_ref[...]
    kbuf[BLOCK + tq:] = kn_ref[...]
    vbuf[0:BLOCK] = vp_ref[...]
    vbuf[BLOCK:BLOCK + tq] = vm_ref[...]
    vbuf[BLOCK + tq:] = vn_ref[...]
    r = lax.broadcasted_iota(jnp.int32, (BLOCK, 3 * BLOCK), 0)
    c = lax.broadcasted_iota(jnp.int32, (BLOCK, 3 * BLOCK), 1)
    d = c - BLOCK - r
    band = jnp.where((d >= -WINDOW) & (d <= WINDOW), 0.0, MASKED)
    low = lax.broadcasted_iota(jnp.int32, (BLOCK, LANES), 1) < HEAD_DIM
    mlo = jnp.where(low, 1.0, 0.0).astype(BF16)
    mhi = jnp.where(low, 0.0, 1.0).astype(BF16)
    rid = lax.broadcasted_iota(jnp.int32, (G * BLOCK, 1), 0)
    for j in range(nblk):
        bias = band
        if j == 0:
            bias = jnp.where(first & (c < BLOCK), MASKED, bias)
        if j == nblk - 1:
            bias = jnp.where(last & (c >= 2 * BLOCK), MASKED, bias)
        slabs = []
        for hk in range(N_KV_HEADS):
            kb = kbuf[j * BLOCK:(j + 3) * BLOCK, hk * LANES:(hk + 1) * LANES]
            vb = vbuf[j * BLOCK:(j + 3) * BLOCK, hk * LANES:(hk + 1) * LANES]
            c0 = hk * G * HEAD_DIM
            qa = q_ref[j * BLOCK:(j + 1) * BLOCK, c0:c0 + LANES]
            qb = q_ref[j * BLOCK:(j + 1) * BLOCK, c0 + LANES:c0 + 2 * LANES]
            ql = jnp.concatenate([qa * mlo, qa * mhi, qb * mlo, qb * mhi], axis=0)
            s = lax.dot_general(ql, kb, (((1,), (1,)), ((), ())), preferred_element_type=F32)
            s = (s.reshape(G, BLOCK, 3 * BLOCK) + bias[None]).reshape(G * BLOCK, 3 * BLOCK)
            sk = jnp.where(rid < BLOCK, sink_ref[hk * G],
                           jnp.where(rid < 2 * BLOCK, sink_ref[hk * G + 1],
                                     jnp.where(rid < 3 * BLOCK, sink_ref[hk * G + 2], sink_ref[hk * G + 3])))
            m = jnp.maximum(jnp.max(s, axis=-1, keepdims=True), sk)
            e = jnp.exp((s - m).astype(BF16))
            out = jnp.dot(e, vb, preferred_element_type=F32)
            den = pltpu.roll(out, HEAD_DIM, axis=1) + jnp.exp(sk - m)
            res = out / den
            for pair in range(G // 2):
                ev = res[(2 * pair) * BLOCK:(2 * pair + 1) * BLOCK]
                od = res[(2 * pair + 1) * BLOCK:(2 * pair + 2) * BLOCK]
                slabs.append(jnp.where(low, ev, pltpu.roll(od, HEAD_DIM, axis=1)))
        o = jnp.concatenate(slabs, axis=1)
        o_ref[j * BLOCK:(j + 1) * BLOCK, :] = _rms(o, g_ref[...]).astype(BF16)


def _attn(sink, q, kd, va, g, *, seq_len, tq):
    T = q.shape[0]
    tiles_per_seq = seq_len // tq
    rb = tq // BLOCK
    nbb = T // BLOCK
    main = lambda i, s: (i, 0)
    prev = lambda i, s: (jnp.maximum(i * rb - 1, 0), 0)
    nxt = lambda i, s: (jnp.minimum((i + 1) * rb, nbb - 1), 0)
    kvw = 2 * D_KV
    grid_spec = pltpu.PrefetchScalarGridSpec(
        num_scalar_prefetch=1, grid=(T // tq,),
        in_specs=[
            pl.BlockSpec((tq, D_ATTN), main),
            pl.BlockSpec((tq, kvw), main), pl.BlockSpec((BLOCK, kvw), prev), pl.BlockSpec((BLOCK, kvw), nxt),
            pl.BlockSpec((tq, kvw), main), pl.BlockSpec((BLOCK, kvw), prev), pl.BlockSpec((BLOCK, kvw), nxt),
            pl.BlockSpec((1, D_ATTN), lambda i, s: (0, 0)),
        ],
        out_specs=pl.BlockSpec((tq, D_ATTN), main),
        scratch_shapes=[pltpu.VMEM((tq + 2 * BLOCK, kvw), BF16), pltpu.VMEM((tq + 2 * BLOCK, kvw), BF16)],
    )
    return pl.pallas_call(
        functools.partial(_attn_kernel, tq=tq, tiles_per_seq=tiles_per_seq),
        grid_spec=grid_spec, out_shape=jax.ShapeDtypeStruct((T, D_ATTN), BF16),
        compiler_params=_cparams(1), name="attn",
    )(sink, q, kd, kd, kd, va, va, va, g)


def _fft_sizes(seq_len):
    n1h = seq_len // FFT_N2
    k1p = n1h + SUBLANES
    return n1h, k1p


@functools.lru_cache(maxsize=None)
def _dft_consts(seq_len):
    n1h, k1p = _fft_sizes(seq_len)
    n1 = 2 * n1h
    k = np.arange(k1p)[:, None]
    valid = (k <= n1h)
    th = 2.0 * np.pi * ((k * np.arange(n1)[None, :]) % n1) / n1
    w1 = np.concatenate([np.cos(th) * valid, -np.sin(th) * valid], axis=0)
    scale = np.where((k == 0) | (k == n1h), 1.0, 2.0) * valid / n1
    thh = th[:, :n1h]
    winv = np.concatenate([np.cos(thh) * scale, -np.sin(thh) * scale], axis=0).T
    a = 2.0 * np.pi * ((np.arange(FFT_N2)[:, None] * np.arange(FFT_N2)[None, :]) % FFT_N2) / FFT_N2
    cm, sm = np.cos(a), np.sin(a)
    g = np.block([[cm, sm], [-sm, cm]])
    gi = np.block([[cm, -sm], [sm, cm]]) / FFT_N2
    gs = np.concatenate([g[:, FFT_N2:], g[:, :FFT_N2]], axis=1)
    gis = np.concatenate([gi[FFT_N2:], gi[:FFT_N2]], axis=0)
    return w1, winv, g, gs, gi, gis


def _twiddle_tables(seq_len):
    n1h, k1p = _fft_sizes(seq_len)
    n = 2 * seq_len
    k = jnp.arange(k1p, dtype=jnp.int32)[:, None]
    m = jnp.arange(FFT_N2, dtype=jnp.int32)[None, :]
    ph = ((k * m) % n).astype(F32) * (2.0 * math.pi / n)
    cs, sn = jnp.cos(ph), jnp.sin(ph)
    fc = jnp.concatenate([cs, cs], axis=1)
    fs = jnp.concatenate([-sn, sn], axis=1)
    tc = jnp.broadcast_to(cs[:, :, None], (k1p, FFT_N2, LANES))
    ts = jnp.broadcast_to(sn[:, :, None], (k1p, FFT_N2, LANES))
    return fc, fs, tc, ts


def _fft1_kernel(w_ref, x_ref, o_ref, *, k1p):
    a = jnp.einsum('kn,nmc->kmc', w_ref[...], x_ref[...], preferred_element_type=F32)
    o_ref[0] = a[:k1p].astype(BF16)
    o_ref[1] = a[k1p:].astype(BF16)


def _fft1(w1, x, *, k1p):
    B, n1in, _, C = x.shape
    grid = (FFT_N2 // FFT_MG, B, C // LANES)
    return pl.pallas_call(
        functools.partial(_fft1_kernel, k1p=k1p),
        grid=grid,
        in_specs=[
            _resident((2 * k1p, n1in)),
            pl.BlockSpec((None, n1in, FFT_MG, LANES), lambda m, b, j: (b, 0, m, j)),
        ],
        out_specs=pl.BlockSpec((None, 2, k1p, FFT_MG, LANES), lambda m, b, j: (b, 0, 0, m, j)),
        out_shape=jax.ShapeDtypeStruct((B, 2, k1p, FFT_N2, C), BF16),
        compiler_params=_cparams(3), name="fft1",
    )(w1, x)


def _pair_diag(a, b):
    z = jnp.zeros_like(a)
    return jnp.concatenate([jnp.concatenate([a, z], axis=1), jnp.concatenate([z, b], axis=1)], axis=0).astype(BF16)


def _fwd_pair(g_ref, gs_ref, fc_ref, fs_ref, a_ref, j):
    def mat(jj):
        return g_ref[...] * fc_ref[jj:jj + 1, :] + gs_ref[...] * fs_ref[jj:jj + 1, :]
    x = jnp.concatenate([a_ref[0, j], a_ref[1, j], a_ref[0, j + 1], a_ref[1, j + 1]], axis=0)
    return jnp.dot(_pair_diag(mat(j), mat(j + 1)), x, preferred_element_type=F32)


def _fspec_kernel(g_ref, gs_ref, fc_ref, fs_ref, a_ref, o_ref):
    n = FFT_N2
    for j in range(0, FFT_KB, 2):
        bf = _fwd_pair(g_ref, gs_ref, fc_ref, fs_ref, a_ref.at[0], j)
        bb = _fwd_pair(g_ref, gs_ref, fc_ref, fs_ref, a_ref.at[1], j)
        for t in range(2):
            re = slice(2 * t * n, (2 * t + 1) * n)
            im = slice((2 * t + 1) * n, (2 * t + 2) * n)
            o_ref[0, j + t] = (bf[re] + bb[re]).astype(BF16)
            o_ref[1, j + t] = (bf[im] - bb[im]).astype(BF16)


def _fspec(g, gs, fc, fs, a):
    _, _, k1p, _, C = a.shape
    row = pl.BlockSpec((FFT_KB, 2 * FFT_N2), lambda kb: (kb, 0))
    return pl.pallas_call(
        _fspec_kernel, grid=(k1p // FFT_KB,),
        in_specs=[_resident((2 * FFT_N2, 2 * FFT_N2)), _resident((2 * FFT_N2, 2 * FFT_N2)), row, row,
                  pl.BlockSpec((2, 2, FFT_KB, FFT_N2, C), lambda kb: (0, 0, kb, 0, 0))],
        out_specs=pl.BlockSpec((2, FFT_KB, FFT_N2, C), lambda kb: (0, kb, 0, 0)),
        out_shape=jax.ShapeDtypeStruct(a.shape[1:], BF16),
        compiler_params=_cparams(1), name="fspec",
    )(g, gs, fc, fs, a)


def _fft2_kernel(g_ref, gs_ref, gi_ref, gis_ref, fc_ref, fs_ref, tc_ref, ts_ref, a_ref, h_ref, o_ref):
    n = FFT_N2

    def inv(jj):
        tc, ts = tc_ref[jj], ts_ref[jj]
        return (gi_ref[...] * jnp.concatenate([tc, tc], axis=0)
                + gis_ref[...] * jnp.concatenate([-ts, ts], axis=0))

    for j in range(0, FFT_KB, 2):
        b = _fwd_pair(g_ref, gs_ref, fc_ref, fs_ref, a_ref, j)
        ys = []
        for t in range(2):
            br, bi = b[2 * t * n:(2 * t + 1) * n], b[(2 * t + 1) * n:(2 * t + 2) * n]
            hr, hi = h_ref[0, j + t].astype(F32), h_ref[1, j + t].astype(F32)
            ys += [br * hr - bi * hi, br * hi + bi * hr]
        y = jnp.concatenate(ys, axis=0).astype(BF16)
        p = jnp.dot(_pair_diag(inv(j), inv(j + 1)), y, preferred_element_type=F32)
        for t in range(2):
            o_ref[0, j + t] = p[2 * t * n:(2 * t + 1) * n].astype(BF16)
            o_ref[1, j + t] = p[(2 * t + 1) * n:(2 * t + 2) * n].astype(BF16)


def _fft2(g, gs, gi, gis, fc, fs, tc, ts, a, hspec):
    B, _, k1p, _, C = a.shape
    mat = _resident((2 * FFT_N2, 2 * FFT_N2))
    row = pl.BlockSpec((FFT_KB, 2 * FFT_N2), lambda kb, b: (kb, 0))
    col = pl.BlockSpec((FFT_KB, FFT_N2, LANES), lambda kb, b: (kb, 0, 0))
    return pl.pallas_call(
        _fft2_kernel, grid=(k1p // FFT_KB, B),
        in_specs=[
            mat, mat, mat, mat, row, row, col, col,
            pl.BlockSpec((None, 2, FFT_KB, FFT_N2, C), lambda kb, b: (b, 0, kb, 0, 0)),
            pl.BlockSpec((2, FFT_KB, FFT_N2, C), lambda kb, b: (0, kb, 0, 0)),
        ],
        out_specs=pl.BlockSpec((None, 2, FFT_KB, FFT_N2, C), lambda kb, b: (b, 0, kb, 0, 0)),
        out_shape=jax.ShapeDtypeStruct(a.shape, BF16),
        compiler_params=_cparams(2), name="fft2",
    )(g, gs, gi, gis, fc, fs, tc, ts, a, hspec)


def _ifft1_kernel(w_ref, a_ref, o_ref):
    a = jnp.concatenate([a_ref[0], a_ref[1]], axis=0)
    o_ref[...] = jnp.einsum('nk,kmc->nmc', w_ref[...], a, preferred_element_type=F32).astype(BF16)


def _ifft1(winv, a):
    B, _, k1p, _, C = a.shape
    n1h = winv.shape[0]
    return pl.pallas_call(
        _ifft1_kernel, grid=(FFT_N2 // FFT_MG, B, C // LANES),
        in_specs=[
            _resident((n1h, 2 * k1p)),
            pl.BlockSpec((None, 2, k1p, FFT_MG, LANES), lambda m, b, j: (b, 0, 0, m, j)),
        ],
        out_specs=pl.BlockSpec((None, n1h, FFT_MG, LANES), lambda m, b, j: (b, 0, m, j)),
        out_shape=jax.ShapeDtypeStruct((B, n1h, FFT_N2, C), BF16),
        compiler_params=_cparams(3), name="ifft1",
    )(winv, a)


def _split_dot(a, b):
    ah = a.astype(BF16)
    al = (a - ah.astype(F32)).astype(BF16)
    bh = b.astype(BF16)
    bl = (b - bh.astype(F32)).astype(BF16)
    d = functools.partial(jnp.dot, preferred_element_type=F32)
    return d(ah, bh) + d(ah, bl) + d(al, bh)


def _filt_kernel(ft_ref, w1a_ref, w1b_ref, b1_ref, q1_ref, w2_ref, b2_ref, q2_ref, w3a_ref, w3b_ref,
                 dl_ref, db_ref, o_ref, *, seq_len, tl):
    i = pl.program_id(0)
    L = seq_len
    hl = tl // 2
    pre = _split_dot(ft_ref[:hl, :], w1a_ref[...]) + _split_dot(ft_ref[hl:, :], w1b_ref[...])
    h = jnp.sin(q1_ref[...] * (pre + b1_ref[...]))
    h = jnp.sin(q2_ref[...] * (_split_dot(h, w2_ref[...]) + b2_ref[...]))
    for half, w3_ref in enumerate((w3a_ref, w3b_ref)):
        hh = _split_dot(h, w3_ref[...])
        n = i * tl + half * hl + lax.broadcasted_iota(jnp.int32, (hl, D_HYENA), 0)
        decay = jnp.exp(-(n.astype(F32) / (L - 1)) * dl_ref[...])
        rows = slice(half * hl, (half + 1) * hl)
        o_ref[0, rows, :] = (hh[:, :D_HYENA] * decay + jnp.where(n == 0, db_ref[...], 0.0)).astype(BF16)
        o_ref[1, rows, :] = jnp.where(n == 0, 0.0, hh[:, D_HYENA:] * decay).astype(BF16)


def _filt(feat, w1a, w1b, b1, q1, w2, b2, q2, w3a, w3b, dl, db, *, seq_len, tl):
    return pl.pallas_call(
        functools.partial(_filt_kernel, seq_len=seq_len, tl=tl),
        grid=(seq_len // tl,),
        in_specs=[
            pl.BlockSpec((tl, LANES), lambda i: (i, 0)),
            _resident((LANES, LANES)), _resident((LANES, LANES)), _resident((1, LANES)), _resident((1, LANES)),
            _resident((LANES, LANES)), _resident((1, LANES)), _resident((1, LANES)),
            _resident((LANES, 2 * D_HYENA)), _resident((LANES, 2 * D_HYENA)),
            _resident((1, D_HYENA)), _resident((1, D_HYENA)),
        ],
        out_specs=pl.BlockSpec((2, tl, D_HYENA), lambda i: (0, i, 0)),
        out_shape=jax.ShapeDtypeStruct((2, seq_len, D_HYENA), BF16),
        compiler_params=_cparams(1), name="filt",
    )(feat, w1a, w1b, b1, q1, w2, b2, q2, w3a, w3b, dl, db)


def _filter_features(seq_len):
    L = seq_len
    bands = (FILTER_EMB - 1) // 2
    pos = jnp.arange(L, dtype=F32)[:, None]
    t = pos / (L - 1)
    w = 2.0 * math.pi * pos / L
    f = jnp.linspace(1e-4, bands - 1, bands, dtype=F32)[None]
    pad = jnp.zeros((L, LANES - FILTER_EMB), F32)
    return jnp.concatenate([t, jnp.cos(f * w), -jnp.sin(f * w), pad], axis=1)


FF_CHUNK = 256


def _ffn_kernel(hm_ref, hp_ref, hn_ref, am_ref, ap_ref, an_ref, xm_ref, xp_ref, xn_ref,
                ym_ref, yp_ref, yn_ref, p_ref, gh_ref, wo_ref, gf_ref, wu_ref, wc_ref, bc_ref,
                wd_ref, wg_ref, wp_ref, o_ref, act_ref, *, tm, tiles_per_seq):
    i = pl.program_id(0)
    first = (i % tiles_per_seq) == 0
    last = (i % tiles_per_seq) == tiles_per_seq - 1

    def stack(m_ref, p_ref, n_ref):
        p = p_ref[...]
        n = n_ref[...]
        p = jnp.where(first, jnp.zeros_like(p), p)
        n = jnp.where(last, jnp.zeros_like(n), n)
        return jnp.concatenate([m_ref[...], n, p], axis=0)

    h = stack(hm_ref, hp_ref, hn_ref)
    a = stack(am_ref, ap_ref, an_ref)
    hy = stack(xm_ref, xp_ref, xn_ref).astype(F32) * stack(ym_ref, yp_ref, yn_ref).astype(F32)
    hyn = _rms(hy, gh_ref[...]).astype(BF16)
    h1 = (h + jnp.dot(a, wo_ref[:D_ATTN, :], preferred_element_type=F32)
          + jnp.dot(hyn, wo_ref[D_ATTN:, :], preferred_element_type=F32))
    n2 = _rms(h1, gf_ref[...]).astype(BF16)

    def conv(c0):
        u = jnp.dot(n2, wu_ref[:, c0:c0 + FF_CHUNK], preferred_element_type=F32)
        return _conv3(u, tm, wc_ref, bc_ref, c0, c0 + FF_CHUNK)

    for cidx in range(D_FF // FF_CHUNK):
        c0 = cidx * FF_CHUNK
        ua = conv(c0)
        ug = conv(D_FF + c0)
        act_ref[:, c0:c0 + FF_CHUNK] = (ug * jax.nn.sigmoid(ug) * ua).astype(BF16)

    h2 = h1[:tm] + jnp.dot(act_ref[...], wd_ref[...], preferred_element_type=F32)
    gate = jax.nn.sigmoid(jnp.dot(h2.astype(BF16), wg_ref[...], preferred_element_type=F32))
    pp = jnp.dot(p_ref[...].astype(BF16), wp_ref[...], preferred_element_type=F32)
    o_ref[...] = h2 + gate * pp


def _ffn(h, attn, x0, y, p, gh, wo, gf, wu, wc, bc, wd, wg, wp, *, seq_len, tm, p_tile0):
    T = h.shape[0]
    tiles_per_seq = seq_len // tm
    r16 = tm // BF16_ROWS
    nb16 = T // BF16_ROWS
    main = lambda i: (i, 0)
    prev = lambda i: (jnp.maximum(i * r16 - 1, 0), 0)
    nxt = lambda i: (jnp.minimum((i + 1) * r16, nb16 - 1), 0)

    def trio(width):
        return [pl.BlockSpec((tm, width), main), pl.BlockSpec((BF16_ROWS, width), prev),
                pl.BlockSpec((BF16_ROWS, width), nxt)]

    in_specs = (trio(D_MODEL) + trio(D_ATTN) + trio(D_HYENA) + trio(D_HYENA) + [
        pl.BlockSpec((tm, D_PLE), lambda i: (p_tile0 + i, 0)),
        _resident((1, D_HYENA)),
        _resident((D_MODEL, D_MODEL)),
        _resident((1, D_MODEL)),
        _resident((D_MODEL, 2 * D_FF)),
        _resident((3, 2 * D_FF)),
        _resident((1, 2 * D_FF)),
        _resident((D_FF, D_MODEL)),
        _resident((D_MODEL, D_MODEL)),
        _resident((D_PLE, D_MODEL)),
    ])
    return pl.pallas_call(
        functools.partial(_ffn_kernel, tm=tm, tiles_per_seq=tiles_per_seq),
        grid=(T // tm,), in_specs=in_specs,
        out_specs=pl.BlockSpec((tm, D_MODEL), main),
        out_shape=jax.ShapeDtypeStruct((T, D_MODEL), F32),
        scratch_shapes=[pltpu.VMEM((tm, D_FF), BF16)],
        compiler_params=_cparams(1), name="ffn",
    )(h, h, h, attn, attn, attn, x0, x0, x0, y, y, y, p, gh, wo, gf, wu, wc, bc, wd, wg, wp)


def _rope_slabs(seq_len):
    half = ROPE_DIM // 2
    inv = ROPE_THETA ** (-jnp.arange(0, ROPE_DIM, 2, dtype=F32) / ROPE_DIM)
    ang = jnp.arange(seq_len, dtype=F32)[:, None] * inv[None]
    cos, sin = jnp.cos(ang), jnp.sin(ang)
    ones = jnp.ones((seq_len, HEAD_DIM - ROPE_DIM), F32)
    zeros = jnp.zeros((seq_len, HEAD_DIM - ROPE_DIM), F32)
    zh = jnp.zeros((seq_len, half), F32)
    rc = jnp.concatenate([cos, cos, ones], axis=1)
    rs1 = jnp.concatenate([-sin, zh, zeros], axis=1)
    rs2 = jnp.concatenate([zh, sin, zeros], axis=1)
    rep = LANES // HEAD_DIM
    return tuple(jnp.tile(a, (1, rep)) for a in (rc, rs1, rs2))


def _group_consts(seq_len):
    n1h, k1p = _fft_sizes(seq_len)
    w1, winv, g, gs, gi, gis = _dft_consts(seq_len)
    fc, fs, tc, ts = _twiddle_tables(seq_len)
    return dict(
        seq_len=seq_len, n1h=n1h, k1p=k1p,
        w1_half=jnp.asarray(w1[:, :n1h], BF16),
        winv=jnp.asarray(winv, BF16),
        g=jnp.asarray(g, F32), gs=jnp.asarray(gs, F32), gi=jnp.asarray(gi, F32), gis=jnp.asarray(gis, F32),
        fc=fc, fs=fs, tc=tc, ts=ts, rope=_rope_slabs(seq_len), feat=_filter_features(seq_len),
    )


def _pair(v):
    return jnp.concatenate([v, v])[None]


def _layer_weights(i, rms_mix, w_in, q_norm, k_norm, sink, w_short, b_short, filt_w1, filt_b1, filt_freq1,
                   filt_w2, filt_b2, filt_freq2, filt_w3, hyena_bias, norm_attn_out, norm_hyena_out,
                   w_out, rms_ffn, w_up, w_ffconv, b_ffconv, w_down, w_ple_gate, w_ple_proj):
    rep_q = D_ATTN // HEAD_DIM
    rep_k = D_KV // HEAD_DIM
    hid = FILTER_HID
    w1p = jnp.zeros((LANES, hid), F32).at[:FILTER_EMB].set(filt_w1[i])
    zpad = jnp.zeros((LANES, hid), F32)
    zsq = jnp.zeros((hid, hid), F32)
    z3 = jnp.zeros((hid, 2 * D_HYENA), F32)
    deltas = jnp.abs(jnp.linspace(MIN_DECAY, MAX_DECAY, D_HYENA, dtype=F32))
    return dict(
        g_mix=rms_mix[i][None], w_in=w_in[i].astype(BF16),
        qg=(jnp.tile(q_norm[i], rep_q) * (HEAD_DIM ** -0.5))[None], kg=jnp.tile(k_norm[i], rep_k)[None],
        sink=sink[i], ws=w_short[i], bs=b_short[i][None],
        fw1a=jnp.concatenate([w1p, zpad], axis=1), fw1b=jnp.concatenate([zpad, w1p], axis=1),
        fb1=_pair(filt_b1[i]), fq1=_pair(filt_freq1[i]),
        fw2=jnp.block([[filt_w2[i], zsq], [zsq, filt_w2[i]]]),
        fb2=_pair(filt_b2[i]), fq2=_pair(filt_freq2[i]),
        fw3a=jnp.concatenate([filt_w3[i], z3], axis=0), fw3b=jnp.concatenate([z3, filt_w3[i]], axis=0),
        deltas=deltas[None], dbias=hyena_bias[i][None],
        g_attn=norm_attn_out[i][None], g_hy=norm_hyena_out[i][None],
        w_out=w_out[i].astype(BF16), g_ffn=rms_ffn[i][None], w_up=w_up[i].astype(BF16),
        wc=w_ffconv[i], bc=b_ffconv[i][None], w_down=w_down[i].astype(BF16),
        w_gate=w_ple_gate[i].astype(BF16), w_proj=w_ple_proj[i].astype(BF16),
    )


def _layer(h, p_all, layer, lw, gc, bd, batch, *, tm, tq, tl):
    L = gc["seq_len"]
    T = batch * L
    n1h, k1p = gc["n1h"], gc["k1p"]
    rc, rs1, rs2 = gc["rope"]
    q, kd, va, x0, vv = _inproj(h, lw["g_mix"], lw["w_in"], lw["qg"], lw["kg"], bd, rc, rs1, rs2,
                                lw["ws"], lw["bs"], seq_len=L, tm=tm)
    attn = _attn(lw["sink"], q, kd, va, lw["g_attn"], seq_len=L, tq=tq)
    filt = _filt(gc["feat"], lw["fw1a"], lw["fw1b"], lw["fb1"], lw["fq1"], lw["fw2"], lw["fb2"], lw["fq2"],
                 lw["fw3a"], lw["fw3b"], lw["deltas"], lw["dbias"], seq_len=L, tl=tl)
    fa = _fft1(gc["w1_half"], filt.reshape(2, n1h, FFT_N2, D_HYENA), k1p=k1p)
    hspec = _fspec(gc["g"], gc["gs"], gc["fc"], gc["fs"], fa)
    a = _fft1(gc["w1_half"], vv.reshape(batch, n1h, FFT_N2, D_HYENA), k1p=k1p)
    a = _fft2(gc["g"], gc["gs"], gc["gi"], gc["gis"], gc["fc"], gc["fs"], gc["tc"], gc["ts"], a, hspec)
    y = _ifft1(gc["winv"], a).reshape(T, D_HYENA)
    return _ffn(h, attn, x0, y, p_all, lw["g_hy"], lw["w_out"], lw["g_ffn"], lw["w_up"], lw["wc"], lw["bc"],
                lw["w_down"], lw["w_gate"], lw["w_proj"], seq_len=L, tm=tm, p_tile0=layer * (T // tm))


def _trunk(x, p, weights, *, tm=512, tq=512, tl=512):
    batch, L, _ = x.shape
    gc = _group_consts(L)
    bd = jnp.asarray(np.kron(np.eye(D_ATTN // HEAD_DIM), np.full((HEAD_DIM, HEAD_DIM), 1.0 / HEAD_DIM)), BF16)
    h = x.reshape(batch * L, D_MODEL)
    p_all = p.reshape(DEPTH * batch * L, D_PLE)
    for i in range(DEPTH):
        lw = _layer_weights(i, *weights)
        h = _layer(h, p_all, i, lw, gc, bd, batch, tm=tm, tq=tq, tl=tl)
    return h.reshape(batch, L, D_MODEL)


def kernel(x_prompt, x_sample, p_prompt, p_sample, rms_mix, w_in, q_norm, k_norm, sink, w_short, b_short, filt_w1, filt_b1, filt_freq1, filt_w2, filt_b2, filt_freq2, filt_w3, hyena_bias, norm_attn_out, norm_hyena_out, w_out, rms_ffn, w_up, w_ffconv, b_ffconv, w_down, w_ple_gate, w_ple_proj):
    weights = (rms_mix, w_in, q_norm, k_norm, sink, w_short, b_short, filt_w1, filt_b1, filt_freq1,
               filt_w2, filt_b2, filt_freq2, filt_w3, hyena_bias, norm_attn_out, norm_hyena_out,
               w_out, rms_ffn, w_up, w_ffconv, b_ffconv, w_down, w_ple_gate, w_ple_proj)
    y_prompt = _trunk(x_prompt, p_prompt, weights)
    y_sample = _trunk(x_sample, p_sample, weights)
    return (y_prompt, y_sample)
```

```python
import functools
import math

import numpy as np
import jax
import jax.numpy as jnp
from jax import lax
from jax.experimental import pallas as pl
from jax.experimental.pallas import tpu as pltpu

F32 = jnp.float32
BF16 = jnp.bfloat16

D_MODEL = 1024
DEPTH = 4
N_Q_HEADS = 8
N_KV_HEADS = 2
HEAD_DIM = 64
D_ATTN = N_Q_HEADS * HEAD_DIM
D_KV = N_KV_HEADS * HEAD_DIM
D_QKV = D_ATTN + 2 * D_KV
WINDOW = 128
BLOCK = 128
ROPE_THETA = 500000.0
ROPE_DIM = HEAD_DIM // 4
D_HYENA = 512
FILTER_EMB = 33
FILTER_HID = 64
FAST_DECAY_PCT = 0.3
SLOW_DECAY_PCT = 1.5
DECAY_TARGET = 1e-2
MAX_DECAY = math.log(DECAY_TARGET) / FAST_DECAY_PCT
MIN_DECAY = math.log(DECAY_TARGET) / SLOW_DECAY_PCT
D_IN = D_QKV + 3 * D_HYENA
D_FF = 2816
D_PLE = 256
EPS = 1e-6
MASKED = -1e30

LANES = 128
SUBLANES = 8
BF16_ROWS = 16
MXU_W = 256
FFT_N2 = 64
FFT_MG = 32
FFT_KB = 24
TILE_INPROJ = 1024
TILE_ATTN = 1024
TILE_FFN = 512
TILE_FILT = 512
VMEM_LIMIT = 56 * 1024 * 1024


def _cparams(n_axes):
    return pltpu.CompilerParams(dimension_semantics=("arbitrary",) * n_axes,
                                vmem_limit_bytes=VMEM_LIMIT)


def _resident(shape, layer=None):
    if layer is None:
        return pl.BlockSpec(shape, lambda *_: (0,) * len(shape), pipeline_mode=pl.Buffered(1))
    return pl.BlockSpec((None,) + tuple(shape), lambda *_: (layer,) + (0,) * len(shape),
                        pipeline_mode=pl.Buffered(1))


def _rms(x, g):
    ms = jnp.mean(x * x, axis=-1, keepdims=True)
    return x * lax.rsqrt(ms + EPS) * g


def _conv3(u, tm, w_ref, b_ref, c0, c1):
    rows = u.shape[0]
    up = pltpu.roll(u, 1, axis=0)[:tm]
    un = pltpu.roll(u, rows - 1, axis=0)[:tm]
    return up * w_ref[0:1, c0:c1] + u[:tm] * w_ref[1:2, c0:c1] + un * w_ref[2:3, c0:c1] + b_ref[:, c0:c1]


def _inproj_kernel(xm_ref, xp_ref, xn_ref, g_ref, w_ref, qg_ref, kg_ref, bd_ref,
                   rc_ref, rs1_ref, rs2_ref, ws_ref, bs_ref,
                   q_ref, kd_ref, va_ref, x0_ref, vv_ref, *, tm, tiles_per_seq):
    i = pl.program_id(0)
    first = (i % tiles_per_seq) == 0
    last = (i % tiles_per_seq) == tiles_per_seq - 1
    xp = jnp.where(first, 0.0, xp_ref[...])
    xn = jnp.where(last, 0.0, xn_ref[...])
    x = jnp.concatenate([xm_ref[...], xn, xp], axis=0)
    n = _rms(x, g_ref[...]).astype(BF16)
    z = jnp.dot(n, w_ref[...], preferred_element_type=F32)

    bd = bd_ref[...]
    rc, rs1, rs2 = rc_ref[...], rs1_ref[...], rs2_ref[...]

    def norm_rope(zs, gain, bds):
        sq = (zs * zs).astype(BF16)
        step = bds.shape[0]
        ms = jnp.concatenate([jnp.dot(sq[:, c:c + step], bds, preferred_element_type=F32)
                              for c in range(0, zs.shape[1], step)], axis=1)
        y = zs * lax.rsqrt(ms + EPS) * gain
        outs = []
        for s in range(y.shape[1] // LANES):
            ys = y[:, s * LANES:(s + 1) * LANES]
            outs.append(ys * rc + pltpu.roll(ys, LANES - ROPE_DIM // 2, axis=1) * rs1
                        + pltpu.roll(ys, ROPE_DIM // 2, axis=1) * rs2)
        return outs

    qs = norm_rope(z[:tm, :D_ATTN], qg_ref[...], bd)
    for s, qv in enumerate(qs):
        q_ref[:, s * LANES:(s + 1) * LANES] = qv.astype(BF16)
    k = norm_rope(z[:tm, D_ATTN:D_ATTN + D_KV], kg_ref[...], bd[:D_KV, :D_KV])[0]
    v = z[:tm, D_ATTN + D_KV:D_QKV]
    low = lax.broadcasted_iota(jnp.int32, (tm, LANES), 1) < HEAD_DIM
    kr = pltpu.roll(k, HEAD_DIM, axis=1)
    vr = pltpu.roll(v, HEAD_DIM, axis=1)
    kd_ref[:, :LANES] = jnp.where(low, k, kr).astype(BF16)
    kd_ref[:, LANES:] = jnp.where(low, kr, k).astype(BF16)
    va_ref[:, :LANES] = jnp.where(low, v, 1.0).astype(BF16)
    va_ref[:, LANES:] = jnp.where(low, vr, 1.0).astype(BF16)

    u = _conv3(z[:, D_QKV:], tm, ws_ref, bs_ref, 0, 3 * D_HYENA)
    x0_ref[...] = u[:, :D_HYENA].astype(BF16)
    vv_ref[...] = (u[:, D_HYENA:2 * D_HYENA] * u[:, 2 * D_HYENA:]).astype(BF16)


def _inproj(x, g, w, qg, kg, bd, rc, rs1, rs2, ws, bs, *, seq_len, tm, layer):
    T = x.shape[0]
    tiles_per_seq = seq_len // tm
    r8 = tm // SUBLANES
    nb8 = T // SUBLANES
    rope_spec = pl.BlockSpec((tm, LANES), lambda i: (i % tiles_per_seq, 0))
    in_specs = [
        pl.BlockSpec((tm, D_MODEL), lambda i: (i, 0)),
        pl.BlockSpec((SUBLANES, D_MODEL), lambda i: (jnp.maximum(i * r8 - 1, 0), 0)),
        pl.BlockSpec((SUBLANES, D_MODEL), lambda i: (jnp.minimum((i + 1) * r8, nb8 - 1), 0)),
        _resident((1, D_MODEL)),
        _resident((D_MODEL, D_IN), layer),
        _resident((1, D_ATTN)),
        _resident((1, D_KV)),
        _resident((MXU_W, MXU_W)),
        rope_spec, rope_spec, rope_spec,
        _resident((3, 3 * D_HYENA)),
        _resident((1, 3 * D_HYENA)),
    ]
    widths = (D_ATTN, 2 * D_KV, 2 * D_KV, D_HYENA, D_HYENA)
    out_specs = [pl.BlockSpec((tm, wd), lambda i: (i, 0)) for wd in widths]
    out_shape = [jax.ShapeDtypeStruct((T, wd), BF16) for wd in widths]
    return pl.pallas_call(
        functools.partial(_inproj_kernel, tm=tm, tiles_per_seq=tiles_per_seq),
        grid=(T // tm,), in_specs=in_specs, out_specs=out_specs, out_shape=out_shape,
        compiler_params=_cparams(1), name="inproj",
    )(x, x, x, g, w, qg, kg, bd, rc, rs1, rs2, ws, bs)


def _attn_kernel(sink_ref, q_ref, km_ref, kp_ref, kn_ref, vm_ref, vp_ref, vn_ref, g_ref,
                 o_ref, kbuf, vbuf, *, tq, tiles_per_seq):
    i = pl.program_id(0)
    first = (i % tiles_per_seq) == 0
    last = (i % tiles_per_seq) == tiles_per_seq - 1
    nblk = tq // BLOCK
    G = N_Q_HEADS // N_KV_HEADS
    kbuf[0:BLOCK] = kp_ref[...]
    kbuf[BLOCK:BLOCK + tq] = km_ref[...]
    kbuf[BLOCK + tq:] = kn_ref[...]
    vbuf[0:BLOCK] = vp_ref[...]
    vbuf[BLOCK:BLOCK + tq] = vm_ref[...]
    vbuf[BLOCK + tq:] = vn_ref[...]
    r = lax.broadcasted_iota(jnp.int32, (BLOCK, 3 * BLOCK), 0)
    c = lax.broadcasted_iota(jnp.int32, (BLOCK, 3 * BLOCK), 1)
    d = c - BLOCK - r
    band = jnp.where((d >= -WINDOW) & (d <= WINDOW), 0.0, MASKED)
    low = lax.broadcasted_iota(jnp.int32, (BLOCK, LANES), 1) < HEAD_DIM
    mlo = jnp.where(low, 1.0, 0.0).astype(BF16)
    mhi = jnp.where(low, 0.0, 1.0).astype(BF16)
    rid = lax.broadcasted_iota(jnp.int32, (G * BLOCK, 1), 0)
    for j in range(nblk):
        bias = band
        if j == 0:
            bias = jnp.where(first & (c < BLOCK), MASKED, bias)
        if j == nblk - 1:
            bias = jnp.where(last & (c >= 2 * BLOCK), MASKED, bias)
        slabs = []
        for hk in range(N_KV_HEADS):
            kb = kbuf[j * BLOCK:(j + 3) * BLOCK, hk * LANES:(hk + 1) * LANES]
            vb = vbuf[j * BLOCK:(j + 3) * BLOCK, hk * LANES:(hk + 1) * LANES]
            c0 = hk * G * HEAD_DIM
            qa = q_ref[j * BLOCK:(j + 1) * BLOCK, c0:c0 + LANES]
            qb = q_ref[j * BLOCK:(j + 1) * BLOCK, c0 + LANES:c0 + 2 * LANES]
            ql = jnp.concatenate([qa * mlo, qa * mhi, qb * mlo, qb * mhi], axis=0)
            s = lax.dot_general(ql, kb, (((1,), (1,)), ((), ())), preferred_element_type=F32)
            s = (s.reshape(G, BLOCK, 3 * BLOCK) + bias[None]).reshape(G * BLOCK, 3 * BLOCK)
            sk = jnp.where(rid < BLOCK, sink_ref[hk * G],
                           jnp.where(rid < 2 * BLOCK, sink_ref[hk * G + 1],
                                     jnp.where(rid < 3 * BLOCK, sink_ref[hk * G + 2], sink_ref[hk * G + 3])))
            m = jnp.maximum(jnp.max(s, axis=-1, keepdims=True), sk)
            e = jnp.exp((s - m).astype(BF16))
            out = jnp.dot(e, vb, preferred_element_type=F32)
            den = pltpu.roll(out, HEAD_DIM, axis=1) + jnp.exp(sk - m)
            res = out / den
            for pair in range(G // 2):
                ev = res[(2 * pair) * BLOCK:(2 * pair + 1) * BLOCK]
                od = res[(2 * pair + 1) * BLOCK:(2 * pair + 2) * BLOCK]
                slabs.append(jnp.where(low, ev, pltpu.roll(od, HEAD_DIM, axis=1)))
        o = jnp.concatenate(slabs, axis=1)
        o_ref[j * BLOCK:(j + 1) * BLOCK, :] = _rms(o, g_ref[...]).astype(BF16)


def _attn(sink, q, kd, va, g, *, seq_len, tq):
    T = q.shape[0]
    tiles_per_seq = seq_len // tq
    rb = tq // BLOCK
    nbb = T // BLOCK
    main = lambda i, s: (i, 0)
    prev = lambda i, s: (jnp.maximum(i * rb - 1, 0), 0)
    nxt = lambda i, s: (jnp.minimum((i + 1) * rb, nbb - 1), 0)
    kvw = 2 * D_KV
    grid_spec = pltpu.PrefetchScalarGridSpec(
        num_scalar_prefetch=1, grid=(T // tq,),
        in_specs=[
            pl.BlockSpec((tq, D_ATTN), main),
            pl.BlockSpec((tq, kvw), main), pl.BlockSpec((BLOCK, kvw), prev), pl.BlockSpec((BLOCK, kvw), nxt),
            pl.BlockSpec((tq, kvw), main), pl.BlockSpec((BLOCK, kvw), prev), pl.BlockSpec((BLOCK, kvw), nxt),
            pl.BlockSpec((1, D_ATTN), lambda i, s: (0, 0)),
        ],
        out_specs=pl.BlockSpec((tq, D_ATTN), main),
        scratch_shapes=[pltpu.VMEM((tq + 2 * BLOCK, kvw), BF16), pltpu.VMEM((tq + 2 * BLOCK, kvw), BF16)],
    )
    return pl.pallas_call(
        functools.partial(_attn_kernel, tq=tq, tiles_per_seq=tiles_per_seq),
        grid_spec=grid_spec, out_shape=jax.ShapeDtypeStruct((T, D_ATTN), BF16),
        compiler_params=_cparams(1), name="attn",
    )(sink, q, kd, kd, kd, va, va, va, g)


def _fft_sizes(seq_len):
    n1h = seq_len // FFT_N2
    k1p = -(-(n1h + 1) // FFT_KB) * FFT_KB
    return n1h, k1p


@functools.lru_cache(maxsize=None)
def _dft_consts(seq_len):
    n1h, k1p = _fft_sizes(seq_len)
    n1 = 2 * n1h
    k = np.arange(k1p)[:, None]
    valid = (k <= n1h)
    th = 2.0 * np.pi * ((k * np.arange(n1)[None, :]) % n1) / n1
    w1 = np.concatenate([np.cos(th) * valid, -np.sin(th) * valid], axis=0)
    scale = np.where((k == 0) | (k == n1h), 1.0, 2.0) * valid / n1
    thh = th[:, :n1h]
    winv = np.concatenate([np.cos(thh) * scale, -np.sin(thh) * scale], axis=0).T
    a = 2.0 * np.pi * ((np.arange(FFT_N2)[:, None] * np.arange(FFT_N2)[None, :]) % FFT_N2) / FFT_N2
    cm, sm = np.cos(a), np.sin(a)
    g = np.block([[cm, sm], [-sm, cm]])
    gi = np.block([[cm, -sm], [sm, cm]]) / FFT_N2
    gs = np.concatenate([g[:, FFT_N2:], g[:, :FFT_N2]], axis=1)
    gis = np.concatenate([gi[FFT_N2:], gi[:FFT_N2]], axis=0)
    return w1, winv, g, gs, gi, gis


def _twiddle_tables(seq_len):
    n1h, k1p = _fft_sizes(seq_len)
    n = 2 * seq_len
    k = jnp.arange(k1p, dtype=jnp.int32)[:, None]
    m = jnp.arange(FFT_N2, dtype=jnp.int32)[None, :]
    ph = ((k * m) % n).astype(F32) * (2.0 * math.pi / n)
    cs, sn = jnp.cos(ph), jnp.sin(ph)
    fc = jnp.concatenate([cs, cs], axis=1)
    fs = jnp.concatenate([-sn, sn], axis=1)
    tc = jnp.broadcast_to(cs[:, :, None], (k1p, FFT_N2, LANES))
    ts = jnp.broadcast_to(sn[:, :, None], (k1p, FFT_N2, LANES))
    return fc, fs, tc, ts


def _fft1_kernel(w_ref, x_ref, o_ref, *, k1p):
    a = jnp.einsum('kn,nmc->kmc', w_ref[...], x_ref[...], preferred_element_type=F32)
    o_ref[0] = a[:k1p].astype(BF16)
    o_ref[1] = a[k1p:].astype(BF16)


def _fft1(w1, x, *, k1p):
    B, n1in, _, C = x.shape
    grid = (FFT_N2 // FFT_MG, B, C // LANES)
    return pl.pallas_call(
        functools.partial(_fft1_kernel, k1p=k1p),
        grid=grid,
        in_specs=[
            _resident((2 * k1p, n1in)),
            pl.BlockSpec((None, n1in, FFT_MG, LANES), lambda m, b, j: (b, 0, m, j)),
        ],
        out_specs=pl.BlockSpec((None, 2, k1p, FFT_MG, LANES), lambda m, b, j: (b, 0, 0, m, j)),
        out_shape=jax.ShapeDtypeStruct((B, 2, k1p, FFT_N2, C), BF16),
        compiler_params=_cparams(3), name="fft1",
    )(w1, x)


def _pair_diag(a, b):
    z = jnp.zeros_like(a)
    return jnp.concatenate([jnp.concatenate([a, z], axis=1), jnp.concatenate([z, b], axis=1)], axis=0).astype(BF16)


def _fwd_pair(g_ref, gs_ref, fc_ref, fs_ref, a_ref, j):
    def mat(jj):
        return g_ref[...] * fc_ref[jj:jj + 1, :] + gs_ref[...] * fs_ref[jj:jj + 1, :]
    x = jnp.concatenate([a_ref[0, j], a_ref[1, j], a_ref[0, j + 1], a_ref[1, j + 1]], axis=0)
    return jnp.dot(_pair_diag(mat(j), mat(j + 1)), x, preferred_element_type=F32)


def _fspec_kernel(g_ref, gs_ref, fc_ref, fs_ref, a_ref, o_ref):
    n = FFT_N2
    for j in range(0, FFT_KB, 2):
        bf = _fwd_pair(g_ref, gs_ref, fc_ref, fs_ref, a_ref.at[0], j)
        bb = _fwd_pair(g_ref, gs_ref, fc_ref, fs_ref, a_ref.at[1], j)
        for t in range(2):
            re = slice(2 * t * n, (2 * t + 1) * n)
            im = slice((2 * t + 1) * n, (2 * t + 2) * n)
            o_ref[0, j + t] = (bf[re] + bb[re]).astype(BF16)
            o_ref[1, j + t] = (bf[im] - bb[im]).astype(BF16)


def _fspec(g, gs, fc, fs, a):
    _, _, k1p, _, C = a.shape
    row = pl.BlockSpec((FFT_KB, 2 * FFT_N2), lambda kb: (kb, 0))
    return pl.pallas_call(
        _fspec_kernel, grid=(k1p // FFT_KB,),
        in_specs=[_resident((2 * FFT_N2, 2 * FFT_N2)), _resident((2 * FFT_N2, 2 * FFT_N2)), row, row,
                  pl.BlockSpec((2, 2, FFT_KB, FFT_N2, C), lambda kb: (0, 0, kb, 0, 0))],
        out_specs=pl.BlockSpec((2, FFT_KB, FFT_N2, C), lambda kb: (0, kb, 0, 0)),
        out_shape=jax.ShapeDtypeStruct(a.shape[1:], BF16),
        compiler_params=_cparams(1), name="fspec",
    )(g, gs, fc, fs, a)


def _fft2_kernel(g_ref, gs_ref, gi_ref, gis_ref, fc_ref, fs_ref, tc_ref, ts_ref, a_ref, h_ref, o_ref):
    n = FFT_N2

    def inv(jj):
        tc, ts = tc_ref[jj], ts_ref[jj]
        return (gi_ref[...] * jnp.concatenate([tc, tc], axis=0)
                + gis_ref[...] * jnp.concatenate([-ts, ts], axis=0))

    for j in range(0, FFT_KB, 2):
        b = _fwd_pair(g_ref, gs_ref, fc_ref, fs_ref, a_ref, j)
        ys = []
        for t in range(2):
            br, bi = b[2 * t * n:(2 * t + 1) * n], b[(2 * t + 1) * n:(2 * t + 2) * n]
            hr, hi = h_ref[0, j + t].astype(F32), h_ref[1, j + t].astype(F32)
            ys += [br * hr - bi * hi, br * hi + bi * hr]
        y = jnp.concatenate(ys, axis=0).astype(BF16)
        p = jnp.dot(_pair_diag(inv(j), inv(j + 1)), y, preferred_element_type=F32)
        for t in range(2):
            o_ref[0, j + t] = p[2 * t * n:(2 * t + 1) * n].astype(BF16)
            o_ref[1, j + t] = p[(2 * t + 1) * n:(2 * t + 2) * n].astype(BF16)


def _fft2(g, gs, gi, gis, fc, fs, tc, ts, a, hspec):
    B, _, k1p, _, C = a.shape
    mat = _resident((2 * FFT_N2, 2 * FFT_N2))
    row = pl.BlockSpec((FFT_KB, 2 * FFT_N2), lambda kb, b: (kb, 0))
    col = pl.BlockSpec((FFT_KB, FFT_N2, LANES), lambda kb, b: (kb, 0, 0))
    return pl.pallas_call(
        _fft2_kernel, grid=(k1p // FFT_KB, B),
        in_specs=[
            mat, mat, mat, mat, row, row, col, col,
            pl.BlockSpec((None, 2, FFT_KB, FFT_N2, C), lambda kb, b: (b, 0, kb, 0, 0)),
            pl.BlockSpec((2, FFT_KB, FFT_N2, C), lambda kb, b: (0, kb, 0, 0)),
        ],
        out_specs=pl.BlockSpec((None, 2, FFT_KB, FFT_N2, C), lambda kb, b: (b, 0, kb, 0, 0)),
        out_shape=jax.ShapeDtypeStruct(a.shape, BF16),
        compiler_params=_cparams(2), name="fft2",
    )(g, gs, gi, gis, fc, fs, tc, ts, a, hspec)


def _ifft1_kernel(w_ref, a_ref, o_ref):
    a = jnp.concatenate([a_ref[0], a_ref[1]], axis=0)
    o_ref[...] = jnp.einsum('nk,kmc->nmc', w_ref[...], a, preferred_element_type=F32).astype(BF16)


def _ifft1(winv, a):
    B, _, k1p, _, C = a.shape
    n1h = winv.shape[0]
    return pl.pallas_call(
        _ifft1_kernel, grid=(FFT_N2 // FFT_MG, B, C // LANES),
        in_specs=[
            _resident((n1h, 2 * k1p)),
            pl.BlockSpec((None, 2, k1p, FFT_MG, LANES), lambda m, b, j: (b, 0, 0, m, j)),
        ],
        out_specs=pl.BlockSpec((None, n1h, FFT_MG, LANES), lambda m, b, j: (b, 0, m, j)),
        out_shape=jax.ShapeDtypeStruct((B, n1h, FFT_N2, C), BF16),
        compiler_params=_cparams(3), name="ifft1",
    )(winv, a)


def _split_dot(a, b):
    ah = a.astype(BF16)
    al = (a - ah.astype(F32)).astype(BF16)
    bh = b.astype(BF16)
    bl = (b - bh.astype(F32)).astype(BF16)
    d = functools.partial(jnp.dot, preferred_element_type=F32)
    return d(ah, bh) + d(ah, bl) + d(al, bh)


def _filt_kernel(ft_ref, w1a_ref, w1b_ref, b1_ref, q1_ref, w2_ref, b2_ref, q2_ref, w3a_ref, w3b_ref,
                 dl_ref, db_ref, o_ref, *, seq_len, tl):
    i = pl.program_id(0)
    L = seq_len
    hl = tl // 2
    pre = _split_dot(ft_ref[:hl, :], w1a_ref[...]) + _split_dot(ft_ref[hl:, :], w1b_ref[...])
    h = jnp.sin(q1_ref[...] * (pre + b1_ref[...]))
    h = jnp.sin(q2_ref[...] * (_split_dot(h, w2_ref[...]) + b2_ref[...]))
    for half, w3_ref in enumerate((w3a_ref, w3b_ref)):
        hh = _split_dot(h, w3_ref[...])
        n = i * tl + half * hl + lax.broadcasted_iota(jnp.int32, (hl, D_HYENA), 0)
        decay = jnp.exp(-(n.astype(F32) / (L - 1)) * dl_ref[...])
        rows = slice(half * hl, (half + 1) * hl)
        o_ref[0, rows, :] = (hh[:, :D_HYENA] * decay + jnp.where(n == 0, db_ref[...], 0.0)).astype(BF16)
        o_ref[1, rows, :] = jnp.where(n == 0, 0.0, hh[:, D_HYENA:] * decay).astype(BF16)


def _filt(feat, w1a, w1b, b1, q1, w2, b2, q2, w3a, w3b, dl, db, *, seq_len, tl):
    return pl.pallas_call(
        functools.partial(_filt_kernel, seq_len=seq_len, tl=tl),
        grid=(seq_len // tl,),
        in_specs=[
            pl.BlockSpec((tl, FEAT_W), lambda i: (i, 0)),
            _resident((FEAT_W, LANES)), _resident((FEAT_W, LANES)), _resident((1, LANES)), _resident((1, LANES)),
            _resident((LANES, LANES)), _resident((1, LANES)), _resident((1, LANES)),
            _resident((LANES, 2 * D_HYENA)), _resident((LANES, 2 * D_HYENA)),
            _resident((1, D_HYENA)), _resident((1, D_HYENA)),
        ],
        out_specs=pl.BlockSpec((2, tl, D_HYENA), lambda i: (0, i, 0)),
        out_shape=jax.ShapeDtypeStruct((2, seq_len, D_HYENA), BF16),
        compiler_params=_cparams(1), name="filt",
    )(feat, w1a, w1b, b1, q1, w2, b2, q2, w3a, w3b, dl, db)


FEAT_W = 40


@functools.lru_cache(maxsize=None)
def _filter_features(seq_len):
    L = seq_len
    bands = (FILTER_EMB - 1) // 2
    pos = np.arange(L, dtype=np.float64)[:, None]
    t = pos / (L - 1)
    w = 2.0 * np.pi * pos / L
    f = np.linspace(1e-4, bands - 1, bands)[None]
    pad = np.zeros((L, FEAT_W - FILTER_EMB))
    return np.concatenate([t, np.cos(f * w), -np.sin(f * w), pad], axis=1).astype(np.float32)


FF_CHUNK = 256


def _ffn_kernel(hm_ref, hp_ref, hn_ref, am_ref, ap_ref, an_ref, xm_ref, xp_ref, xn_ref,
                ym_ref, yp_ref, yn_ref, p_ref, gh_ref, wo_ref, gf_ref, wu_ref, wc_ref, bc_ref,
                wd_ref, wg_ref, wp_ref, o_ref, act_ref, *, tm, tiles_per_seq):
    i = pl.program_id(0)
    first = (i % tiles_per_seq) == 0
    last = (i % tiles_per_seq) == tiles_per_seq - 1

    def stack(m_ref, p_ref, n_ref):
        p = p_ref[BF16_ROWS - SUBLANES:, :]
        n = n_ref[:SUBLANES, :]
        p = jnp.where(first, jnp.zeros_like(p), p)
        n = jnp.where(last, jnp.zeros_like(n), n)
        return jnp.concatenate([m_ref[...], n, p], axis=0)

    h = stack(hm_ref, hp_ref, hn_ref)
    a = stack(am_ref, ap_ref, an_ref)
    hy = stack(xm_ref, xp_ref, xn_ref).astype(F32) * stack(ym_ref, yp_ref, yn_ref).astype(F32)
    hyn = _rms(hy, gh_ref[...]).astype(BF16)
    h1 = (h + jnp.dot(a, wo_ref[:D_ATTN, :], preferred_element_type=F32)
          + jnp.dot(hyn, wo_ref[D_ATTN:, :], preferred_element_type=F32))
    n2 = _rms(h1, gf_ref[...]).astype(BF16)

    def conv(c0):
        u = jnp.dot(n2, wu_ref[:, c0:c0 + FF_CHUNK], preferred_element_type=F32)
        return _conv3(u, tm, wc_ref, bc_ref, c0, c0 + FF_CHUNK)

    for cidx in range(D_FF // FF_CHUNK):
        c0 = cidx * FF_CHUNK
        ua = conv(c0)
        ug = conv(D_FF + c0)
        act_ref[:, c0:c0 + FF_CHUNK] = (ug * jax.nn.sigmoid(ug) * ua).astype(BF16)

    h2 = h1[:tm] + jnp.dot(act_ref[...], wd_ref[...], preferred_element_type=F32)
    gate = jax.nn.sigmoid(jnp.dot(h2.astype(BF16), wg_ref[...], preferred_element_type=F32))
    pp = jnp.dot(p_ref[...].astype(BF16), wp_ref[...], preferred_element_type=F32)
    o_ref[...] = h2 + gate * pp


def _ffn(h, attn, x0, y, p, gh, wo, gf, wu, wc, bc, wd, wg, wp, *, seq_len, tm, layer):
    p_tile0 = layer * (h.shape[0] // tm)
    T = h.shape[0]
    tiles_per_seq = seq_len // tm
    r16 = tm // BF16_ROWS
    nb16 = T // BF16_ROWS
    main = lambda i: (i, 0)
    prev = lambda i: (jnp.maximum(i * r16 - 1, 0), 0)
    nxt = lambda i: (jnp.minimum((i + 1) * r16, nb16 - 1), 0)

    def trio(width):
        return [pl.BlockSpec((tm, width), main), pl.BlockSpec((BF16_ROWS, width), prev),
                pl.BlockSpec((BF16_ROWS, width), nxt)]

    in_specs = (trio(D_MODEL) + trio(D_ATTN) + trio(D_HYENA) + trio(D_HYENA) + [
        pl.BlockSpec((tm, D_PLE), lambda i: (p_tile0 + i, 0)),
        _resident((1, D_HYENA)),
        _resident((D_MODEL, D_MODEL), layer),
        _resident((1, D_MODEL)),
        _resident((D_MODEL, 2 * D_FF), layer),
        _resident((3, 2 * D_FF)),
        _resident((1, 2 * D_FF)),
        _resident((D_FF, D_MODEL), layer),
        _resident((D_MODEL, D_MODEL), layer),
        _resident((D_PLE, D_MODEL), layer),
    ])
    return pl.pallas_call(
        functools.partial(_ffn_kernel, tm=tm, tiles_per_seq=tiles_per_seq),
        grid=(T // tm,), in_specs=in_specs,
        out_specs=pl.BlockSpec((tm, D_MODEL), main),
        out_shape=jax.ShapeDtypeStruct((T, D_MODEL), F32),
        scratch_shapes=[pltpu.VMEM((tm, D_FF), BF16)],
        compiler_params=_cparams(1), name="ffn",
    )(h, h, h, attn, attn, attn, x0, x0, x0, y, y, y, p, gh, wo, gf, wu, wc, bc, wd, wg, wp)


def _rope_slabs(seq_len):
    half = ROPE_DIM // 2
    inv = ROPE_THETA ** (-jnp.arange(0, ROPE_DIM, 2, dtype=F32) / ROPE_DIM)
    ang = jnp.arange(seq_len, dtype=F32)[:, None] * inv[None]
    cos, sin = jnp.cos(ang), jnp.sin(ang)
    ones = jnp.ones((seq_len, HEAD_DIM - ROPE_DIM), F32)
    zeros = jnp.zeros((seq_len, HEAD_DIM - ROPE_DIM), F32)
    zh = jnp.zeros((seq_len, half), F32)
    rc = jnp.concatenate([cos, cos, ones], axis=1)
    rs1 = jnp.concatenate([-sin, zh, zeros], axis=1)
    rs2 = jnp.concatenate([zh, sin, zeros], axis=1)
    rep = LANES // HEAD_DIM
    return tuple(jnp.tile(a, (1, rep)) for a in (rc, rs1, rs2))


def _group_consts(seq_len):
    n1h, k1p = _fft_sizes(seq_len)
    w1, winv, g, gs, gi, gis = _dft_consts(seq_len)
    fc, fs, tc, ts = _twiddle_tables(seq_len)
    return dict(
        seq_len=seq_len, n1h=n1h, k1p=k1p,
        w1_half=jnp.asarray(w1[:, :n1h], BF16),
        winv=jnp.asarray(winv, BF16),
        g=jnp.asarray(g, F32), gs=jnp.asarray(gs, F32), gi=jnp.asarray(gi, F32), gis=jnp.asarray(gis, F32),
        fc=fc, fs=fs, tc=tc, ts=ts, rope=_rope_slabs(seq_len), feat=_filter_features(seq_len),
    )


def _pair(v):
    return jnp.concatenate([v, v])[None]


def _layer_weights(i, rms_mix, w_in, q_norm, k_norm, sink, w_short, b_short, filt_w1, filt_b1, filt_freq1,
                   filt_w2, filt_b2, filt_freq2, filt_w3, hyena_bias, norm_attn_out, norm_hyena_out,
                   w_out, rms_ffn, w_up, w_ffconv, b_ffconv, w_down, w_ple_gate, w_ple_proj):
    rep_q = D_ATTN // HEAD_DIM
    rep_k = D_KV // HEAD_DIM
    hid = FILTER_HID
    w1p = jnp.zeros((FEAT_W, hid), F32).at[:FILTER_EMB].set(filt_w1[i])
    zpad = jnp.zeros((FEAT_W, hid), F32)
    zsq = jnp.zeros((hid, hid), F32)
    z3 = jnp.zeros((hid, 2 * D_HYENA), F32)
    deltas = jnp.abs(jnp.linspace(MIN_DECAY, MAX_DECAY, D_HYENA, dtype=F32))
    return dict(
        g_mix=rms_mix[i][None], w_in=w_in,
        qg=(jnp.tile(q_norm[i], rep_q) * (HEAD_DIM ** -0.5))[None], kg=jnp.tile(k_norm[i], rep_k)[None],
        sink=sink[i], ws=w_short[i], bs=b_short[i][None],
        fw1a=jnp.concatenate([w1p, zpad], axis=1), fw1b=jnp.concatenate([zpad, w1p], axis=1),
        fb1=_pair(filt_b1[i]), fq1=_pair(filt_freq1[i]),
        fw2=jnp.block([[filt_w2[i], zsq], [zsq, filt_w2[i]]]),
        fb2=_pair(filt_b2[i]), fq2=_pair(filt_freq2[i]),
        fw3a=jnp.concatenate([filt_w3[i], z3], axis=0), fw3b=jnp.concatenate([z3, filt_w3[i]], axis=0),
        deltas=deltas[None], dbias=hyena_bias[i][None],
        g_attn=norm_attn_out[i][None], g_hy=norm_hyena_out[i][None],
        w_out=w_out, g_ffn=rms_ffn[i][None], w_up=w_up,
        wc=w_ffconv[i], bc=b_ffconv[i][None], w_down=w_down,
        w_gate=w_ple_gate, w_proj=w_ple_proj,
    )


def _layer(h, p_all, layer, lw, gc, bd, batch):
    L = gc["seq_len"]
    T = batch * L
    n1h, k1p = gc["n1h"], gc["k1p"]
    rc, rs1, rs2 = gc["rope"]
    tm = min(TILE_FFN, L)
    q, kd, va, x0, vv = _inproj(h, lw["g_mix"], lw["w_in"], lw["qg"], lw["kg"], bd, rc, rs1, rs2,
                                lw["ws"], lw["bs"], seq_len=L, tm=min(TILE_INPROJ, L), layer=layer)
    attn = _attn(lw["sink"], q, kd, va, lw["g_attn"], seq_len=L, tq=min(TILE_ATTN, L))
    filt = _filt(gc["feat"], lw["fw1a"], lw["fw1b"], lw["fb1"], lw["fq1"], lw["fw2"], lw["fb2"], lw["fq2"],
                 lw["fw3a"], lw["fw3b"], lw["deltas"], lw["dbias"], seq_len=L, tl=min(TILE_FILT, L))
    fa = _fft1(gc["w1_half"], filt.reshape(2, n1h, FFT_N2, D_HYENA), k1p=k1p)
    hspec = _fspec(gc["g"], gc["gs"], gc["fc"], gc["fs"], fa)
    a = _fft1(gc["w1_half"], vv.reshape(batch, n1h, FFT_N2, D_HYENA), k1p=k1p)
    a = _fft2(gc["g"], gc["gs"], gc["gi"], gc["gis"], gc["fc"], gc["fs"], gc["tc"], gc["ts"], a, hspec)
    y = _ifft1(gc["winv"], a).reshape(T, D_HYENA)
    return _ffn(h, attn, x0, y, p_all, lw["g_hy"], lw["w_out"], lw["g_ffn"], lw["w_up"], lw["wc"], lw["bc"],
                lw["w_down"], lw["w_gate"], lw["w_proj"], seq_len=L, tm=tm, layer=layer)


def _trunk(x, p, weights):
    batch, L, _ = x.shape
    gc = _group_consts(L)
    bd = jnp.asarray(np.kron(np.eye(MXU_W // HEAD_DIM), np.full((HEAD_DIM, HEAD_DIM), 1.0 / HEAD_DIM)), BF16)
    h = x.reshape(batch * L, D_MODEL)
    p_all = p.reshape(DEPTH * batch * L, D_PLE)
    for i in range(DEPTH):
        lw = _layer_weights(i, *weights)
        h = _layer(h, p_all, i, lw, gc, bd, batch)
    return h.reshape(batch, L, D_MODEL)


def kernel(x_prompt, x_sample, p_prompt, p_sample, rms_mix, w_in, q_norm, k_norm, sink, w_short, b_short, filt_w1, filt_b1, filt_freq1, filt_w2, filt_b2, filt_freq2, filt_w3, hyena_bias, norm_attn_out, norm_hyena_out, w_out, rms_ffn, w_up, w_ffconv, b_ffconv, w_down, w_ple_gate, w_ple_proj):
    w_in, w_out, w_up, w_down, w_ple_gate, w_ple_proj = (
        w.astype(BF16) for w in (w_in, w_out, w_up, w_down, w_ple_gate, w_ple_proj))
    weights = (rms_mix, w_in, q_norm, k_norm, sink, w_short, b_short, filt_w1, filt_b1, filt_freq1,
               filt_w2, filt_b2, filt_freq2, filt_w3, hyena_bias, norm_attn_out, norm_hyena_out,
               w_out, rms_ffn, w_up, w_ffconv, b_ffconv, w_down, w_ple_gate, w_ple_proj)
    y_prompt = _trunk(x_prompt, p_prompt, weights)
    y_sample = _trunk(x_sample, p_sample, weights)
    return (y_prompt, y_sample)
```

```python
import functools
import math

import numpy as np
import jax
import jax.numpy as jnp
from jax import lax
from jax.experimental import pallas as pl
from jax.experimental.pallas import tpu as pltpu

F32 = jnp.float32
BF16 = jnp.bfloat16

D_MODEL = 1024
DEPTH = 4
N_Q_HEADS = 8
N_KV_HEADS = 2
HEAD_DIM = 64
D_ATTN = N_Q_HEADS * HEAD_DIM
D_KV = N_KV_HEADS * HEAD_DIM
D_QKV = D_ATTN + 2 * D_KV
WINDOW = 128
BLOCK = 128
ROPE_THETA = 500000.0
ROPE_DIM = HEAD_DIM // 4
D_HYENA = 512
FILTER_EMB = 33
FILTER_HID = 64
FAST_DECAY_PCT = 0.3
SLOW_DECAY_PCT = 1.5
DECAY_TARGET = 1e-2
MAX_DECAY = math.log(DECAY_TARGET) / FAST_DECAY_PCT
MIN_DECAY = math.log(DECAY_TARGET) / SLOW_DECAY_PCT
D_IN = D_QKV + 3 * D_HYENA
D_FF = 2816
D_PLE = 256
EPS = 1e-6
MASKED = -1e30

LANES = 128
SUBLANES = 8
BF16_ROWS = 16
MXU_W = 256
Q_SLABS = N_Q_HEADS * LANES
FFT_N2 = 64
FFT_MG = 32
FFT_KB = 24
TILE_INPROJ = 1024
TILE_ATTN = 1024
TILE_FFN = 512
TILE_FILT = 512
VMEM_LIMIT = 56 * 1024 * 1024


def _cparams(n_axes):
    return pltpu.CompilerParams(dimension_semantics=("arbitrary",) * n_axes,
                                vmem_limit_bytes=VMEM_LIMIT)


def _resident(shape, layer=None):
    if layer is None:
        return pl.BlockSpec(shape, lambda *_: (0,) * len(shape), pipeline_mode=pl.Buffered(1))
    return pl.BlockSpec((None,) + tuple(shape), lambda *_: (layer,) + (0,) * len(shape),
                        pipeline_mode=pl.Buffered(1))


def _rms(x, g):
    ms = jnp.mean(x * x, axis=-1, keepdims=True)
    return x * lax.rsqrt(ms + EPS) * g


def _conv3(u, tm, w_ref, b_ref, c0, c1):
    rows = u.shape[0]
    up = pltpu.roll(u, 1, axis=0)[:tm]
    un = pltpu.roll(u, rows - 1, axis=0)[:tm]
    return up * w_ref[0:1, c0:c1] + u[:tm] * w_ref[1:2, c0:c1] + un * w_ref[2:3, c0:c1] + b_ref[:, c0:c1]


def _inproj_kernel(xm_ref, xp_ref, xn_ref, g_ref, w_ref, qg_ref, kg_ref, bd_ref,
                   rc_ref, rs1_ref, rs2_ref, ws_ref, bs_ref,
                   q_ref, k_ref, v_ref, x0_ref, vv_ref, *, tm, tiles_per_seq):
    i = pl.program_id(0)
    first = (i % tiles_per_seq) == 0
    last = (i % tiles_per_seq) == tiles_per_seq - 1
    xp = jnp.where(first, 0.0, xp_ref[...])
    xn = jnp.where(last, 0.0, xn_ref[...])
    x = jnp.concatenate([xm_ref[...], xn, xp], axis=0)
    n = _rms(x, g_ref[...]).astype(BF16)
    z = jnp.dot(n, w_ref[...], preferred_element_type=F32)

    bd = bd_ref[...]
    rc, rs1, rs2 = rc_ref[...], rs1_ref[...], rs2_ref[...]

    def norm_rope(zs, gain, bds):
        sq = (zs * zs).astype(BF16)
        step = bds.shape[0]
        ms = jnp.concatenate([jnp.dot(sq[:, c:c + step], bds, preferred_element_type=F32)
                              for c in range(0, zs.shape[1], step)], axis=1)
        y = zs * lax.rsqrt(ms + EPS) * gain
        outs = []
        for s in range(y.shape[1] // LANES):
            ys = y[:, s * LANES:(s + 1) * LANES]
            outs.append(ys * rc + pltpu.roll(ys, LANES - ROPE_DIM // 2, axis=1) * rs1
                        + pltpu.roll(ys, ROPE_DIM // 2, axis=1) * rs2)
        return outs

    qs = norm_rope(z[:tm, :D_ATTN], qg_ref[...], bd)
    low = lax.broadcasted_iota(jnp.int32, (tm, LANES), 1) < HEAD_DIM
    heads_per_kv = N_Q_HEADS // N_KV_HEADS
    for s, qv in enumerate(qs):
        qr = pltpu.roll(qv, HEAD_DIM, axis=1)
        for half in range(2):
            h = 2 * s + half
            src = qv if half == h // heads_per_kv else qr
            keep = low if h // heads_per_kv == 0 else jnp.logical_not(low)
            q_ref[:, h * LANES:(h + 1) * LANES] = jnp.where(keep, src, 0.0).astype(BF16)
    k_ref[...] = norm_rope(z[:tm, D_ATTN:D_ATTN + D_KV], kg_ref[...], bd[:D_KV, :D_KV])[0].astype(BF16)
    v_ref[...] = z[:tm, D_ATTN + D_KV:D_QKV].astype(BF16)

    u = _conv3(z[:, D_QKV:], tm, ws_ref, bs_ref, 0, 3 * D_HYENA)
    x0_ref[...] = u[:, :D_HYENA].astype(BF16)
    vv_ref[...] = (u[:, D_HYENA:2 * D_HYENA] * u[:, 2 * D_HYENA:]).astype(BF16)


def _inproj(x, g, w, qg, kg, bd, rc, rs1, rs2, ws, bs, *, seq_len, tm, layer):
    T = x.shape[0]
    tiles_per_seq = seq_len // tm
    r8 = tm // SUBLANES
    nb8 = T // SUBLANES
    rope_spec = pl.BlockSpec((tm, LANES), lambda i: (i % tiles_per_seq, 0))
    in_specs = [
        pl.BlockSpec((tm, D_MODEL), lambda i: (i, 0)),
        pl.BlockSpec((SUBLANES, D_MODEL), lambda i: (jnp.maximum(i * r8 - 1, 0), 0)),
        pl.BlockSpec((SUBLANES, D_MODEL), lambda i: (jnp.minimum((i + 1) * r8, nb8 - 1), 0)),
        _resident((1, D_MODEL)),
        _resident((D_MODEL, D_IN), layer),
        _resident((1, D_ATTN)),
        _resident((1, D_KV)),
        _resident((MXU_W, MXU_W)),
        rope_spec, rope_spec, rope_spec,
        _resident((3, 3 * D_HYENA)),
        _resident((1, 3 * D_HYENA)),
    ]
    widths = (Q_SLABS, D_KV, D_KV, D_HYENA, D_HYENA)
    out_specs = [pl.BlockSpec((tm, wd), lambda i: (i, 0)) for wd in widths]
    out_shape = [jax.ShapeDtypeStruct((T, wd), BF16) for wd in widths]
    return pl.pallas_call(
        functools.partial(_inproj_kernel, tm=tm, tiles_per_seq=tiles_per_seq),
        grid=(T // tm,), in_specs=in_specs, out_specs=out_specs, out_shape=out_shape,
        compiler_params=_cparams(1), name="inproj",
    )(x, x, x, g, w, qg, kg, bd, rc, rs1, rs2, ws, bs)


def _attn_kernel(sink_ref, q_ref, km_ref, kp_ref, kn_ref, vm_ref, vp_ref, vn_ref, g_ref,
                 o_ref, kbuf, vbuf, *, tq, tiles_per_seq):
    i = pl.program_id(0)
    first = (i % tiles_per_seq) == 0
    last = (i % tiles_per_seq) == tiles_per_seq - 1
    nblk = tq // BLOCK
    H = N_Q_HEADS
    G = N_Q_HEADS // N_KV_HEADS
    kbuf[0:BLOCK] = kp_ref[...]
    kbuf[BLOCK:BLOCK + tq] = km_ref[...]
    kbuf[BLOCK + tq:] = kn_ref[...]
    vbuf[0:BLOCK, :D_KV] = vp_ref[...]
    vbuf[BLOCK:BLOCK + tq, :D_KV] = vm_ref[...]
    vbuf[BLOCK + tq:, :D_KV] = vn_ref[...]
    vbuf[:, D_KV:] = jnp.ones((tq + 2 * BLOCK, LANES), BF16)
    r = lax.broadcasted_iota(jnp.int32, (BLOCK, 3 * BLOCK), 0)
    c = lax.broadcasted_iota(jnp.int32, (BLOCK, 3 * BLOCK), 1)
    d = c - BLOCK - r
    band = jnp.where((d >= -WINDOW) & (d <= WINDOW), 0.0, MASKED)
    low = lax.broadcasted_iota(jnp.int32, (BLOCK, LANES), 1) < HEAD_DIM
    rid = lax.broadcasted_iota(jnp.int32, (H * BLOCK, 1), 0)
    sk = jnp.zeros((H * BLOCK, 1), F32)
    for h in range(H):
        sk = jnp.where((rid >= h * BLOCK) & (rid < (h + 1) * BLOCK), sink_ref[h], sk)
    for j in range(nblk):
        bias = band
        if j == 0:
            bias = jnp.where(first & (c < BLOCK), MASKED, bias)
        if j == nblk - 1:
            bias = jnp.where(last & (c >= 2 * BLOCK), MASKED, bias)
        kb = kbuf[j * BLOCK:(j + 3) * BLOCK, :]
        vb = vbuf[j * BLOCK:(j + 3) * BLOCK, :]
        ql = jnp.concatenate([q_ref[j * BLOCK:(j + 1) * BLOCK, h * LANES:(h + 1) * LANES] for h in range(H)],
                             axis=0)
        s = lax.dot_general(ql, kb, (((1,), (1,)), ((), ())), preferred_element_type=F32)
        s = (s.reshape(H, BLOCK, 3 * BLOCK) + bias[None]).reshape(H * BLOCK, 3 * BLOCK)
        m = jnp.maximum(jnp.max(s, axis=-1, keepdims=True), sk)
        e = jnp.exp((s - m).astype(BF16))
        out = jnp.dot(e, vb, preferred_element_type=F32)
        res = out[:, :LANES] / (out[:, LANES:] + jnp.exp(sk - m))
        slabs = []
        for p in range(H // 2):
            ra = res[(2 * p) * BLOCK:(2 * p + 1) * BLOCK]
            rb = res[(2 * p + 1) * BLOCK:(2 * p + 2) * BLOCK]
            if (2 * p) // G == 0:
                slabs.append(jnp.where(low, ra, pltpu.roll(rb, HEAD_DIM, axis=1)))
            else:
                slabs.append(jnp.where(low, pltpu.roll(ra, HEAD_DIM, axis=1), rb))
        o = jnp.concatenate(slabs, axis=1)
        o_ref[j * BLOCK:(j + 1) * BLOCK, :] = _rms(o, g_ref[...]).astype(BF16)


def _attn(sink, q, k, v, g, *, seq_len, tq):
    T = q.shape[0]
    tiles_per_seq = seq_len // tq
    rb = tq // BLOCK
    nbb = T // BLOCK
    main = lambda i, s: (i, 0)
    prev = lambda i, s: (jnp.maximum(i * rb - 1, 0), 0)
    nxt = lambda i, s: (jnp.minimum((i + 1) * rb, nbb - 1), 0)
    kv_trio = [pl.BlockSpec((tq, D_KV), main), pl.BlockSpec((BLOCK, D_KV), prev), pl.BlockSpec((BLOCK, D_KV), nxt)]
    grid_spec = pltpu.PrefetchScalarGridSpec(
        num_scalar_prefetch=1, grid=(T // tq,),
        in_specs=[pl.BlockSpec((tq, Q_SLABS), main)] + kv_trio + kv_trio + [
            pl.BlockSpec((1, D_ATTN), lambda i, s: (0, 0))],
        out_specs=pl.BlockSpec((tq, D_ATTN), main),
        scratch_shapes=[pltpu.VMEM((tq + 2 * BLOCK, D_KV), BF16), pltpu.VMEM((tq + 2 * BLOCK, D_KV + LANES), BF16)],
    )
    return pl.pallas_call(
        functools.partial(_attn_kernel, tq=tq, tiles_per_seq=tiles_per_seq),
        grid_spec=grid_spec, out_shape=jax.ShapeDtypeStruct((T, D_ATTN), BF16),
        compiler_params=_cparams(1), name="attn",
    )(sink, q, k, k, k, v, v, v, g)


def _fft_sizes(seq_len):
    n1h = seq_len // FFT_N2
    k1p = -(-(n1h + 1) // FFT_KB) * FFT_KB
    return n1h, k1p


@functools.lru_cache(maxsize=None)
def _dft_consts(seq_len):
    n1h, k1p = _fft_sizes(seq_len)
    n1 = 2 * n1h
    k = np.arange(k1p)[:, None]
    valid = (k <= n1h)
    th = 2.0 * np.pi * ((k * np.arange(n1)[None, :]) % n1) / n1
    w1 = np.concatenate([np.cos(th) * valid, -np.sin(th) * valid], axis=0)
    scale = np.where((k == 0) | (k == n1h), 1.0, 2.0) * valid / n1
    thh = th[:, :n1h]
    winv = np.concatenate([np.cos(thh) * scale, -np.sin(thh) * scale], axis=0).T
    a = 2.0 * np.pi * ((np.arange(FFT_N2)[:, None] * np.arange(FFT_N2)[None, :]) % FFT_N2) / FFT_N2
    cm, sm = np.cos(a), np.sin(a)
    g = np.block([[cm, sm], [-sm, cm]])
    gi = np.block([[cm, -sm], [sm, cm]]) / FFT_N2
    gs = np.concatenate([g[:, FFT_N2:], g[:, :FFT_N2]], axis=1)
    gis = np.concatenate([gi[FFT_N2:], gi[:FFT_N2]], axis=0)
    return w1, winv, g, gs, gi, gis


def _twiddle_tables(seq_len):
    n1h, k1p = _fft_sizes(seq_len)
    n = 2 * seq_len
    k = jnp.arange(k1p, dtype=jnp.int32)[:, None]
    m = jnp.arange(FFT_N2, dtype=jnp.int32)[None, :]
    ph = ((k * m) % n).astype(F32) * (2.0 * math.pi / n)
    cs, sn = jnp.cos(ph), jnp.sin(ph)
    fc = jnp.concatenate([cs, cs], axis=1)
    fs = jnp.concatenate([-sn, sn], axis=1)
    tc = jnp.broadcast_to(cs[:, :, None], (k1p, FFT_N2, LANES))
    ts = jnp.broadcast_to(sn[:, :, None], (k1p, FFT_N2, LANES))
    return fc, fs, tc, ts


def _fft1_kernel(w_ref, x_ref, o_ref, *, k1p):
    a = jnp.einsum('kn,nmc->kmc', w_ref[...], x_ref[...], preferred_element_type=F32)
    o_ref[0] = a[:k1p].astype(BF16)
    o_ref[1] = a[k1p:].astype(BF16)


def _fft1(w1, x, *, k1p):
    B, n1in, _, C = x.shape
    grid = (FFT_N2 // FFT_MG, B, C // LANES)
    return pl.pallas_call(
        functools.partial(_fft1_kernel, k1p=k1p),
        grid=grid,
        in_specs=[
            _resident((2 * k1p, n1in)),
            pl.BlockSpec((None, n1in, FFT_MG, LANES), lambda m, b, j: (b, 0, m, j)),
        ],
        out_specs=pl.BlockSpec((None, 2, k1p, FFT_MG, LANES), lambda m, b, j: (b, 0, 0, m, j)),
        out_shape=jax.ShapeDtypeStruct((B, 2, k1p, FFT_N2, C), BF16),
        compiler_params=_cparams(3), name="fft1",
    )(w1, x)


def _pair_diag(a, b):
    z = jnp.zeros_like(a)
    return jnp.concatenate([jnp.concatenate([a, z], axis=1), jnp.concatenate([z, b], axis=1)], axis=0).astype(BF16)


def _fwd_pair(g_ref, gs_ref, fc_ref, fs_ref, a_ref, j):
    def mat(jj):
        return g_ref[...] * fc_ref[jj:jj + 1, :] + gs_ref[...] * fs_ref[jj:jj + 1, :]
    x = jnp.concatenate([a_ref[0, j], a_ref[1, j], a_ref[0, j + 1], a_ref[1, j + 1]], axis=0)
    return jnp.dot(_pair_diag(mat(j), mat(j + 1)), x, preferred_element_type=F32)


def _fspec_kernel(g_ref, gs_ref, fc_ref, fs_ref, a_ref, o_ref):
    n = FFT_N2
    for j in range(0, FFT_KB, 2):
        bf = _fwd_pair(g_ref, gs_ref, fc_ref, fs_ref, a_ref.at[0], j)
        bb = _fwd_pair(g_ref, gs_ref, fc_ref, fs_ref, a_ref.at[1], j)
        for t in range(2):
            re = slice(2 * t * n, (2 * t + 1) * n)
            im = slice((2 * t + 1) * n, (2 * t + 2) * n)
            o_ref[0, j + t] = (bf[re] + bb[re]).astype(BF16)
            o_ref[1, j + t] = (bf[im] - bb[im]).astype(BF16)


def _fspec(g, gs, fc, fs, a):
    _, _, k1p, _, C = a.shape
    row = pl.BlockSpec((FFT_KB, 2 * FFT_N2), lambda kb: (kb, 0))
    return pl.pallas_call(
        _fspec_kernel, grid=(k1p // FFT_KB,),
        in_specs=[_resident((2 * FFT_N2, 2 * FFT_N2)), _resident((2 * FFT_N2, 2 * FFT_N2)), row, row,
                  pl.BlockSpec((2, 2, FFT_KB, FFT_N2, C), lambda kb: (0, 0, kb, 0, 0))],
        out_specs=pl.BlockSpec((2, FFT_KB, FFT_N2, C), lambda kb: (0, kb, 0, 0)),
        out_shape=jax.ShapeDtypeStruct(a.shape[1:], BF16),
        compiler_params=_cparams(1), name="fspec",
    )(g, gs, fc, fs, a)


def _fft2_kernel(g_ref, gs_ref, gi_ref, gis_ref, fc_ref, fs_ref, tc_ref, ts_ref, a_ref, h_ref, o_ref):
    n = FFT_N2

    def inv(jj):
        tc, ts = tc_ref[jj], ts_ref[jj]
        return (gi_ref[...] * jnp.concatenate([tc, tc], axis=0)
                + gis_ref[...] * jnp.concatenate([-ts, ts], axis=0))

    for j in range(0, FFT_KB, 2):
        b = _fwd_pair(g_ref, gs_ref, fc_ref, fs_ref, a_ref, j)
        ys = []
        for t in range(2):
            br, bi = b[2 * t * n:(2 * t + 1) * n], b[(2 * t + 1) * n:(2 * t + 2) * n]
            hr, hi = h_ref[0, j + t].astype(F32), h_ref[1, j + t].astype(F32)
            ys += [br * hr - bi * hi, br * hi + bi * hr]
        y = jnp.concatenate(ys, axis=0).astype(BF16)
        p = jnp.dot(_pair_diag(inv(j), inv(j + 1)), y, preferred_element_type=F32)
        for t in range(2):
            o_ref[0, j + t] = p[2 * t * n:(2 * t + 1) * n].astype(BF16)
            o_ref[1, j + t] = p[(2 * t + 1) * n:(2 * t + 2) * n].astype(BF16)


def _fft2(g, gs, gi, gis, fc, fs, tc, ts, a, hspec):
    B, _, k1p, _, C = a.shape
    mat = _resident((2 * FFT_N2, 2 * FFT_N2))
    row = pl.BlockSpec((FFT_KB, 2 * FFT_N2), lambda kb, b: (kb, 0))
    col = pl.BlockSpec((FFT_KB, FFT_N2, LANES), lambda kb, b: (kb, 0, 0))
    return pl.pallas_call(
        _fft2_kernel, grid=(k1p // FFT_KB, B),
        in_specs=[
            mat, mat, mat, mat, row, row, col, col,
            pl.BlockSpec((None, 2, FFT_KB, FFT_N2, C), lambda kb, b: (b, 0, kb, 0, 0)),
            pl.BlockSpec((2, FFT_KB, FFT_N2, C), lambda kb, b: (0, kb, 0, 0)),
        ],
        out_specs=pl.BlockSpec((None, 2, FFT_KB, FFT_N2, C), lambda kb, b: (b, 0, kb, 0, 0)),
        out_shape=jax.ShapeDtypeStruct(a.shape, BF16),
        compiler_params=_cparams(2), name="fft2",
    )(g, gs, gi, gis, fc, fs, tc, ts, a, hspec)


def _ifft1_kernel(w_ref, a_ref, o_ref):
    a = jnp.concatenate([a_ref[0], a_ref[1]], axis=0)
    o_ref[...] = jnp.einsum('nk,kmc->nmc', w_ref[...], a, preferred_element_type=F32).astype(BF16)


def _ifft1(winv, a):
    B, _, k1p, _, C = a.shape
    n1h = winv.shape[0]
    return pl.pallas_call(
        _ifft1_kernel, grid=(FFT_N2 // FFT_MG, B, C // LANES),
        in_specs=[
            _resident((n1h, 2 * k1p)),
            pl.BlockSpec((None, 2, k1p, FFT_MG, LANES), lambda m, b, j: (b, 0, 0, m, j)),
        ],
        out_specs=pl.BlockSpec((None, n1h, FFT_MG, LANES), lambda m, b, j: (b, 0, m, j)),
        out_shape=jax.ShapeDtypeStruct((B, n1h, FFT_N2, C), BF16),
        compiler_params=_cparams(3), name="ifft1",
    )(winv, a)


def _split_dot(a, b):
    ah = a.astype(BF16)
    al = (a - ah.astype(F32)).astype(BF16)
    bh = b.astype(BF16)
    bl = (b - bh.astype(F32)).astype(BF16)
    d = functools.partial(jnp.dot, preferred_element_type=F32)
    return d(ah, bh) + d(ah, bl) + d(al, bh)


def _filt_kernel(ft_ref, w1a_ref, w1b_ref, b1_ref, q1_ref, w2_ref, b2_ref, q2_ref, w3a_ref, w3b_ref,
                 dl_ref, db_ref, o_ref, *, seq_len, tl):
    i = pl.program_id(0)
    L = seq_len
    hl = tl // 2
    pre = _split_dot(ft_ref[:hl, :], w1a_ref[...]) + _split_dot(ft_ref[hl:, :], w1b_ref[...])
    h = jnp.sin(q1_ref[...] * (pre + b1_ref[...]))
    h = jnp.sin(q2_ref[...] * (_split_dot(h, w2_ref[...]) + b2_ref[...]))
    for half, w3_ref in enumerate((w3a_ref, w3b_ref)):
        hh = _split_dot(h, w3_ref[...])
        n = i * tl + half * hl + lax.broadcasted_iota(jnp.int32, (hl, D_HYENA), 0)
        decay = jnp.exp(-(n.astype(F32) / (L - 1)) * dl_ref[...])
        rows = slice(half * hl, (half + 1) * hl)
        o_ref[0, rows, :] = (hh[:, :D_HYENA] * decay + jnp.where(n == 0, db_ref[...], 0.0)).astype(BF16)
        o_ref[1, rows, :] = jnp.where(n == 0, 0.0, hh[:, D_HYENA:] * decay).astype(BF16)


def _filt(feat, w1a, w1b, b1, q1, w2, b2, q2, w3a, w3b, dl, db, *, seq_len, tl):
    return pl.pallas_call(
        functools.partial(_filt_kernel, seq_len=seq_len, tl=tl),
        grid=(seq_len // tl,),
        in_specs=[
            pl.BlockSpec((tl, FEAT_W), lambda i: (i, 0)),
            _resident((FEAT_W, LANES)), _resident((FEAT_W, LANES)), _resident((1, LANES)), _resident((1, LANES)),
            _resident((LANES, LANES)), _resident((1, LANES)), _resident((1, LANES)),
            _resident((LANES, 2 * D_HYENA)), _resident((LANES, 2 * D_HYENA)),
            _resident((1, D_HYENA)), _resident((1, D_HYENA)),
        ],
        out_specs=pl.BlockSpec((2, tl, D_HYENA), lambda i: (0, i, 0)),
        out_shape=jax.ShapeDtypeStruct((2, seq_len, D_HYENA), BF16),
        compiler_params=_cparams(1), name="filt",
    )(feat, w1a, w1b, b1, q1, w2, b2, q2, w3a, w3b, dl, db)


FEAT_W = 40


@functools.lru_cache(maxsize=None)
def _filter_features(seq_len):
    L = seq_len
    bands = (FILTER_EMB - 1) // 2
    pos = np.arange(L, dtype=np.float64)[:, None]
    t = pos / (L - 1)
    w = 2.0 * np.pi * pos / L
    f = np.linspace(1e-4, bands - 1, bands)[None]
    pad = np.zeros((L, FEAT_W - FILTER_EMB))
    return np.concatenate([t, np.cos(f * w), -np.sin(f * w), pad], axis=1).astype(np.float32)


FF_CHUNK = 256


def _ffn_kernel(hm_ref, hp_ref, hn_ref, am_ref, ap_ref, an_ref, xm_ref, xp_ref, xn_ref,
                ym_ref, yp_ref, yn_ref, p_ref, gh_ref, wo_ref, gf_ref, wu_ref, wc_ref, bc_ref,
                wd_ref, wg_ref, wp_ref, o_ref, act_ref, *, tm, tiles_per_seq):
    i = pl.program_id(0)
    first = (i % tiles_per_seq) == 0
    last = (i % tiles_per_seq) == tiles_per_seq - 1

    def stack(m_ref, p_ref, n_ref):
        p = p_ref[BF16_ROWS - SUBLANES:, :]
        n = n_ref[:SUBLANES, :]
        p = jnp.where(first, jnp.zeros_like(p), p)
        n = jnp.where(last, jnp.zeros_like(n), n)
        return jnp.concatenate([m_ref[...], n, p], axis=0)

    h = stack(hm_ref, hp_ref, hn_ref)
    a = stack(am_ref, ap_ref, an_ref)
    hy = stack(xm_ref, xp_ref, xn_ref).astype(F32) * stack(ym_ref, yp_ref, yn_ref).astype(F32)
    hyn = _rms(hy, gh_ref[...]).astype(BF16)
    h1 = (h + jnp.dot(a, wo_ref[:D_ATTN, :], preferred_element_type=F32)
          + jnp.dot(hyn, wo_ref[D_ATTN:, :], preferred_element_type=F32))
    n2 = _rms(h1, gf_ref[...]).astype(BF16)

    def conv(c0):
        u = jnp.dot(n2, wu_ref[:, c0:c0 + FF_CHUNK], preferred_element_type=F32)
        return _conv3(u, tm, wc_ref, bc_ref, c0, c0 + FF_CHUNK)

    for cidx in range(D_FF // FF_CHUNK):
        c0 = cidx * FF_CHUNK
        ua = conv(c0)
        ug = conv(D_FF + c0)
        act_ref[:, c0:c0 + FF_CHUNK] = (ug * jax.nn.sigmoid(ug) * ua).astype(BF16)

    h2 = h1[:tm] + jnp.dot(act_ref[...], wd_ref[...], preferred_element_type=F32)
    gate = jax.nn.sigmoid(jnp.dot(h2.astype(BF16), wg_ref[...], preferred_element_type=F32))
    pp = jnp.dot(p_ref[...].astype(BF16), wp_ref[...], preferred_element_type=F32)
    o_ref[...] = h2 + gate * pp


def _ffn(h, attn, x0, y, p, gh, wo, gf, wu, wc, bc, wd, wg, wp, *, seq_len, tm, layer):
    p_tile0 = layer * (h.shape[0] // tm)
    T = h.shape[0]
    tiles_per_seq = seq_len // tm
    r16 = tm // BF16_ROWS
    nb16 = T // BF16_ROWS
    main = lambda i: (i, 0)
    prev = lambda i: (jnp.maximum(i * r16 - 1, 0), 0)
    nxt = lambda i: (jnp.minimum((i + 1) * r16, nb16 - 1), 0)

    def trio(width):
        return [pl.BlockSpec((tm, width), main), pl.BlockSpec((BF16_ROWS, width), prev),
                pl.BlockSpec((BF16_ROWS, width), nxt)]

    in_specs = (trio(D_MODEL) + trio(D_ATTN) + trio(D_HYENA) + trio(D_HYENA) + [
        pl.BlockSpec((tm, D_PLE), lambda i: (p_tile0 + i, 0)),
        _resident((1, D_HYENA)),
        _resident((D_MODEL, D_MODEL), layer),
        _resident((1, D_MODEL)),
        _resident((D_MODEL, 2 * D_FF), layer),
        _resident((3, 2 * D_FF)),
        _resident((1, 2 * D_FF)),
        _resident((D_FF, D_MODEL), layer),
        _resident((D_MODEL, D_MODEL), layer),
        _resident((D_PLE, D_MODEL), layer),
    ])
    return pl.pallas_call(
        functools.partial(_ffn_kernel, tm=tm, tiles_per_seq=tiles_per_seq),
        grid=(T // tm,), in_specs=in_specs,
        out_specs=pl.BlockSpec((tm, D_MODEL), main),
        out_shape=jax.ShapeDtypeStruct((T, D_MODEL), F32),
        scratch_shapes=[pltpu.VMEM((tm, D_FF), BF16)],
        compiler_params=_cparams(1), name="ffn",
    )(h, h, h, attn, attn, attn, x0, x0, x0, y, y, y, p, gh, wo, gf, wu, wc, bc, wd, wg, wp)


def _rope_slabs(seq_len):
    half = ROPE_DIM // 2
    inv = ROPE_THETA ** (-jnp.arange(0, ROPE_DIM, 2, dtype=F32) / ROPE_DIM)
    ang = jnp.arange(seq_len, dtype=F32)[:, None] * inv[None]
    cos, sin = jnp.cos(ang), jnp.sin(ang)
    ones = jnp.ones((seq_len, HEAD_DIM - ROPE_DIM), F32)
    zeros = jnp.zeros((seq_len, HEAD_DIM - ROPE_DIM), F32)
    zh = jnp.zeros((seq_len, half), F32)
    rc = jnp.concatenate([cos, cos, ones], axis=1)
    rs1 = jnp.concatenate([-sin, zh, zeros], axis=1)
    rs2 = jnp.concatenate([zh, sin, zeros], axis=1)
    rep = LANES // HEAD_DIM
    return tuple(jnp.tile(a, (1, rep)) for a in (rc, rs1, rs2))


def _group_consts(seq_len):
    n1h, k1p = _fft_sizes(seq_len)
    w1, winv, g, gs, gi, gis = _dft_consts(seq_len)
    fc, fs, tc, ts = _twiddle_tables(seq_len)
    return dict(
        seq_len=seq_len, n1h=n1h, k1p=k1p,
        w1_half=jnp.asarray(w1[:, :n1h], BF16),
        winv=jnp.asarray(winv, BF16),
        g=jnp.asarray(g, F32), gs=jnp.asarray(gs, F32), gi=jnp.asarray(gi, F32), gis=jnp.asarray(gis, F32),
        fc=fc, fs=fs, tc=tc, ts=ts, rope=_rope_slabs(seq_len), feat=_filter_features(seq_len),
    )


def _pair(v):
    return jnp.concatenate([v, v])[None]


def _layer_weights(i, rms_mix, w_in, q_norm, k_norm, sink, w_short, b_short, filt_w1, filt_b1, filt_freq1,
                   filt_w2, filt_b2, filt_freq2, filt_w3, hyena_bias, norm_attn_out, norm_hyena_out,
                   w_out, rms_ffn, w_up, w_ffconv, b_ffconv, w_down, w_ple_gate, w_ple_proj):
    rep_q = D_ATTN // HEAD_DIM
    rep_k = D_KV // HEAD_DIM
    hid = FILTER_HID
    w1p = jnp.zeros((FEAT_W, hid), F32).at[:FILTER_EMB].set(filt_w1[i])
    zpad = jnp.zeros((FEAT_W, hid), F32)
    zsq = jnp.zeros((hid, hid), F32)
    z3 = jnp.zeros((hid, 2 * D_HYENA), F32)
    deltas = jnp.abs(jnp.linspace(MIN_DECAY, MAX_DECAY, D_HYENA, dtype=F32))
    return dict(
        g_mix=rms_mix[i][None], w_in=w_in,
        qg=(jnp.tile(q_norm[i], rep_q) * (HEAD_DIM ** -0.5))[None], kg=jnp.tile(k_norm[i], rep_k)[None],
        sink=sink[i], ws=w_short[i], bs=b_short[i][None],
        fw1a=jnp.concatenate([w1p, zpad], axis=1), fw1b=jnp.concatenate([zpad, w1p], axis=1),
        fb1=_pair(filt_b1[i]), fq1=_pair(filt_freq1[i]),
        fw2=jnp.block([[filt_w2[i], zsq], [zsq, filt_w2[i]]]),
        fb2=_pair(filt_b2[i]), fq2=_pair(filt_freq2[i]),
        fw3a=jnp.concatenate([filt_w3[i], z3], axis=0), fw3b=jnp.concatenate([z3, filt_w3[i]], axis=0),
        deltas=deltas[None], dbias=hyena_bias[i][None],
        g_attn=norm_attn_out[i][None], g_hy=norm_hyena_out[i][None],
        w_out=w_out, g_ffn=rms_ffn[i][None], w_up=w_up,
        wc=w_ffconv[i], bc=b_ffconv[i][None], w_down=w_down,
        w_gate=w_ple_gate, w_proj=w_ple_proj,
    )


def _layer(h, p_all, layer, lw, gc, bd, batch):
    L = gc["seq_len"]
    T = batch * L
    n1h, k1p = gc["n1h"], gc["k1p"]
    rc, rs1, rs2 = gc["rope"]
    tm = min(TILE_FFN, L)
    q, k, v, x0, vv = _inproj(h, lw["g_mix"], lw["w_in"], lw["qg"], lw["kg"], bd, rc, rs1, rs2,
                                lw["ws"], lw["bs"], seq_len=L, tm=min(TILE_INPROJ, L), layer=layer)
    attn = _attn(lw["sink"], q, k, v, lw["g_attn"], seq_len=L, tq=min(TILE_ATTN, L))
    filt = _filt(gc["feat"], lw["fw1a"], lw["fw1b"], lw["fb1"], lw["fq1"], lw["fw2"], lw["fb2"], lw["fq2"],
                 lw["fw3a"], lw["fw3b"], lw["deltas"], lw["dbias"], seq_len=L, tl=min(TILE_FILT, L))
    fa = _fft1(gc["w1_half"], filt.reshape(2, n1h, FFT_N2, D_HYENA), k1p=k1p)
    hspec = _fspec(gc["g"], gc["gs"], gc["fc"], gc["fs"], fa)
    a = _fft1(gc["w1_half"], vv.reshape(batch, n1h, FFT_N2, D_HYENA), k1p=k1p)
    a = _fft2(gc["g"], gc["gs"], gc["gi"], gc["gis"], gc["fc"], gc["fs"], gc["tc"], gc["ts"], a, hspec)
    y = _ifft1(gc["winv"], a).reshape(T, D_HYENA)
    return _ffn(h, attn, x0, y, p_all, lw["g_hy"], lw["w_out"], lw["g_ffn"], lw["w_up"], lw["wc"], lw["bc"],
                lw["w_down"], lw["w_gate"], lw["w_proj"], seq_len=L, tm=tm, layer=layer)


def _trunk(x, p, weights):
    batch, L, _ = x.shape
    gc = _group_consts(L)
    bd = jnp.asarray(np.kron(np.eye(MXU_W // HEAD_DIM), np.full((HEAD_DIM, HEAD_DIM), 1.0 / HEAD_DIM)), BF16)
    h = x.reshape(batch * L, D_MODEL)
    p_all = p.reshape(DEPTH * batch * L, D_PLE)
    for i in range(DEPTH):
        lw = _layer_weights(i, *weights)
        h = _layer(h, p_all, i, lw, gc, bd, batch)
    return h.reshape(batch, L, D_MODEL)


def kernel(x_prompt, x_sample, p_prompt, p_sample, rms_mix, w_in, q_norm, k_norm, sink, w_short, b_short, filt_w1, filt_b1, filt_freq1, filt_w2, filt_b2, filt_freq2, filt_w3, hyena_bias, norm_attn_out, norm_hyena_out, w_out, rms_ffn, w_up, w_ffconv, b_ffconv, w_down, w_ple_gate, w_ple_proj):
    w_in, w_out, w_up, w_down, w_ple_gate, w_ple_proj = (
        w.astype(BF16) for w in (w_in, w_out, w_up, w_down, w_ple_gate, w_ple_proj))
    weights = (rms_mix, w_in, q_norm, k_norm, sink, w_short, b_short, filt_w1, filt_b1, filt_freq1,
               filt_w2, filt_b2, filt_freq2, filt_w3, hyena_bias, norm_attn_out, norm_hyena_out,
               w_out, rms_ffn, w_up, w_ffconv, b_ffconv, w_down, w_ple_gate, w_ple_proj)
    y_prompt = _trunk(x_prompt, p_prompt, weights)
    y_sample = _trunk(x_sample, p_sample, weights)
    return (y_prompt, y_sample)
```

```python
import functools
import math

import numpy as np
import jax
import jax.numpy as jnp
from jax import lax
from jax.experimental import pallas as pl
from jax.experimental.pallas import tpu as pltpu

F32 = jnp.float32
BF16 = jnp.bfloat16

D_MODEL = 1024
DEPTH = 4
N_Q_HEADS = 8
N_KV_HEADS = 2
HEAD_DIM = 64
D_ATTN = N_Q_HEADS * HEAD_DIM
D_KV = N_KV_HEADS * HEAD_DIM
D_QKV = D_ATTN + 2 * D_KV
WINDOW = 128
BLOCK = 128
ROPE_THETA = 500000.0
ROPE_DIM = HEAD_DIM // 4
D_HYENA = 512
FILTER_EMB = 33
FILTER_HID = 64
FAST_DECAY_PCT = 0.3
SLOW_DECAY_PCT = 1.5
DECAY_TARGET = 1e-2
MAX_DECAY = math.log(DECAY_TARGET) / FAST_DECAY_PCT
MIN_DECAY = math.log(DECAY_TARGET) / SLOW_DECAY_PCT
D_IN = D_QKV + 3 * D_HYENA
D_FF = 2816
D_PLE = 256
EPS = 1e-6
MASKED = -1e30

LANES = 128
SUBLANES = 8
BF16_ROWS = 16
MXU_W = 256
Q_SLABS = N_Q_HEADS * LANES
FFT_N2 = 64
FFT_MG = 32
FFT_KB = 24
TILE_INPROJ = 1024
TILE_ATTN = 2048
TILE_FFN = 512
TILE_FILT = 2048
VMEM_LIMIT = 56 * 1024 * 1024


def _cparams(n_axes):
    return pltpu.CompilerParams(dimension_semantics=("arbitrary",) * n_axes,
                                vmem_limit_bytes=VMEM_LIMIT)


def _resident(shape, layer=None):
    if layer is None:
        return pl.BlockSpec(shape, lambda *_: (0,) * len(shape), pipeline_mode=pl.Buffered(1))
    return pl.BlockSpec((None,) + tuple(shape), lambda *_: (layer,) + (0,) * len(shape),
                        pipeline_mode=pl.Buffered(1))


def _rms(x, g):
    ms = jnp.mean(x * x, axis=-1, keepdims=True)
    return x * lax.rsqrt(ms + EPS) * g


def _conv3(u, tm, w_ref, b_ref, c0, c1):
    rows = u.shape[0]
    up = pltpu.roll(u, 1, axis=0)[:tm]
    un = pltpu.roll(u, rows - 1, axis=0)[:tm]
    return up * w_ref[0:1, c0:c1] + u[:tm] * w_ref[1:2, c0:c1] + un * w_ref[2:3, c0:c1] + b_ref[:, c0:c1]


def _inproj_kernel(xm_ref, xp_ref, xn_ref, g_ref, w_ref, qg_ref, kg_ref, bd_ref,
                   rc_ref, rs1_ref, rs2_ref, ws_ref, bs_ref,
                   q_ref, k_ref, v_ref, x0_ref, vv_ref, *, tm, tiles_per_seq):
    i = pl.program_id(0)
    first = (i % tiles_per_seq) == 0
    last = (i % tiles_per_seq) == tiles_per_seq - 1
    xp = jnp.where(first, 0.0, xp_ref[...])
    xn = jnp.where(last, 0.0, xn_ref[...])
    x = jnp.concatenate([xm_ref[...], xn, xp], axis=0)
    n = _rms(x, g_ref[...]).astype(BF16)
    z = jnp.dot(n, w_ref[...], preferred_element_type=F32)

    bd = bd_ref[...]
    rc, rs1, rs2 = rc_ref[...], rs1_ref[...], rs2_ref[...]

    def norm_rope(zs, gain, bds):
        sq = (zs * zs).astype(BF16)
        step = bds.shape[0]
        ms = jnp.concatenate([jnp.dot(sq[:, c:c + step], bds, preferred_element_type=F32)
                              for c in range(0, zs.shape[1], step)], axis=1)
        y = zs * lax.rsqrt(ms + EPS) * gain
        outs = []
        for s in range(y.shape[1] // LANES):
            ys = y[:, s * LANES:(s + 1) * LANES]
            outs.append(ys * rc + pltpu.roll(ys, LANES - ROPE_DIM // 2, axis=1) * rs1
                        + pltpu.roll(ys, ROPE_DIM // 2, axis=1) * rs2)
        return outs

    qs = norm_rope(z[:tm, :D_ATTN], qg_ref[...], bd)
    low = lax.broadcasted_iota(jnp.int32, (tm, LANES), 1) < HEAD_DIM
    heads_per_kv = N_Q_HEADS // N_KV_HEADS
    for s, qv in enumerate(qs):
        qr = pltpu.roll(qv, HEAD_DIM, axis=1)
        for half in range(2):
            h = 2 * s + half
            src = qv if half == h // heads_per_kv else qr
            keep = low if h // heads_per_kv == 0 else jnp.logical_not(low)
            q_ref[:, h * LANES:(h + 1) * LANES] = jnp.where(keep, src, 0.0).astype(BF16)
    k_ref[...] = norm_rope(z[:tm, D_ATTN:D_ATTN + D_KV], kg_ref[...], bd[:D_KV, :D_KV])[0].astype(BF16)
    v_ref[...] = z[:tm, D_ATTN + D_KV:D_QKV].astype(BF16)

    u = _conv3(z[:, D_QKV:], tm, ws_ref, bs_ref, 0, 3 * D_HYENA)
    x0_ref[...] = u[:, :D_HYENA].astype(BF16)
    vv_ref[...] = (u[:, D_HYENA:2 * D_HYENA] * u[:, 2 * D_HYENA:]).astype(BF16)


def _inproj(x, g, w, qg, kg, bd, rc, rs1, rs2, ws, bs, *, seq_len, tm, layer):
    T = x.shape[0]
    tiles_per_seq = seq_len // tm
    r8 = tm // SUBLANES
    nb8 = T // SUBLANES
    rope_spec = pl.BlockSpec((tm, LANES), lambda i: (i % tiles_per_seq, 0))
    in_specs = [
        pl.BlockSpec((tm, D_MODEL), lambda i: (i, 0)),
        pl.BlockSpec((SUBLANES, D_MODEL), lambda i: (jnp.maximum(i * r8 - 1, 0), 0)),
        pl.BlockSpec((SUBLANES, D_MODEL), lambda i: (jnp.minimum((i + 1) * r8, nb8 - 1), 0)),
        _resident((1, D_MODEL)),
        _resident((D_MODEL, D_IN), layer),
        _resident((1, D_ATTN)),
        _resident((1, D_KV)),
        _resident((MXU_W, MXU_W)),
        rope_spec, rope_spec, rope_spec,
        _resident((3, 3 * D_HYENA)),
        _resident((1, 3 * D_HYENA)),
    ]
    widths = (Q_SLABS, D_KV, D_KV, D_HYENA, D_HYENA)
    out_specs = [pl.BlockSpec((tm, wd), lambda i: (i, 0)) for wd in widths]
    out_shape = [jax.ShapeDtypeStruct((T, wd), BF16) for wd in widths]
    return pl.pallas_call(
        functools.partial(_inproj_kernel, tm=tm, tiles_per_seq=tiles_per_seq),
        grid=(T // tm,), in_specs=in_specs, out_specs=out_specs, out_shape=out_shape,
        compiler_params=_cparams(1), name="inproj",
    )(x, x, x, g, w, qg, kg, bd, rc, rs1, rs2, ws, bs)


def _attn_kernel(sink_ref, q_ref, km_ref, kp_ref, kn_ref, vm_ref, vp_ref, vn_ref, g_ref,
                 o_ref, kbuf, vbuf, *, tq, tiles_per_seq):
    i = pl.program_id(0)
    first = (i % tiles_per_seq) == 0
    last = (i % tiles_per_seq) == tiles_per_seq - 1
    nblk = tq // BLOCK
    H = N_Q_HEADS
    G = N_Q_HEADS // N_KV_HEADS
    kbuf[0:BLOCK] = kp_ref[...]
    kbuf[BLOCK:BLOCK + tq] = km_ref[...]
    kbuf[BLOCK + tq:] = kn_ref[...]
    vbuf[0:BLOCK, :D_KV] = vp_ref[...]
    vbuf[BLOCK:BLOCK + tq, :D_KV] = vm_ref[...]
    vbuf[BLOCK + tq:, :D_KV] = vn_ref[...]
    vbuf[:, D_KV:] = jnp.ones((tq + 2 * BLOCK, LANES), BF16)
    r = lax.broadcasted_iota(jnp.int32, (BLOCK, 3 * BLOCK), 0)
    c = lax.broadcasted_iota(jnp.int32, (BLOCK, 3 * BLOCK), 1)
    d = c - BLOCK - r
    band = jnp.where((d >= -WINDOW) & (d <= WINDOW), 0.0, MASKED)
    low = lax.broadcasted_iota(jnp.int32, (BLOCK, LANES), 1) < HEAD_DIM
    rid = lax.broadcasted_iota(jnp.int32, (H * BLOCK, 1), 0)
    sk = jnp.zeros((H * BLOCK, 1), F32)
    for h in range(H):
        sk = jnp.where((rid >= h * BLOCK) & (rid < (h + 1) * BLOCK), sink_ref[h], sk)
    for j in range(nblk):
        bias = band
        if j == 0:
            bias = jnp.where(first & (c < BLOCK), MASKED, bias)
        if j == nblk - 1:
            bias = jnp.where(last & (c >= 2 * BLOCK), MASKED, bias)
        kb = kbuf[j * BLOCK:(j + 3) * BLOCK, :]
        vb = vbuf[j * BLOCK:(j + 3) * BLOCK, :]
        ql = jnp.concatenate([q_ref[j * BLOCK:(j + 1) * BLOCK, h * LANES:(h + 1) * LANES] for h in range(H)],
                             axis=0)
        s = lax.dot_general(ql, kb, (((1,), (1,)), ((), ())), preferred_element_type=F32)
        s = (s.reshape(H, BLOCK, 3 * BLOCK) + bias[None]).reshape(H * BLOCK, 3 * BLOCK)
        m = jnp.maximum(jnp.max(s, axis=-1, keepdims=True), sk)
        e = jnp.exp((s - m).astype(BF16))
        out = jnp.dot(e, vb, preferred_element_type=F32)
        res = out[:, :LANES] / (out[:, LANES:] + jnp.exp(sk - m))
        slabs = []
        for p in range(H // 2):
            ra = res[(2 * p) * BLOCK:(2 * p + 1) * BLOCK]
            rb = res[(2 * p + 1) * BLOCK:(2 * p + 2) * BLOCK]
            if (2 * p) // G == 0:
                slabs.append(jnp.where(low, ra, pltpu.roll(rb, HEAD_DIM, axis=1)))
            else:
                slabs.append(jnp.where(low, pltpu.roll(ra, HEAD_DIM, axis=1), rb))
        o = jnp.concatenate(slabs, axis=1)
        o_ref[j * BLOCK:(j + 1) * BLOCK, :] = _rms(o, g_ref[...]).astype(BF16)


def _attn(sink, q, k, v, g, *, seq_len, tq):
    T = q.shape[0]
    tiles_per_seq = seq_len // tq
    rb = tq // BLOCK
    nbb = T // BLOCK
    main = lambda i, s: (i, 0)
    prev = lambda i, s: (jnp.maximum(i * rb - 1, 0), 0)
    nxt = lambda i, s: (jnp.minimum((i + 1) * rb, nbb - 1), 0)
    kv_trio = [pl.BlockSpec((tq, D_KV), main), pl.BlockSpec((BLOCK, D_KV), prev), pl.BlockSpec((BLOCK, D_KV), nxt)]
    grid_spec = pltpu.PrefetchScalarGridSpec(
        num_scalar_prefetch=1, grid=(T // tq,),
        in_specs=[pl.BlockSpec((tq, Q_SLABS), main)] + kv_trio + kv_trio + [
            pl.BlockSpec((1, D_ATTN), lambda i, s: (0, 0))],
        out_specs=pl.BlockSpec((tq, D_ATTN), main),
        scratch_shapes=[pltpu.VMEM((tq + 2 * BLOCK, D_KV), BF16), pltpu.VMEM((tq + 2 * BLOCK, D_KV + LANES), BF16)],
    )
    return pl.pallas_call(
        functools.partial(_attn_kernel, tq=tq, tiles_per_seq=tiles_per_seq),
        grid_spec=grid_spec, out_shape=jax.ShapeDtypeStruct((T, D_ATTN), BF16),
        compiler_params=_cparams(1), name="attn",
    )(sink, q, k, k, k, v, v, v, g)


def _fft_sizes(seq_len):
    n1h = seq_len // FFT_N2
    k1p = -(-(n1h + 1) // FFT_KB) * FFT_KB
    return n1h, k1p


@functools.lru_cache(maxsize=None)
def _dft_consts(seq_len):
    n1h, k1p = _fft_sizes(seq_len)
    n1 = 2 * n1h
    k = np.arange(k1p)[:, None]
    valid = (k <= n1h)
    th = 2.0 * np.pi * ((k * np.arange(n1)[None, :]) % n1) / n1
    w1 = np.concatenate([np.cos(th) * valid, -np.sin(th) * valid], axis=0)
    scale = np.where((k == 0) | (k == n1h), 1.0, 2.0) * valid / n1
    thh = th[:, :n1h]
    winv = np.concatenate([np.cos(thh) * scale, -np.sin(thh) * scale], axis=0).T
    a = 2.0 * np.pi * ((np.arange(FFT_N2)[:, None] * np.arange(FFT_N2)[None, :]) % FFT_N2) / FFT_N2
    cm, sm = np.cos(a), np.sin(a)
    g = np.block([[cm, sm], [-sm, cm]])
    gi = np.block([[cm, -sm], [sm, cm]]) / FFT_N2
    gs = np.concatenate([g[:, FFT_N2:], g[:, :FFT_N2]], axis=1)
    gis = np.concatenate([gi[FFT_N2:], gi[:FFT_N2]], axis=0)
    return w1, winv, g, gs, gi, gis


def _twiddle_tables(seq_len):
    n1h, k1p = _fft_sizes(seq_len)
    n = 2 * seq_len
    k = jnp.arange(k1p, dtype=jnp.int32)[:, None]
    m = jnp.arange(FFT_N2, dtype=jnp.int32)[None, :]
    ph = ((k * m) % n).astype(F32) * (2.0 * math.pi / n)
    cs, sn = jnp.cos(ph), jnp.sin(ph)
    fc = jnp.concatenate([cs, cs], axis=1)
    fs = jnp.concatenate([-sn, sn], axis=1)
    tc = jnp.broadcast_to(cs[:, :, None], (k1p, FFT_N2, LANES))
    ts = jnp.broadcast_to(sn[:, :, None], (k1p, FFT_N2, LANES))
    return fc, fs, tc, ts


def _fft1_kernel(w_ref, x_ref, o_ref, *, k1p):
    a = jnp.einsum('kn,nmc->kmc', w_ref[...], x_ref[...], preferred_element_type=F32)
    o_ref[0] = a[:k1p].astype(BF16)
    o_ref[1] = a[k1p:].astype(BF16)


def _fft1(w1, x, *, k1p):
    B, n1in, _, C = x.shape
    grid = (FFT_N2 // FFT_MG, B, C // LANES)
    return pl.pallas_call(
        functools.partial(_fft1_kernel, k1p=k1p),
        grid=grid,
        in_specs=[
            _resident((2 * k1p, n1in)),
            pl.BlockSpec((None, n1in, FFT_MG, LANES), lambda m, b, j: (b, 0, m, j)),
        ],
        out_specs=pl.BlockSpec((None, 2, k1p, FFT_MG, LANES), lambda m, b, j: (b, 0, 0, m, j)),
        out_shape=jax.ShapeDtypeStruct((B, 2, k1p, FFT_N2, C), BF16),
        compiler_params=_cparams(3), name="fft1",
    )(w1, x)


def _pair_diag(a, b):
    z = jnp.zeros_like(a)
    return jnp.concatenate([jnp.concatenate([a, z], axis=1), jnp.concatenate([z, b], axis=1)], axis=0).astype(BF16)


def _fwd_pair(g_ref, gs_ref, fc_ref, fs_ref, a_ref, j):
    def mat(jj):
        return g_ref[...] * fc_ref[jj:jj + 1, :] + gs_ref[...] * fs_ref[jj:jj + 1, :]
    x = jnp.concatenate([a_ref[0, j], a_ref[1, j], a_ref[0, j + 1], a_ref[1, j + 1]], axis=0)
    return jnp.dot(_pair_diag(mat(j), mat(j + 1)), x, preferred_element_type=F32)


def _fspec_kernel(g_ref, gs_ref, fc_ref, fs_ref, a_ref, o_ref):
    n = FFT_N2
    for j in range(0, FFT_KB, 2):
        bf = _fwd_pair(g_ref, gs_ref, fc_ref, fs_ref, a_ref.at[0], j)
        bb = _fwd_pair(g_ref, gs_ref, fc_ref, fs_ref, a_ref.at[1], j)
        for t in range(2):
            re = slice(2 * t * n, (2 * t + 1) * n)
            im = slice((2 * t + 1) * n, (2 * t + 2) * n)
            o_ref[0, j + t] = (bf[re] + bb[re]).astype(BF16)
            o_ref[1, j + t] = (bf[im] - bb[im]).astype(BF16)


def _fspec(g, gs, fc, fs, a):
    _, _, k1p, _, C = a.shape
    row = pl.BlockSpec((FFT_KB, 2 * FFT_N2), lambda kb: (kb, 0))
    return pl.pallas_call(
        _fspec_kernel, grid=(k1p // FFT_KB,),
        in_specs=[_resident((2 * FFT_N2, 2 * FFT_N2)), _resident((2 * FFT_N2, 2 * FFT_N2)), row, row,
                  pl.BlockSpec((2, 2, FFT_KB, FFT_N2, C), lambda kb: (0, 0, kb, 0, 0))],
        out_specs=pl.BlockSpec((2, FFT_KB, FFT_N2, C), lambda kb: (0, kb, 0, 0)),
        out_shape=jax.ShapeDtypeStruct(a.shape[1:], BF16),
        compiler_params=_cparams(1), name="fspec",
    )(g, gs, fc, fs, a)


def _fft2_kernel(g_ref, gs_ref, gi_ref, gis_ref, fc_ref, fs_ref, tc_ref, ts_ref, a_ref, h_ref, o_ref):
    n = FFT_N2

    def inv(jj):
        tc, ts = tc_ref[jj], ts_ref[jj]
        return (gi_ref[...] * jnp.concatenate([tc, tc], axis=0)
                + gis_ref[...] * jnp.concatenate([-ts, ts], axis=0))

    for j in range(0, FFT_KB, 2):
        b = _fwd_pair(g_ref, gs_ref, fc_ref, fs_ref, a_ref, j)
        ys = []
        for t in range(2):
            br, bi = b[2 * t * n:(2 * t + 1) * n], b[(2 * t + 1) * n:(2 * t + 2) * n]
            hr, hi = h_ref[0, j + t].astype(F32), h_ref[1, j + t].astype(F32)
            ys += [br * hr - bi * hi, br * hi + bi * hr]
        y = jnp.concatenate(ys, axis=0).astype(BF16)
        p = jnp.dot(_pair_diag(inv(j), inv(j + 1)), y, preferred_element_type=F32)
        for t in range(2):
            o_ref[0, j + t] = p[2 * t * n:(2 * t + 1) * n].astype(BF16)
            o_ref[1, j + t] = p[(2 * t + 1) * n:(2 * t + 2) * n].astype(BF16)


def _fft2(g, gs, gi, gis, fc, fs, tc, ts, a, hspec):
    B, _, k1p, _, C = a.shape
    mat = _resident((2 * FFT_N2, 2 * FFT_N2))
    row = pl.BlockSpec((FFT_KB, 2 * FFT_N2), lambda kb, b: (kb, 0))
    col = pl.BlockSpec((FFT_KB, FFT_N2, LANES), lambda kb, b: (kb, 0, 0))
    return pl.pallas_call(
        _fft2_kernel, grid=(k1p // FFT_KB, B),
        in_specs=[
            mat, mat, mat, mat, row, row, col, col,
            pl.BlockSpec((None, 2, FFT_KB, FFT_N2, C), lambda kb, b: (b, 0, kb, 0, 0)),
            pl.BlockSpec((2, FFT_KB, FFT_N2, C), lambda kb, b: (0, kb, 0, 0)),
        ],
        out_specs=pl.BlockSpec((None, 2, FFT_KB, FFT_N2, C), lambda kb, b: (b, 0, kb, 0, 0)),
        out_shape=jax.ShapeDtypeStruct(a.shape, BF16),
        compiler_params=_cparams(2), name="fft2",
    )(g, gs, gi, gis, fc, fs, tc, ts, a, hspec)


def _ifft1_kernel(w_ref, a_ref, o_ref):
    a = jnp.concatenate([a_ref[0], a_ref[1]], axis=0)
    o_ref[...] = jnp.einsum('nk,kmc->nmc', w_ref[...], a, preferred_element_type=F32).astype(BF16)


def _ifft1(winv, a):
    B, _, k1p, _, C = a.shape
    n1h = winv.shape[0]
    return pl.pallas_call(
        _ifft1_kernel, grid=(FFT_N2 // FFT_MG, B, C // LANES),
        in_specs=[
            _resident((n1h, 2 * k1p)),
            pl.BlockSpec((None, 2, k1p, FFT_MG, LANES), lambda m, b, j: (b, 0, 0, m, j)),
        ],
        out_specs=pl.BlockSpec((None, n1h, FFT_MG, LANES), lambda m, b, j: (b, 0, m, j)),
        out_shape=jax.ShapeDtypeStruct((B, n1h, FFT_N2, C), BF16),
        compiler_params=_cparams(3), name="ifft1",
    )(winv, a)


def _split_dot(a, b):
    ah = a.astype(BF16)
    al = (a - ah.astype(F32)).astype(BF16)
    bh = b.astype(BF16)
    bl = (b - bh.astype(F32)).astype(BF16)
    d = functools.partial(jnp.dot, preferred_element_type=F32)
    return d(ah, bh) + d(ah, bl) + d(al, bh)


def _filt_kernel(ft_ref, w1a_ref, w1b_ref, b1_ref, q1_ref, w2_ref, b2_ref, q2_ref, w3a_ref, w3b_ref,
                 dl_ref, db_ref, o_ref, *, seq_len, tl):
    i = pl.program_id(0)
    L = seq_len
    hl = tl // 2
    pre = _split_dot(ft_ref[:hl, :], w1a_ref[...]) + _split_dot(ft_ref[hl:, :], w1b_ref[...])
    h = jnp.sin(q1_ref[...] * (pre + b1_ref[...]))
    h = jnp.sin(q2_ref[...] * (_split_dot(h, w2_ref[...]) + b2_ref[...]))
    for half, w3_ref in enumerate((w3a_ref, w3b_ref)):
        hh = _split_dot(h, w3_ref[...])
        n = i * tl + half * hl + lax.broadcasted_iota(jnp.int32, (hl, D_HYENA), 0)
        decay = jnp.exp(-(n.astype(F32) / (L - 1)) * dl_ref[...])
        rows = slice(half * hl, (half + 1) * hl)
        o_ref[0, rows, :] = (hh[:, :D_HYENA] * decay + jnp.where(n == 0, db_ref[...], 0.0)).astype(BF16)
        o_ref[1, rows, :] = jnp.where(n == 0, 0.0, hh[:, D_HYENA:] * decay).astype(BF16)


def _filt(feat, w1a, w1b, b1, q1, w2, b2, q2, w3a, w3b, dl, db, *, seq_len, tl):
    return pl.pallas_call(
        functools.partial(_filt_kernel, seq_len=seq_len, tl=tl),
        grid=(seq_len // tl,),
        in_specs=[
            pl.BlockSpec((tl, FEAT_W), lambda i: (i, 0)),
            _resident((FEAT_W, LANES)), _resident((FEAT_W, LANES)), _resident((1, LANES)), _resident((1, LANES)),
            _resident((LANES, LANES)), _resident((1, LANES)), _resident((1, LANES)),
            _resident((LANES, 2 * D_HYENA)), _resident((LANES, 2 * D_HYENA)),
            _resident((1, D_HYENA)), _resident((1, D_HYENA)),
        ],
        out_specs=pl.BlockSpec((2, tl, D_HYENA), lambda i: (0, i, 0)),
        out_shape=jax.ShapeDtypeStruct((2, seq_len, D_HYENA), BF16),
        compiler_params=_cparams(1), name="filt",
    )(feat, w1a, w1b, b1, q1, w2, b2, q2, w3a, w3b, dl, db)


FEAT_W = 40


@functools.lru_cache(maxsize=None)
def _filter_features(seq_len):
    L = seq_len
    bands = (FILTER_EMB - 1) // 2
    pos = np.arange(L, dtype=np.float64)[:, None]
    t = pos / (L - 1)
    w = 2.0 * np.pi * pos / L
    f = np.linspace(1e-4, bands - 1, bands)[None]
    pad = np.zeros((L, FEAT_W - FILTER_EMB))
    return np.concatenate([t, np.cos(f * w), -np.sin(f * w), pad], axis=1).astype(np.float32)


FF_CHUNK = 256


def _ffn_kernel(hm_ref, hp_ref, hn_ref, am_ref, ap_ref, an_ref, xm_ref, xp_ref, xn_ref,
                ym_ref, yp_ref, yn_ref, p_ref, gh_ref, wo_ref, gf_ref, wu_ref, wc_ref, bc_ref,
                wd_ref, wg_ref, wp_ref, o_ref, act_ref, *, tm, tiles_per_seq):
    i = pl.program_id(0)
    first = (i % tiles_per_seq) == 0
    last = (i % tiles_per_seq) == tiles_per_seq - 1

    def stack(m_ref, p_ref, n_ref):
        p = p_ref[BF16_ROWS - SUBLANES:, :]
        n = n_ref[:SUBLANES, :]
        p = jnp.where(first, jnp.zeros_like(p), p)
        n = jnp.where(last, jnp.zeros_like(n), n)
        return jnp.concatenate([m_ref[...], n, p], axis=0)

    h = stack(hm_ref, hp_ref, hn_ref)
    a = stack(am_ref, ap_ref, an_ref)
    hy = stack(xm_ref, xp_ref, xn_ref).astype(F32) * stack(ym_ref, yp_ref, yn_ref).astype(F32)
    hyn = _rms(hy, gh_ref[...]).astype(BF16)
    h1 = (h + jnp.dot(a, wo_ref[:D_ATTN, :], preferred_element_type=F32)
          + jnp.dot(hyn, wo_ref[D_ATTN:, :], preferred_element_type=F32))
    n2 = _rms(h1, gf_ref[...]).astype(BF16)

    def conv(c0):
        u = jnp.dot(n2, wu_ref[:, c0:c0 + FF_CHUNK], preferred_element_type=F32)
        return _conv3(u, tm, wc_ref, bc_ref, c0, c0 + FF_CHUNK)

    for cidx in range(D_FF // FF_CHUNK):
        c0 = cidx * FF_CHUNK
        ua = conv(c0)
        ug = conv(D_FF + c0)
        act_ref[:, c0:c0 + FF_CHUNK] = (ug * jax.nn.sigmoid(ug) * ua).astype(BF16)

    h2 = h1[:tm] + jnp.dot(act_ref[...], wd_ref[...], preferred_element_type=F32)
    gate = jax.nn.sigmoid(jnp.dot(h2.astype(BF16), wg_ref[...], preferred_element_type=F32))
    pp = jnp.dot(p_ref[...].astype(BF16), wp_ref[...], preferred_element_type=F32)
    o_ref[...] = h2 + gate * pp


def _ffn(h, attn, x0, y, p, gh, wo, gf, wu, wc, bc, wd, wg, wp, *, seq_len, tm, layer):
    p_tile0 = layer * (h.shape[0] // tm)
    T = h.shape[0]
    tiles_per_seq = seq_len // tm
    r16 = tm // BF16_ROWS
    nb16 = T // BF16_ROWS
    main = lambda i: (i, 0)
    prev = lambda i: (jnp.maximum(i * r16 - 1, 0), 0)
    nxt = lambda i: (jnp.minimum((i + 1) * r16, nb16 - 1), 0)

    def trio(width):
        return [pl.BlockSpec((tm, width), main), pl.BlockSpec((BF16_ROWS, width), prev),
                pl.BlockSpec((BF16_ROWS, width), nxt)]

    in_specs = (trio(D_MODEL) + trio(D_ATTN) + trio(D_HYENA) + trio(D_HYENA) + [
        pl.BlockSpec((tm, D_PLE), lambda i: (p_tile0 + i, 0)),
        _resident((1, D_HYENA)),
        _resident((D_MODEL, D_MODEL), layer),
        _resident((1, D_MODEL)),
        _resident((D_MODEL, 2 * D_FF), layer),
        _resident((3, 2 * D_FF)),
        _resident((1, 2 * D_FF)),
        _resident((D_FF, D_MODEL), layer),
        _resident((D_MODEL, D_MODEL), layer),
        _resident((D_PLE, D_MODEL), layer),
    ])
    return pl.pallas_call(
        functools.partial(_ffn_kernel, tm=tm, tiles_per_seq=tiles_per_seq),
        grid=(T // tm,), in_specs=in_specs,
        out_specs=pl.BlockSpec((tm, D_MODEL), main),
        out_shape=jax.ShapeDtypeStruct((T, D_MODEL), F32),
        scratch_shapes=[pltpu.VMEM((tm, D_FF), BF16)],
        compiler_params=_cparams(1), name="ffn",
    )(h, h, h, attn, attn, attn, x0, x0, x0, y, y, y, p, gh, wo, gf, wu, wc, bc, wd, wg, wp)


def _rope_slabs(seq_len):
    half = ROPE_DIM // 2
    inv = ROPE_THETA ** (-jnp.arange(0, ROPE_DIM, 2, dtype=F32) / ROPE_DIM)
    ang = jnp.arange(seq_len, dtype=F32)[:, None] * inv[None]
    cos, sin = jnp.cos(ang), jnp.sin(ang)
    ones = jnp.ones((seq_len, HEAD_DIM - ROPE_DIM), F32)
    zeros = jnp.zeros((seq_len, HEAD_DIM - ROPE_DIM), F32)
    zh = jnp.zeros((seq_len, half), F32)
    rc = jnp.concatenate([cos, cos, ones], axis=1)
    rs1 = jnp.concatenate([-sin, zh, zeros], axis=1)
    rs2 = jnp.concatenate([zh, sin, zeros], axis=1)
    rep = LANES // HEAD_DIM
    return tuple(jnp.tile(a, (1, rep)) for a in (rc, rs1, rs2))


def _group_consts(seq_len):
    n1h, k1p = _fft_sizes(seq_len)
    w1, winv, g, gs, gi, gis = _dft_consts(seq_len)
    fc, fs, tc, ts = _twiddle_tables(seq_len)
    return dict(
        seq_len=seq_len, n1h=n1h, k1p=k1p,
        w1_half=jnp.asarray(w1[:, :n1h], BF16),
        winv=jnp.asarray(winv, BF16),
        g=jnp.asarray(g, F32), gs=jnp.asarray(gs, F32), gi=jnp.asarray(gi, F32), gis=jnp.asarray(gis, F32),
        fc=fc, fs=fs, tc=tc, ts=ts, rope=_rope_slabs(seq_len), feat=_filter_features(seq_len),
    )


def _pair(v):
    return jnp.concatenate([v, v])[None]


def _layer_weights(i, rms_mix, w_in, q_norm, k_norm, sink, w_short, b_short, filt_w1, filt_b1, filt_freq1,
                   filt_w2, filt_b2, filt_freq2, filt_w3, hyena_bias, norm_attn_out, norm_hyena_out,
                   w_out, rms_ffn, w_up, w_ffconv, b_ffconv, w_down, w_ple_gate, w_ple_proj):
    rep_q = D_ATTN // HEAD_DIM
    rep_k = D_KV // HEAD_DIM
    hid = FILTER_HID
    w1p = jnp.zeros((FEAT_W, hid), F32).at[:FILTER_EMB].set(filt_w1[i])
    zpad = jnp.zeros((FEAT_W, hid), F32)
    zsq = jnp.zeros((hid, hid), F32)
    z3 = jnp.zeros((hid, 2 * D_HYENA), F32)
    deltas = jnp.abs(jnp.linspace(MIN_DECAY, MAX_DECAY, D_HYENA, dtype=F32))
    return dict(
        g_mix=rms_mix[i][None], w_in=w_in,
        qg=(jnp.tile(q_norm[i], rep_q) * (HEAD_DIM ** -0.5))[None], kg=jnp.tile(k_norm[i], rep_k)[None],
        sink=sink[i], ws=w_short[i], bs=b_short[i][None],
        fw1a=jnp.concatenate([w1p, zpad], axis=1), fw1b=jnp.concatenate([zpad, w1p], axis=1),
        fb1=_pair(filt_b1[i]), fq1=_pair(filt_freq1[i]),
        fw2=jnp.block([[filt_w2[i], zsq], [zsq, filt_w2[i]]]),
        fb2=_pair(filt_b2[i]), fq2=_pair(filt_freq2[i]),
        fw3a=jnp.concatenate([filt_w3[i], z3], axis=0), fw3b=jnp.concatenate([z3, filt_w3[i]], axis=0),
        deltas=deltas[None], dbias=hyena_bias[i][None],
        g_attn=norm_attn_out[i][None], g_hy=norm_hyena_out[i][None],
        w_out=w_out, g_ffn=rms_ffn[i][None], w_up=w_up,
        wc=w_ffconv[i], bc=b_ffconv[i][None], w_down=w_down,
        w_gate=w_ple_gate, w_proj=w_ple_proj,
    )


def _layer(h, p_all, layer, lw, gc, bd, batch):
    L = gc["seq_len"]
    T = batch * L
    n1h, k1p = gc["n1h"], gc["k1p"]
    rc, rs1, rs2 = gc["rope"]
    tm = min(TILE_FFN, L)
    q, k, v, x0, vv = _inproj(h, lw["g_mix"], lw["w_in"], lw["qg"], lw["kg"], bd, rc, rs1, rs2,
                                lw["ws"], lw["bs"], seq_len=L, tm=min(TILE_INPROJ, L), layer=layer)
    attn = _attn(lw["sink"], q, k, v, lw["g_attn"], seq_len=L, tq=min(TILE_ATTN, L))
    filt = _filt(gc["feat"], lw["fw1a"], lw["fw1b"], lw["fb1"], lw["fq1"], lw["fw2"], lw["fb2"], lw["fq2"],
                 lw["fw3a"], lw["fw3b"], lw["deltas"], lw["dbias"], seq_len=L, tl=min(TILE_FILT, L))
    fa = _fft1(gc["w1_half"], filt.reshape(2, n1h, FFT_N2, D_HYENA), k1p=k1p)
    hspec = _fspec(gc["g"], gc["gs"], gc["fc"], gc["fs"], fa)
    a = _fft1(gc["w1_half"], vv.reshape(batch, n1h, FFT_N2, D_HYENA), k1p=k1p)
    a = _fft2(gc["g"], gc["gs"], gc["gi"], gc["gis"], gc["fc"], gc["fs"], gc["tc"], gc["ts"], a, hspec)
    y = _ifft1(gc["winv"], a).reshape(T, D_HYENA)
    return _ffn(h, attn, x0, y, p_all, lw["g_hy"], lw["w_out"], lw["g_ffn"], lw["w_up"], lw["wc"], lw["bc"],
                lw["w_down"], lw["w_gate"], lw["w_proj"], seq_len=L, tm=tm, layer=layer)


def _trunk(x, p, weights):
    batch, L, _ = x.shape
    gc = _group_consts(L)
    bd = jnp.asarray(np.kron(np.eye(MXU_W // HEAD_DIM), np.full((HEAD_DIM, HEAD_DIM), 1.0 / HEAD_DIM)), BF16)
    h = x.reshape(batch * L, D_MODEL)
    p_all = p.reshape(DEPTH * batch * L, D_PLE)
    for i in range(DEPTH):
        lw = _layer_weights(i, *weights)
        h = _layer(h, p_all, i, lw, gc, bd, batch)
    return h.reshape(batch, L, D_MODEL)


def kernel(x_prompt, x_sample, p_prompt, p_sample, rms_mix, w_in, q_norm, k_norm, sink, w_short, b_short, filt_w1, filt_b1, filt_freq1, filt_w2, filt_b2, filt_freq2, filt_w3, hyena_bias, norm_attn_out, norm_hyena_out, w_out, rms_ffn, w_up, w_ffconv, b_ffconv, w_down, w_ple_gate, w_ple_proj):
    w_in, w_out, w_up, w_down, w_ple_gate, w_ple_proj = (
        w.astype(BF16) for w in (w_in, w_out, w_up, w_down, w_ple_gate, w_ple_proj))
    weights = (rms_mix, w_in, q_norm, k_norm, sink, w_short, b_short, filt_w1, filt_b1, filt_freq1,
               filt_w2, filt_b2, filt_freq2, filt_w3, hyena_bias, norm_attn_out, norm_hyena_out,
               w_out, rms_ffn, w_up, w_ffconv, b_ffconv, w_down, w_ple_gate, w_ple_proj)
    y_prompt = _trunk(x_prompt, p_prompt, weights)
    y_sample = _trunk(x_sample, p_sample, weights)
    return (y_prompt, y_sample)
```

```python
import functools
import math

import numpy as np
import jax
import jax.numpy as jnp
from jax import lax
from jax.experimental import pallas as pl
from jax.experimental.pallas import tpu as pltpu

F32 = jnp.float32
BF16 = jnp.bfloat16

D_MODEL = 1024
DEPTH = 4
N_Q_HEADS = 8
N_KV_HEADS = 2
HEAD_DIM = 64
D_ATTN = N_Q_HEADS * HEAD_DIM
D_KV = N_KV_HEADS * HEAD_DIM
D_QKV = D_ATTN + 2 * D_KV
WINDOW = 128
BLOCK = 128
ROPE_THETA = 500000.0
ROPE_DIM = HEAD_DIM // 4
D_HYENA = 512
FILTER_EMB = 33
FILTER_HID = 64
FAST_DECAY_PCT = 0.3
SLOW_DECAY_PCT = 1.5
DECAY_TARGET = 1e-2
MAX_DECAY = math.log(DECAY_TARGET) / FAST_DECAY_PCT
MIN_DECAY = math.log(DECAY_TARGET) / SLOW_DECAY_PCT
D_IN = D_QKV + 3 * D_HYENA
D_FF = 2816
D_PLE = 256
EPS = 1e-6
MASKED = -1e30

LANES = 128
SUBLANES = 8
BF16_ROWS = 16
MXU_W = 256
Q_SLABS = N_Q_HEADS * LANES
COL_HYENA = 0
COL_Q = 3 * D_HYENA
COL_K = COL_Q + D_ATTN
COL_V = COL_K + D_KV
FFT_N2 = 64
FFT_MG = 32
FFT_KB = 24
TILE_INPROJ = 1024
TILE_ATTN = 2048
TILE_FFN = 512
TILE_FILT = 2048
VMEM_LIMIT = 56 * 1024 * 1024


def _cparams(n_axes):
    return pltpu.CompilerParams(dimension_semantics=("arbitrary",) * n_axes,
                                vmem_limit_bytes=VMEM_LIMIT)


def _resident(shape, layer=None):
    if layer is None:
        return pl.BlockSpec(shape, lambda *_: (0,) * len(shape), pipeline_mode=pl.Buffered(1))
    return pl.BlockSpec((None,) + tuple(shape), lambda *_: (layer,) + (0,) * len(shape),
                        pipeline_mode=pl.Buffered(1))


def _rms(x, g):
    ms = jnp.mean(x * x, axis=-1, keepdims=True)
    return x * lax.rsqrt(ms + EPS) * g


def _conv3(u, tm, w_ref, b_ref, c0, c1):
    rows = u.shape[0]
    up = pltpu.roll(u, 1, axis=0)[:tm]
    un = pltpu.roll(u, rows - 1, axis=0)[:tm]
    return up * w_ref[0:1, c0:c1] + u[:tm] * w_ref[1:2, c0:c1] + un * w_ref[2:3, c0:c1] + b_ref[:, c0:c1]


def _inproj_kernel(xm_ref, xp_ref, xn_ref, g_ref, w_ref, qg_ref, kg_ref, bd_ref,
                   rc_ref, rs1_ref, rs2_ref, ws_ref, bs_ref,
                   q_ref, k_ref, v_ref, x0_ref, vv_ref, *, tm, tiles_per_seq):
    i = pl.program_id(0)
    first = (i % tiles_per_seq) == 0
    last = (i % tiles_per_seq) == tiles_per_seq - 1
    xp = jnp.where(first, 0.0, xp_ref[...])
    xn = jnp.where(last, 0.0, xn_ref[...])
    x = jnp.concatenate([xm_ref[...], xn, xp], axis=0)
    n = _rms(x, g_ref[...]).astype(BF16)
    z = jnp.dot(n, w_ref[...], preferred_element_type=F32)

    bd = bd_ref[...]
    rc, rs1, rs2 = rc_ref[...], rs1_ref[...], rs2_ref[...]

    def norm_rope(zs, gain, bds):
        sq = (zs * zs).astype(BF16)
        step = bds.shape[0]
        ms = jnp.concatenate([jnp.dot(sq[:, c:c + step], bds, preferred_element_type=F32)
                              for c in range(0, zs.shape[1], step)], axis=1)
        y = zs * lax.rsqrt(ms + EPS) * gain
        outs = []
        for s in range(y.shape[1] // LANES):
            ys = y[:, s * LANES:(s + 1) * LANES]
            outs.append(ys * rc + pltpu.roll(ys, LANES - ROPE_DIM // 2, axis=1) * rs1
                        + pltpu.roll(ys, ROPE_DIM // 2, axis=1) * rs2)
        return outs

    u = _conv3(z[:, COL_HYENA:COL_HYENA + 3 * D_HYENA], tm, ws_ref, bs_ref, 0, 3 * D_HYENA)
    x0_ref[...] = u[:, :D_HYENA].astype(BF16)
    vv_ref[...] = (u[:, D_HYENA:2 * D_HYENA] * u[:, 2 * D_HYENA:]).astype(BF16)

    qs = norm_rope(z[:tm, COL_Q:COL_Q + D_ATTN], qg_ref[...], bd)
    low = lax.broadcasted_iota(jnp.int32, (tm, LANES), 1) < HEAD_DIM
    heads_per_kv = N_Q_HEADS // N_KV_HEADS
    for s, qv in enumerate(qs):
        qr = pltpu.roll(qv, HEAD_DIM, axis=1)
        for half in range(2):
            h = 2 * s + half
            src = qv if half == h // heads_per_kv else qr
            keep = low if h // heads_per_kv == 0 else jnp.logical_not(low)
            q_ref[:, h * LANES:(h + 1) * LANES] = jnp.where(keep, src, 0.0).astype(BF16)
    k_ref[...] = norm_rope(z[:tm, COL_K:COL_K + D_KV], kg_ref[...], bd[:D_KV, :D_KV])[0].astype(BF16)
    v_ref[...] = z[:tm, COL_V:COL_V + D_KV].astype(BF16)


def _inproj(x, g, w, qg, kg, bd, rc, rs1, rs2, ws, bs, *, seq_len, tm, layer):
    T = x.shape[0]
    tiles_per_seq = seq_len // tm
    r8 = tm // SUBLANES
    nb8 = T // SUBLANES
    rope_spec = pl.BlockSpec((tm, LANES), lambda i: (i % tiles_per_seq, 0))
    in_specs = [
        pl.BlockSpec((tm, D_MODEL), lambda i: (i, 0)),
        pl.BlockSpec((SUBLANES, D_MODEL), lambda i: (jnp.maximum(i * r8 - 1, 0), 0)),
        pl.BlockSpec((SUBLANES, D_MODEL), lambda i: (jnp.minimum((i + 1) * r8, nb8 - 1), 0)),
        _resident((1, D_MODEL)),
        _resident((D_MODEL, D_IN), layer),
        _resident((1, D_ATTN)),
        _resident((1, D_KV)),
        _resident((MXU_W, MXU_W)),
        rope_spec, rope_spec, rope_spec,
        _resident((3, 3 * D_HYENA)),
        _resident((1, 3 * D_HYENA)),
    ]
    widths = (Q_SLABS, D_KV, D_KV, D_HYENA, D_HYENA)
    out_specs = [pl.BlockSpec((tm, wd), lambda i: (i, 0)) for wd in widths]
    out_shape = [jax.ShapeDtypeStruct((T, wd), BF16) for wd in widths]
    return pl.pallas_call(
        functools.partial(_inproj_kernel, tm=tm, tiles_per_seq=tiles_per_seq),
        grid=(T // tm,), in_specs=in_specs, out_specs=out_specs, out_shape=out_shape,
        compiler_params=_cparams(1), name="inproj",
    )(x, x, x, g, w, qg, kg, bd, rc, rs1, rs2, ws, bs)


def _attn_kernel(sink_ref, q_ref, km_ref, kp_ref, kn_ref, vm_ref, vp_ref, vn_ref, g_ref,
                 o_ref, kbuf, vbuf, *, tq, tiles_per_seq):
    i = pl.program_id(0)
    first = (i % tiles_per_seq) == 0
    last = (i % tiles_per_seq) == tiles_per_seq - 1
    nblk = tq // BLOCK
    H = N_Q_HEADS
    G = N_Q_HEADS // N_KV_HEADS
    kbuf[0:BLOCK] = kp_ref[...]
    kbuf[BLOCK:BLOCK + tq] = km_ref[...]
    kbuf[BLOCK + tq:] = kn_ref[...]
    vbuf[0:BLOCK, :D_KV] = vp_ref[...]
    vbuf[BLOCK:BLOCK + tq, :D_KV] = vm_ref[...]
    vbuf[BLOCK + tq:, :D_KV] = vn_ref[...]
    vbuf[:, D_KV:] = jnp.ones((tq + 2 * BLOCK, LANES), BF16)
    r = lax.broadcasted_iota(jnp.int32, (BLOCK, 3 * BLOCK), 0)
    c = lax.broadcasted_iota(jnp.int32, (BLOCK, 3 * BLOCK), 1)
    d = c - BLOCK - r
    band = jnp.where((d >= -WINDOW) & (d <= WINDOW), 0.0, MASKED)
    low = lax.broadcasted_iota(jnp.int32, (BLOCK, LANES), 1) < HEAD_DIM
    rid = lax.broadcasted_iota(jnp.int32, (H * BLOCK, 1), 0)
    sk = jnp.zeros((H * BLOCK, 1), F32)
    for h in range(H):
        sk = jnp.where((rid >= h * BLOCK) & (rid < (h + 1) * BLOCK), sink_ref[h], sk)
    for j in range(nblk):
        bias = band
        if j == 0:
            bias = jnp.where(first & (c < BLOCK), MASKED, bias)
        if j == nblk - 1:
            bias = jnp.where(last & (c >= 2 * BLOCK), MASKED, bias)
        kb = kbuf[j * BLOCK:(j + 3) * BLOCK, :]
        vb = vbuf[j * BLOCK:(j + 3) * BLOCK, :]
        ql = jnp.concatenate([q_ref[j * BLOCK:(j + 1) * BLOCK, h * LANES:(h + 1) * LANES] for h in range(H)],
                             axis=0)
        s = lax.dot_general(ql, kb, (((1,), (1,)), ((), ())), preferred_element_type=F32)
        s = (s.reshape(H, BLOCK, 3 * BLOCK) + bias[None]).reshape(H * BLOCK, 3 * BLOCK)
        m = jnp.maximum(jnp.max(s, axis=-1, keepdims=True), sk)
        e = jnp.exp((s - m).astype(BF16))
        out = jnp.dot(e, vb, preferred_element_type=F32)
        res = out[:, :LANES] / (out[:, LANES:] + jnp.exp(sk - m))
        slabs = []
        for p in range(H // 2):
            ra = res[(2 * p) * BLOCK:(2 * p + 1) * BLOCK]
            rb = res[(2 * p + 1) * BLOCK:(2 * p + 2) * BLOCK]
            if (2 * p) // G == 0:
                slabs.append(jnp.where(low, ra, pltpu.roll(rb, HEAD_DIM, axis=1)))
            else:
                slabs.append(jnp.where(low, pltpu.roll(ra, HEAD_DIM, axis=1), rb))
        o = jnp.concatenate(slabs, axis=1)
        o_ref[j * BLOCK:(j + 1) * BLOCK, :] = _rms(o, g_ref[...]).astype(BF16)


def _attn(sink, q, k, v, g, *, seq_len, tq):
    T = q.shape[0]
    tiles_per_seq = seq_len // tq
    rb = tq // BLOCK
    nbb = T // BLOCK
    main = lambda i, s: (i, 0)
    prev = lambda i, s: (jnp.maximum(i * rb - 1, 0), 0)
    nxt = lambda i, s: (jnp.minimum((i + 1) * rb, nbb - 1), 0)
    kv_trio = [pl.BlockSpec((tq, D_KV), main), pl.BlockSpec((BLOCK, D_KV), prev), pl.BlockSpec((BLOCK, D_KV), nxt)]
    grid_spec = pltpu.PrefetchScalarGridSpec(
        num_scalar_prefetch=1, grid=(T // tq,),
        in_specs=[pl.BlockSpec((tq, Q_SLABS), main)] + kv_trio + kv_trio + [
            pl.BlockSpec((1, D_ATTN), lambda i, s: (0, 0))],
        out_specs=pl.BlockSpec((tq, D_ATTN), main),
        scratch_shapes=[pltpu.VMEM((tq + 2 * BLOCK, D_KV), BF16), pltpu.VMEM((tq + 2 * BLOCK, D_KV + LANES), BF16)],
    )
    return pl.pallas_call(
        functools.partial(_attn_kernel, tq=tq, tiles_per_seq=tiles_per_seq),
        grid_spec=grid_spec, out_shape=jax.ShapeDtypeStruct((T, D_ATTN), BF16),
        compiler_params=_cparams(1), name="attn",
    )(sink, q, k, k, k, v, v, v, g)


def _fft_sizes(seq_len):
    n1h = seq_len // FFT_N2
    k1p = -(-(n1h + 1) // FFT_KB) * FFT_KB
    return n1h, k1p


@functools.lru_cache(maxsize=None)
def _dft_consts(seq_len):
    n1h, k1p = _fft_sizes(seq_len)
    n1 = 2 * n1h
    k = np.arange(k1p)[:, None]
    valid = (k <= n1h)
    th = 2.0 * np.pi * ((k * np.arange(n1)[None, :]) % n1) / n1
    w1 = np.concatenate([np.cos(th) * valid, -np.sin(th) * valid], axis=0)
    scale = np.where((k == 0) | (k == n1h), 1.0, 2.0) * valid / n1
    thh = th[:, :n1h]
    winv = np.concatenate([np.cos(thh) * scale, -np.sin(thh) * scale], axis=0).T
    a = 2.0 * np.pi * ((np.arange(FFT_N2)[:, None] * np.arange(FFT_N2)[None, :]) % FFT_N2) / FFT_N2
    cm, sm = np.cos(a), np.sin(a)
    g = np.block([[cm, sm], [-sm, cm]])
    gi = np.block([[cm, -sm], [sm, cm]]) / FFT_N2
    gs = np.concatenate([g[:, FFT_N2:], g[:, :FFT_N2]], axis=1)
    gis = np.concatenate([gi[FFT_N2:], gi[:FFT_N2]], axis=0)
    return w1, winv, g, gs, gi, gis


def _twiddle_tables(seq_len):
    n1h, k1p = _fft_sizes(seq_len)
    n = 2 * seq_len
    k = jnp.arange(k1p, dtype=jnp.int32)[:, None]
    m = jnp.arange(FFT_N2, dtype=jnp.int32)[None, :]
    ph = ((k * m) % n).astype(F32) * (2.0 * math.pi / n)
    cs, sn = jnp.cos(ph), jnp.sin(ph)
    fc = jnp.concatenate([cs, cs], axis=1)
    fs = jnp.concatenate([-sn, sn], axis=1)
    tc = jnp.broadcast_to(cs[:, :, None], (k1p, FFT_N2, LANES))
    ts = jnp.broadcast_to(sn[:, :, None], (k1p, FFT_N2, LANES))
    return fc, fs, tc, ts


def _fft1_kernel(w_ref, x_ref, o_ref, *, k1p):
    a = jnp.einsum('kn,nmc->kmc', w_ref[...], x_ref[...], preferred_element_type=F32)
    o_ref[0] = a[:k1p].astype(BF16)
    o_ref[1] = a[k1p:].astype(BF16)


def _fft1(w1, x, *, k1p):
    B, n1in, _, C = x.shape
    grid = (FFT_N2 // FFT_MG, B, C // LANES)
    return pl.pallas_call(
        functools.partial(_fft1_kernel, k1p=k1p),
        grid=grid,
        in_specs=[
            _resident((2 * k1p, n1in)),
            pl.BlockSpec((None, n1in, FFT_MG, LANES), lambda m, b, j: (b, 0, m, j)),
        ],
        out_specs=pl.BlockSpec((None, 2, k1p, FFT_MG, LANES), lambda m, b, j: (b, 0, 0, m, j)),
        out_shape=jax.ShapeDtypeStruct((B, 2, k1p, FFT_N2, C), BF16),
        compiler_params=_cparams(3), name="fft1",
    )(w1, x)


def _pair_diag(a, b):
    z = jnp.zeros_like(a)
    return jnp.concatenate([jnp.concatenate([a, z], axis=1), jnp.concatenate([z, b], axis=1)], axis=0).astype(BF16)


def _fwd_pair(g_ref, gs_ref, fc_ref, fs_ref, a_ref, j):
    def mat(jj):
        return g_ref[...] * fc_ref[jj:jj + 1, :] + gs_ref[...] * fs_ref[jj:jj + 1, :]
    x = jnp.concatenate([a_ref[0, j], a_ref[1, j], a_ref[0, j + 1], a_ref[1, j + 1]], axis=0)
    return jnp.dot(_pair_diag(mat(j), mat(j + 1)), x, preferred_element_type=F32)


def _fspec_kernel(g_ref, gs_ref, fc_ref, fs_ref, a_ref, o_ref):
    n = FFT_N2
    for j in range(0, FFT_KB, 2):
        bf = _fwd_pair(g_ref, gs_ref, fc_ref, fs_ref, a_ref.at[0], j)
        bb = _fwd_pair(g_ref, gs_ref, fc_ref, fs_ref, a_ref.at[1], j)
        for t in range(2):
            re = slice(2 * t * n, (2 * t + 1) * n)
            im = slice((2 * t + 1) * n, (2 * t + 2) * n)
            o_ref[0, j + t] = (bf[re] + bb[re]).astype(BF16)
            o_ref[1, j + t] = (bf[im] - bb[im]).astype(BF16)


def _fspec(g, gs, fc, fs, a):
    _, _, k1p, _, C = a.shape
    row = pl.BlockSpec((FFT_KB, 2 * FFT_N2), lambda kb: (kb, 0))
    return pl.pallas_call(
        _fspec_kernel, grid=(k1p // FFT_KB,),
        in_specs=[_resident((2 * FFT_N2, 2 * FFT_N2)), _resident((2 * FFT_N2, 2 * FFT_N2)), row, row,
                  pl.BlockSpec((2, 2, FFT_KB, FFT_N2, C), lambda kb: (0, 0, kb, 0, 0))],
        out_specs=pl.BlockSpec((2, FFT_KB, FFT_N2, C), lambda kb: (0, kb, 0, 0)),
        out_shape=jax.ShapeDtypeStruct(a.shape[1:], BF16),
        compiler_params=_cparams(1), name="fspec",
    )(g, gs, fc, fs, a)


def _fft2_kernel(g_ref, gs_ref, gi_ref, gis_ref, fc_ref, fs_ref, tc_ref, ts_ref, a_ref, h_ref, o_ref):
    n = FFT_N2

    def inv(jj):
        tc, ts = tc_ref[jj], ts_ref[jj]
        return (gi_ref[...] * jnp.concatenate([tc, tc], axis=0)
                + gis_ref[...] * jnp.concatenate([-ts, ts], axis=0))

    for j in range(0, FFT_KB, 2):
        b = _fwd_pair(g_ref, gs_ref, fc_ref, fs_ref, a_ref, j)
        ys = []
        for t in range(2):
            br, bi = b[2 * t * n:(2 * t + 1) * n], b[(2 * t + 1) * n:(2 * t + 2) * n]
            hr, hi = h_ref[0, j + t].astype(F32), h_ref[1, j + t].astype(F32)
            ys += [br * hr - bi * hi, br * hi + bi * hr]
        y = jnp.concatenate(ys, axis=0).astype(BF16)
        p = jnp.dot(_pair_diag(inv(j), inv(j + 1)), y, preferred_element_type=F32)
        for t in range(2):
            o_ref[0, j + t] = p[2 * t * n:(2 * t + 1) * n].astype(BF16)
            o_ref[1, j + t] = p[(2 * t + 1) * n:(2 * t + 2) * n].astype(BF16)


def _fft2(g, gs, gi, gis, fc, fs, tc, ts, a, hspec):
    B, _, k1p, _, C = a.shape
    mat = _resident((2 * FFT_N2, 2 * FFT_N2))
    row = pl.BlockSpec((FFT_KB, 2 * FFT_N2), lambda kb, b: (kb, 0))
    col = pl.BlockSpec((FFT_KB, FFT_N2, LANES), lambda kb, b: (kb, 0, 0))
    return pl.pallas_call(
        _fft2_kernel, grid=(k1p // FFT_KB, B),
        in_specs=[
            mat, mat, mat, mat, row, row, col, col,
            pl.BlockSpec((None, 2, FFT_KB, FFT_N2, C), lambda kb, b: (b, 0, kb, 0, 0)),
            pl.BlockSpec((2, FFT_KB, FFT_N2, C), lambda kb, b: (0, kb, 0, 0)),
        ],
        out_specs=pl.BlockSpec((None, 2, FFT_KB, FFT_N2, C), lambda kb, b: (b, 0, kb, 0, 0)),
        out_shape=jax.ShapeDtypeStruct(a.shape, BF16),
        compiler_params=_cparams(2), name="fft2",
    )(g, gs, gi, gis, fc, fs, tc, ts, a, hspec)


def _ifft1_kernel(w_ref, a_ref, o_ref):
    a = jnp.concatenate([a_ref[0], a_ref[1]], axis=0)
    o_ref[...] = jnp.einsum('nk,kmc->nmc', w_ref[...], a, preferred_element_type=F32).astype(BF16)


def _ifft1(winv, a):
    B, _, k1p, _, C = a.shape
    n1h = winv.shape[0]
    return pl.pallas_call(
        _ifft1_kernel, grid=(FFT_N2 // FFT_MG, B, C // LANES),
        in_specs=[
            _resident((n1h, 2 * k1p)),
            pl.BlockSpec((None, 2, k1p, FFT_MG, LANES), lambda m, b, j: (b, 0, 0, m, j)),
        ],
        out_specs=pl.BlockSpec((None, n1h, FFT_MG, LANES), lambda m, b, j: (b, 0, m, j)),
        out_shape=jax.ShapeDtypeStruct((B, n1h, FFT_N2, C), BF16),
        compiler_params=_cparams(3), name="ifft1",
    )(winv, a)


def _split_dot(a, b):
    ah = a.astype(BF16)
    al = (a - ah.astype(F32)).astype(BF16)
    bh = b.astype(BF16)
    bl = (b - bh.astype(F32)).astype(BF16)
    d = functools.partial(jnp.dot, preferred_element_type=F32)
    return d(ah, bh) + d(ah, bl) + d(al, bh)


def _filt_kernel(ft_ref, w1a_ref, w1b_ref, b1_ref, q1_ref, w2_ref, b2_ref, q2_ref, w3a_ref, w3b_ref,
                 dl_ref, db_ref, o_ref, *, seq_len, tl):
    i = pl.program_id(0)
    L = seq_len
    hl = tl // 2
    pre = _split_dot(ft_ref[:hl, :], w1a_ref[...]) + _split_dot(ft_ref[hl:, :], w1b_ref[...])
    h = jnp.sin(q1_ref[...] * (pre + b1_ref[...]))
    h = jnp.sin(q2_ref[...] * (_split_dot(h, w2_ref[...]) + b2_ref[...]))
    for half, w3_ref in enumerate((w3a_ref, w3b_ref)):
        hh = _split_dot(h, w3_ref[...])
        n = i * tl + half * hl + lax.broadcasted_iota(jnp.int32, (hl, D_HYENA), 0)
        decay = jnp.exp(-(n.astype(F32) / (L - 1)) * dl_ref[...])
        rows = slice(half * hl, (half + 1) * hl)
        o_ref[0, rows, :] = (hh[:, :D_HYENA] * decay + jnp.where(n == 0, db_ref[...], 0.0)).astype(BF16)
        o_ref[1, rows, :] = jnp.where(n == 0, 0.0, hh[:, D_HYENA:] * decay).astype(BF16)


def _filt(feat, w1a, w1b, b1, q1, w2, b2, q2, w3a, w3b, dl, db, *, seq_len, tl):
    return pl.pallas_call(
        functools.partial(_filt_kernel, seq_len=seq_len, tl=tl),
        grid=(seq_len // tl,),
        in_specs=[
            pl.BlockSpec((tl, FEAT_W), lambda i: (i, 0)),
            _resident((FEAT_W, LANES)), _resident((FEAT_W, LANES)), _resident((1, LANES)), _resident((1, LANES)),
            _resident((LANES, LANES)), _resident((1, LANES)), _resident((1, LANES)),
            _resident((LANES, 2 * D_HYENA)), _resident((LANES, 2 * D_HYENA)),
            _resident((1, D_HYENA)), _resident((1, D_HYENA)),
        ],
        out_specs=pl.BlockSpec((2, tl, D_HYENA), lambda i: (0, i, 0)),
        out_shape=jax.ShapeDtypeStruct((2, seq_len, D_HYENA), BF16),
        compiler_params=_cparams(1), name="filt",
    )(feat, w1a, w1b, b1, q1, w2, b2, q2, w3a, w3b, dl, db)


FEAT_W = 40


@functools.lru_cache(maxsize=None)
def _filter_features(seq_len):
    L = seq_len
    bands = (FILTER_EMB - 1) // 2
    pos = np.arange(L, dtype=np.float64)[:, None]
    t = pos / (L - 1)
    w = 2.0 * np.pi * pos / L
    f = np.linspace(1e-4, bands - 1, bands)[None]
    pad = np.zeros((L, FEAT_W - FILTER_EMB))
    return np.concatenate([t, np.cos(f * w), -np.sin(f * w), pad], axis=1).astype(np.float32)


FF_CHUNK = 256


def _ffn_kernel(hm_ref, hp_ref, hn_ref, am_ref, ap_ref, an_ref, xm_ref, xp_ref, xn_ref,
                ym_ref, yp_ref, yn_ref, p_ref, gh_ref, wo_ref, gf_ref, wu_ref, wc_ref, bc_ref,
                wd_ref, wg_ref, wp_ref, o_ref, act_ref, *, tm, tiles_per_seq):
    i = pl.program_id(0)
    first = (i % tiles_per_seq) == 0
    last = (i % tiles_per_seq) == tiles_per_seq - 1

    def stack(m_ref, p_ref, n_ref):
        p = p_ref[BF16_ROWS - SUBLANES:, :]
        n = n_ref[:SUBLANES, :]
        p = jnp.where(first, jnp.zeros_like(p), p)
        n = jnp.where(last, jnp.zeros_like(n), n)
        return jnp.concatenate([m_ref[...], n, p], axis=0)

    h = stack(hm_ref, hp_ref, hn_ref)
    a = stack(am_ref, ap_ref, an_ref)
    hy = stack(xm_ref, xp_ref, xn_ref).astype(F32) * stack(ym_ref, yp_ref, yn_ref).astype(F32)
    hyn = _rms(hy, gh_ref[...]).astype(BF16)
    h1 = (h + jnp.dot(a, wo_ref[:D_ATTN, :], preferred_element_type=F32)
          + jnp.dot(hyn, wo_ref[D_ATTN:, :], preferred_element_type=F32))
    n2 = _rms(h1, gf_ref[...]).astype(BF16)

    def conv(c0):
        u = jnp.dot(n2, wu_ref[:, c0:c0 + FF_CHUNK], preferred_element_type=F32)
        return _conv3(u, tm, wc_ref, bc_ref, c0, c0 + FF_CHUNK)

    for cidx in range(D_FF // FF_CHUNK):
        c0 = cidx * FF_CHUNK
        ua = conv(c0)
        ug = conv(D_FF + c0)
        act_ref[:, c0:c0 + FF_CHUNK] = (ug * jax.nn.sigmoid(ug) * ua).astype(BF16)

    h2 = h1[:tm] + jnp.dot(act_ref[...], wd_ref[...], preferred_element_type=F32)
    gate = jax.nn.sigmoid(jnp.dot(h2.astype(BF16), wg_ref[...], preferred_element_type=F32))
    pp = jnp.dot(p_ref[...].astype(BF16), wp_ref[...], preferred_element_type=F32)
    o_ref[...] = h2 + gate * pp


def _ffn(h, attn, x0, y, p, gh, wo, gf, wu, wc, bc, wd, wg, wp, *, seq_len, tm, layer):
    p_tile0 = layer * (h.shape[0] // tm)
    T = h.shape[0]
    tiles_per_seq = seq_len // tm
    r16 = tm // BF16_ROWS
    nb16 = T // BF16_ROWS
    main = lambda i: (i, 0)
    prev = lambda i: (jnp.maximum(i * r16 - 1, 0), 0)
    nxt = lambda i: (jnp.minimum((i + 1) * r16, nb16 - 1), 0)

    def trio(width):
        return [pl.BlockSpec((tm, width), main), pl.BlockSpec((BF16_ROWS, width), prev),
                pl.BlockSpec((BF16_ROWS, width), nxt)]

    in_specs = (trio(D_MODEL) + trio(D_ATTN) + trio(D_HYENA) + trio(D_HYENA) + [
        pl.BlockSpec((tm, D_PLE), lambda i: (p_tile0 + i, 0)),
        _resident((1, D_HYENA)),
        _resident((D_MODEL, D_MODEL), layer),
        _resident((1, D_MODEL)),
        _resident((D_MODEL, 2 * D_FF), layer),
        _resident((3, 2 * D_FF)),
        _resident((1, 2 * D_FF)),
        _resident((D_FF, D_MODEL), layer),
        _resident((D_MODEL, D_MODEL), layer),
        _resident((D_PLE, D_MODEL), layer),
    ])
    return pl.pallas_call(
        functools.partial(_ffn_kernel, tm=tm, tiles_per_seq=tiles_per_seq),
        grid=(T // tm,), in_specs=in_specs,
        out_specs=pl.BlockSpec((tm, D_MODEL), main),
        out_shape=jax.ShapeDtypeStruct((T, D_MODEL), F32),
        scratch_shapes=[pltpu.VMEM((tm, D_FF), BF16)],
        compiler_params=_cparams(1), name="ffn",
    )(h, h, h, attn, attn, attn, x0, x0, x0, y, y, y, p, gh, wo, gf, wu, wc, bc, wd, wg, wp)


def _rope_slabs(seq_len):
    half = ROPE_DIM // 2
    inv = ROPE_THETA ** (-jnp.arange(0, ROPE_DIM, 2, dtype=F32) / ROPE_DIM)
    ang = jnp.arange(seq_len, dtype=F32)[:, None] * inv[None]
    cos, sin = jnp.cos(ang), jnp.sin(ang)
    ones = jnp.ones((seq_len, HEAD_DIM - ROPE_DIM), F32)
    zeros = jnp.zeros((seq_len, HEAD_DIM - ROPE_DIM), F32)
    zh = jnp.zeros((seq_len, half), F32)
    rc = jnp.concatenate([cos, cos, ones], axis=1)
    rs1 = jnp.concatenate([-sin, zh, zeros], axis=1)
    rs2 = jnp.concatenate([zh, sin, zeros], axis=1)
    rep = LANES // HEAD_DIM
    return tuple(jnp.tile(a, (1, rep)) for a in (rc, rs1, rs2))


def _group_consts(seq_len):
    n1h, k1p = _fft_sizes(seq_len)
    w1, winv, g, gs, gi, gis = _dft_consts(seq_len)
    fc, fs, tc, ts = _twiddle_tables(seq_len)
    return dict(
        seq_len=seq_len, n1h=n1h, k1p=k1p,
        w1_half=jnp.asarray(w1[:, :n1h], BF16),
        winv=jnp.asarray(winv, BF16),
        g=jnp.asarray(g, F32), gs=jnp.asarray(gs, F32), gi=jnp.asarray(gi, F32), gis=jnp.asarray(gis, F32),
        fc=fc, fs=fs, tc=tc, ts=ts, rope=_rope_slabs(seq_len), feat=_filter_features(seq_len),
    )


def _pair(v):
    return jnp.concatenate([v, v])[None]


def _layer_weights(i, rms_mix, w_in, q_norm, k_norm, sink, w_short, b_short, filt_w1, filt_b1, filt_freq1,
                   filt_w2, filt_b2, filt_freq2, filt_w3, hyena_bias, norm_attn_out, norm_hyena_out,
                   w_out, rms_ffn, w_up, w_ffconv, b_ffconv, w_down, w_ple_gate, w_ple_proj):
    rep_q = D_ATTN // HEAD_DIM
    rep_k = D_KV // HEAD_DIM
    hid = FILTER_HID
    w1p = jnp.zeros((FEAT_W, hid), F32).at[:FILTER_EMB].set(filt_w1[i])
    zpad = jnp.zeros((FEAT_W, hid), F32)
    zsq = jnp.zeros((hid, hid), F32)
    z3 = jnp.zeros((hid, 2 * D_HYENA), F32)
    deltas = jnp.abs(jnp.linspace(MIN_DECAY, MAX_DECAY, D_HYENA, dtype=F32))
    return dict(
        g_mix=rms_mix[i][None], w_in=w_in,
        qg=(jnp.tile(q_norm[i], rep_q) * (HEAD_DIM ** -0.5))[None], kg=jnp.tile(k_norm[i], rep_k)[None],
        sink=sink[i], ws=w_short[i], bs=b_short[i][None],
        fw1a=jnp.concatenate([w1p, zpad], axis=1), fw1b=jnp.concatenate([zpad, w1p], axis=1),
        fb1=_pair(filt_b1[i]), fq1=_pair(filt_freq1[i]),
        fw2=jnp.block([[filt_w2[i], zsq], [zsq, filt_w2[i]]]),
        fb2=_pair(filt_b2[i]), fq2=_pair(filt_freq2[i]),
        fw3a=jnp.concatenate([filt_w3[i], z3], axis=0), fw3b=jnp.concatenate([z3, filt_w3[i]], axis=0),
        deltas=deltas[None], dbias=hyena_bias[i][None],
        g_attn=norm_attn_out[i][None], g_hy=norm_hyena_out[i][None],
        w_out=w_out, g_ffn=rms_ffn[i][None], w_up=w_up,
        wc=w_ffconv[i], bc=b_ffconv[i][None], w_down=w_down,
        w_gate=w_ple_gate, w_proj=w_ple_proj,
    )


def _layer(h, p_all, layer, lw, gc, bd, batch):
    L = gc["seq_len"]
    T = batch * L
    n1h, k1p = gc["n1h"], gc["k1p"]
    rc, rs1, rs2 = gc["rope"]
    tm = min(TILE_FFN, L)
    q, k, v, x0, vv = _inproj(h, lw["g_mix"], lw["w_in"], lw["qg"], lw["kg"], bd, rc, rs1, rs2,
                                lw["ws"], lw["bs"], seq_len=L, tm=min(TILE_INPROJ, L), layer=layer)
    attn = _attn(lw["sink"], q, k, v, lw["g_attn"], seq_len=L, tq=min(TILE_ATTN, L))
    filt = _filt(gc["feat"], lw["fw1a"], lw["fw1b"], lw["fb1"], lw["fq1"], lw["fw2"], lw["fb2"], lw["fq2"],
                 lw["fw3a"], lw["fw3b"], lw["deltas"], lw["dbias"], seq_len=L, tl=min(TILE_FILT, L))
    fa = _fft1(gc["w1_half"], filt.reshape(2, n1h, FFT_N2, D_HYENA), k1p=k1p)
    hspec = _fspec(gc["g"], gc["gs"], gc["fc"], gc["fs"], fa)
    a = _fft1(gc["w1_half"], vv.reshape(batch, n1h, FFT_N2, D_HYENA), k1p=k1p)
    a = _fft2(gc["g"], gc["gs"], gc["gi"], gc["gis"], gc["fc"], gc["fs"], gc["tc"], gc["ts"], a, hspec)
    y = _ifft1(gc["winv"], a).reshape(T, D_HYENA)
    return _ffn(h, attn, x0, y, p_all, lw["g_hy"], lw["w_out"], lw["g_ffn"], lw["w_up"], lw["wc"], lw["bc"],
                lw["w_down"], lw["w_gate"], lw["w_proj"], seq_len=L, tm=tm, layer=layer)


def _trunk(x, p, weights):
    batch, L, _ = x.shape
    gc = _group_consts(L)
    bd = jnp.asarray(np.kron(np.eye(MXU_W // HEAD_DIM), np.full((HEAD_DIM, HEAD_DIM), 1.0 / HEAD_DIM)), BF16)
    h = x.reshape(batch * L, D_MODEL)
    p_all = p.reshape(DEPTH * batch * L, D_PLE)
    for i in range(DEPTH):
        lw = _layer_weights(i, *weights)
        h = _layer(h, p_all, i, lw, gc, bd, batch)
    return h.reshape(batch, L, D_MODEL)


def kernel(x_prompt, x_sample, p_prompt, p_sample, rms_mix, w_in, q_norm, k_norm, sink, w_short, b_short, filt_w1, filt_b1, filt_freq1, filt_w2, filt_b2, filt_freq2, filt_w3, hyena_bias, norm_attn_out, norm_hyena_out, w_out, rms_ffn, w_up, w_ffconv, b_ffconv, w_down, w_ple_gate, w_ple_proj):
    w_in, w_out, w_up, w_down, w_ple_gate, w_ple_proj = (
        w.astype(BF16) for w in (w_in, w_out, w_up, w_down, w_ple_gate, w_ple_proj))
    w_in = jnp.concatenate([w_in[..., D_QKV:], w_in[..., :D_QKV]], axis=-1)
    weights = (rms_mix, w_in, q_norm, k_norm, sink, w_short, b_short, filt_w1, filt_b1, filt_freq1,
               filt_w2, filt_b2, filt_freq2, filt_w3, hyena_bias, norm_attn_out, norm_hyena_out,
               w_out, rms_ffn, w_up, w_ffconv, b_ffconv, w_down, w_ple_gate, w_ple_proj)
    y_prompt = _trunk(x_prompt, p_prompt, weights)
    y_sample = _trunk(x_sample, p_sample, weights)
    return (y_prompt, y_sample)
```

```python
import functools
import math

import numpy as np
import jax
import jax.numpy as jnp
from jax import lax
from jax.experimental import pallas as pl
from jax.experimental.pallas import tpu as pltpu

F32 = jnp.float32
BF16 = jnp.bfloat16

D_MODEL = 1024
DEPTH = 4
N_Q_HEADS = 8
N_KV_HEADS = 2
HEAD_DIM = 64
D_ATTN = N_Q_HEADS * HEAD_DIM
D_KV = N_KV_HEADS * HEAD_DIM
D_QKV = D_ATTN + 2 * D_KV
WINDOW = 128
BLOCK = 128
ROPE_THETA = 500000.0
ROPE_DIM = HEAD_DIM // 4
D_HYENA = 512
FILTER_EMB = 33
FILTER_HID = 64
FAST_DECAY_PCT = 0.3
SLOW_DECAY_PCT = 1.5
DECAY_TARGET = 1e-2
MAX_DECAY = math.log(DECAY_TARGET) / FAST_DECAY_PCT
MIN_DECAY = math.log(DECAY_TARGET) / SLOW_DECAY_PCT
D_IN = D_QKV + 3 * D_HYENA
D_FF = 2816
D_PLE = 256
EPS = 1e-6
MASKED = -1e30

LANES = 128
SUBLANES = 8
BF16_ROWS = 16
MXU_W = 256
Q_SLABS = N_Q_HEADS * LANES
COL_HYENA = 0
COL_Q = 3 * D_HYENA
COL_K = COL_Q + D_ATTN
COL_V = COL_K + D_KV
FFT_N2 = 64
FFT_MG = 32
FFT_CW = 256
FFT_KB = 24
TILE_INPROJ = 1024
TILE_ATTN = 2048
TILE_FFN = 512
TILE_FILT = 2048
VMEM_LIMIT = 56 * 1024 * 1024


def _cparams(n_axes):
    return pltpu.CompilerParams(dimension_semantics=("arbitrary",) * n_axes,
                                vmem_limit_bytes=VMEM_LIMIT)


def _resident(shape, layer=None):
    if layer is None:
        return pl.BlockSpec(shape, lambda *_: (0,) * len(shape), pipeline_mode=pl.Buffered(1))
    return pl.BlockSpec((None,) + tuple(shape), lambda *_: (layer,) + (0,) * len(shape),
                        pipeline_mode=pl.Buffered(1))


def _rms(x, g):
    ms = jnp.mean(x * x, axis=-1, keepdims=True)
    return x * lax.rsqrt(ms + EPS) * g


def _conv3(u, tm, w_ref, b_ref, c0, c1):
    rows = u.shape[0]
    up = pltpu.roll(u, 1, axis=0)[:tm]
    un = pltpu.roll(u, rows - 1, axis=0)[:tm]
    return up * w_ref[0:1, c0:c1] + u[:tm] * w_ref[1:2, c0:c1] + un * w_ref[2:3, c0:c1] + b_ref[:, c0:c1]


def _inproj_kernel(xm_ref, xp_ref, xn_ref, g_ref, w_ref, qg_ref, kg_ref, bd_ref,
                   rc_ref, rs1_ref, rs2_ref, ws_ref, bs_ref,
                   q_ref, k_ref, v_ref, x0_ref, vv_ref, *, tm, tiles_per_seq):
    i = pl.program_id(0)
    first = (i % tiles_per_seq) == 0
    last = (i % tiles_per_seq) == tiles_per_seq - 1
    xp = jnp.where(first, 0.0, xp_ref[...])
    xn = jnp.where(last, 0.0, xn_ref[...])
    x = jnp.concatenate([xm_ref[...], xn, xp], axis=0)
    n = _rms(x, g_ref[...]).astype(BF16)
    z = jnp.dot(n, w_ref[...], preferred_element_type=F32)

    bd = bd_ref[...]
    rc, rs1, rs2 = rc_ref[...], rs1_ref[...], rs2_ref[...]

    def norm_rope(zs, gain, bds):
        sq = (zs * zs).astype(BF16)
        step = bds.shape[0]
        ms = jnp.concatenate([jnp.dot(sq[:, c:c + step], bds, preferred_element_type=F32)
                              for c in range(0, zs.shape[1], step)], axis=1)
        y = zs * lax.rsqrt(ms + EPS) * gain
        outs = []
        for s in range(y.shape[1] // LANES):
            ys = y[:, s * LANES:(s + 1) * LANES]
            outs.append(ys * rc + pltpu.roll(ys, LANES - ROPE_DIM // 2, axis=1) * rs1
                        + pltpu.roll(ys, ROPE_DIM // 2, axis=1) * rs2)
        return outs

    u = _conv3(z[:, COL_HYENA:COL_HYENA + 3 * D_HYENA], tm, ws_ref, bs_ref, 0, 3 * D_HYENA)
    x0_ref[...] = u[:, :D_HYENA].astype(BF16)
    vv_ref[...] = (u[:, D_HYENA:2 * D_HYENA] * u[:, 2 * D_HYENA:]).astype(BF16)

    qs = norm_rope(z[:tm, COL_Q:COL_Q + D_ATTN], qg_ref[...], bd)
    low = lax.broadcasted_iota(jnp.int32, (tm, LANES), 1) < HEAD_DIM
    heads_per_kv = N_Q_HEADS // N_KV_HEADS
    for s, qv in enumerate(qs):
        qr = pltpu.roll(qv, HEAD_DIM, axis=1)
        for half in range(2):
            h = 2 * s + half
            src = qv if half == h // heads_per_kv else qr
            keep = low if h // heads_per_kv == 0 else jnp.logical_not(low)
            q_ref[:, h * LANES:(h + 1) * LANES] = jnp.where(keep, src, 0.0).astype(BF16)
    k_ref[...] = norm_rope(z[:tm, COL_K:COL_K + D_KV], kg_ref[...], bd[:D_KV, :D_KV])[0].astype(BF16)
    v_ref[...] = z[:tm, COL_V:COL_V + D_KV].astype(BF16)


def _inproj(x, g, w, qg, kg, bd, rc, rs1, rs2, ws, bs, *, seq_len, tm, layer):
    T = x.shape[0]
    tiles_per_seq = seq_len // tm
    r8 = tm // SUBLANES
    nb8 = T // SUBLANES
    rope_spec = pl.BlockSpec((tm, LANES), lambda i: (i % tiles_per_seq, 0))
    in_specs = [
        pl.BlockSpec((tm, D_MODEL), lambda i: (i, 0)),
        pl.BlockSpec((SUBLANES, D_MODEL), lambda i: (jnp.maximum(i * r8 - 1, 0), 0)),
        pl.BlockSpec((SUBLANES, D_MODEL), lambda i: (jnp.minimum((i + 1) * r8, nb8 - 1), 0)),
        _resident((1, D_MODEL)),
        _resident((D_MODEL, D_IN), layer),
        _resident((1, D_ATTN)),
        _resident((1, D_KV)),
        _resident((MXU_W, MXU_W)),
        rope_spec, rope_spec, rope_spec,
        _resident((3, 3 * D_HYENA)),
        _resident((1, 3 * D_HYENA)),
    ]
    widths = (Q_SLABS, D_KV, D_KV, D_HYENA, D_HYENA)
    out_specs = [pl.BlockSpec((tm, wd), lambda i: (i, 0)) for wd in widths]
    out_shape = [jax.ShapeDtypeStruct((T, wd), BF16) for wd in widths]
    return pl.pallas_call(
        functools.partial(_inproj_kernel, tm=tm, tiles_per_seq=tiles_per_seq),
        grid=(T // tm,), in_specs=in_specs, out_specs=out_specs, out_shape=out_shape,
        compiler_params=_cparams(1), name="inproj",
    )(x, x, x, g, w, qg, kg, bd, rc, rs1, rs2, ws, bs)


def _attn_kernel(sink_ref, q_ref, km_ref, kp_ref, kn_ref, vm_ref, vp_ref, vn_ref, g_ref,
                 o_ref, kbuf, vbuf, *, tq, tiles_per_seq):
    i = pl.program_id(0)
    first = (i % tiles_per_seq) == 0
    last = (i % tiles_per_seq) == tiles_per_seq - 1
    nblk = tq // BLOCK
    H = N_Q_HEADS
    G = N_Q_HEADS // N_KV_HEADS
    kbuf[0:BLOCK] = kp_ref[...]
    kbuf[BLOCK:BLOCK + tq] = km_ref[...]
    kbuf[BLOCK + tq:] = kn_ref[...]
    vbuf[0:BLOCK, :D_KV] = vp_ref[...]
    vbuf[BLOCK:BLOCK + tq, :D_KV] = vm_ref[...]
    vbuf[BLOCK + tq:, :D_KV] = vn_ref[...]
    vbuf[:, D_KV:] = jnp.ones((tq + 2 * BLOCK, LANES), BF16)
    r = lax.broadcasted_iota(jnp.int32, (BLOCK, 3 * BLOCK), 0)
    c = lax.broadcasted_iota(jnp.int32, (BLOCK, 3 * BLOCK), 1)
    d = c - BLOCK - r
    band = jnp.where((d >= -WINDOW) & (d <= WINDOW), 0.0, MASKED)
    low = lax.broadcasted_iota(jnp.int32, (BLOCK, LANES), 1) < HEAD_DIM
    rid = lax.broadcasted_iota(jnp.int32, (H * BLOCK, 1), 0)
    sk = jnp.zeros((H * BLOCK, 1), F32)
    for h in range(H):
        sk = jnp.where((rid >= h * BLOCK) & (rid < (h + 1) * BLOCK), sink_ref[h], sk)
    for j in range(nblk):
        bias = band
        if j == 0:
            bias = jnp.where(first & (c < BLOCK), MASKED, bias)
        if j == nblk - 1:
            bias = jnp.where(last & (c >= 2 * BLOCK), MASKED, bias)
        kb = kbuf[j * BLOCK:(j + 3) * BLOCK, :]
        vb = vbuf[j * BLOCK:(j + 3) * BLOCK, :]
        ql = jnp.concatenate([q_ref[j * BLOCK:(j + 1) * BLOCK, h * LANES:(h + 1) * LANES] for h in range(H)],
                             axis=0)
        s = lax.dot_general(ql, kb, (((1,), (1,)), ((), ())), preferred_element_type=F32)
        s = (s.reshape(H, BLOCK, 3 * BLOCK) + bias[None]).reshape(H * BLOCK, 3 * BLOCK)
        m = jnp.maximum(jnp.max(s, axis=-1, keepdims=True), sk)
        e = jnp.exp((s - m).astype(BF16))
        out = jnp.dot(e, vb, preferred_element_type=F32)
        res = out[:, :LANES] / (out[:, LANES:] + jnp.exp(sk - m))
        slabs = []
        for p in range(H // 2):
            ra = res[(2 * p) * BLOCK:(2 * p + 1) * BLOCK]
            rb = res[(2 * p + 1) * BLOCK:(2 * p + 2) * BLOCK]
            if (2 * p) // G == 0:
                slabs.append(jnp.where(low, ra, pltpu.roll(rb, HEAD_DIM, axis=1)))
            else:
                slabs.append(jnp.where(low, pltpu.roll(ra, HEAD_DIM, axis=1), rb))
        o = jnp.concatenate(slabs, axis=1)
        o_ref[j * BLOCK:(j + 1) * BLOCK, :] = _rms(o, g_ref[...]).astype(BF16)


def _attn(sink, q, k, v, g, *, seq_len, tq):
    T = q.shape[0]
    tiles_per_seq = seq_len // tq
    rb = tq // BLOCK
    nbb = T // BLOCK
    main = lambda i, s: (i, 0)
    prev = lambda i, s: (jnp.maximum(i * rb - 1, 0), 0)
    nxt = lambda i, s: (jnp.minimum((i + 1) * rb, nbb - 1), 0)
    kv_trio = [pl.BlockSpec((tq, D_KV), main), pl.BlockSpec((BLOCK, D_KV), prev), pl.BlockSpec((BLOCK, D_KV), nxt)]
    grid_spec = pltpu.PrefetchScalarGridSpec(
        num_scalar_prefetch=1, grid=(T // tq,),
        in_specs=[pl.BlockSpec((tq, Q_SLABS), main)] + kv_trio + kv_trio + [
            pl.BlockSpec((1, D_ATTN), lambda i, s: (0, 0))],
        out_specs=pl.BlockSpec((tq, D_ATTN), main),
        scratch_shapes=[pltpu.VMEM((tq + 2 * BLOCK, D_KV), BF16), pltpu.VMEM((tq + 2 * BLOCK, D_KV + LANES), BF16)],
    )
    return pl.pallas_call(
        functools.partial(_attn_kernel, tq=tq, tiles_per_seq=tiles_per_seq),
        grid_spec=grid_spec, out_shape=jax.ShapeDtypeStruct((T, D_ATTN), BF16),
        compiler_params=_cparams(1), name="attn",
    )(sink, q, k, k, k, v, v, v, g)


def _fft_sizes(seq_len):
    n1h = seq_len // FFT_N2
    k1p = -(-(n1h + 1) // FFT_KB) * FFT_KB
    return n1h, k1p


@functools.lru_cache(maxsize=None)
def _dft_consts(seq_len):
    n1h, k1p = _fft_sizes(seq_len)
    n1 = 2 * n1h
    k = np.arange(k1p)[:, None]
    valid = (k <= n1h)
    th = 2.0 * np.pi * ((k * np.arange(n1)[None, :]) % n1) / n1
    w1 = np.concatenate([np.cos(th) * valid, -np.sin(th) * valid], axis=0)
    scale = np.where((k == 0) | (k == n1h), 1.0, 2.0) * valid / n1
    thh = th[:, :n1h]
    winv = np.concatenate([np.cos(thh) * scale, -np.sin(thh) * scale], axis=0).T
    a = 2.0 * np.pi * ((np.arange(FFT_N2)[:, None] * np.arange(FFT_N2)[None, :]) % FFT_N2) / FFT_N2
    cm, sm = np.cos(a), np.sin(a)
    g = np.block([[cm, sm], [-sm, cm]])
    gi = np.block([[cm, -sm], [sm, cm]]) / FFT_N2
    gs = np.concatenate([g[:, FFT_N2:], g[:, :FFT_N2]], axis=1)
    gis = np.concatenate([gi[FFT_N2:], gi[:FFT_N2]], axis=0)
    return w1, winv, g, gs, gi, gis


def _twiddle_tables(seq_len):
    n1h, k1p = _fft_sizes(seq_len)
    n = 2 * seq_len
    k = jnp.arange(k1p, dtype=jnp.int32)[:, None]
    m = jnp.arange(FFT_N2, dtype=jnp.int32)[None, :]
    ph = ((k * m) % n).astype(F32) * (2.0 * math.pi / n)
    cs, sn = jnp.cos(ph), jnp.sin(ph)
    fc = jnp.concatenate([cs, cs], axis=1)
    fs = jnp.concatenate([-sn, sn], axis=1)
    tc = jnp.broadcast_to(cs[:, :, None], (k1p, FFT_N2, LANES))
    ts = jnp.broadcast_to(sn[:, :, None], (k1p, FFT_N2, LANES))
    return fc, fs, tc, ts


def _fft1_kernel(w_ref, x_ref, o_ref, *, k1p):
    for s in range(FFT_CW // LANES):
        lanes = slice(s * LANES, (s + 1) * LANES)
        a = jnp.einsum('kn,nmc->kmc', w_ref[...], x_ref[:, :, lanes], preferred_element_type=F32)
        o_ref[0, :, :, lanes] = a[:k1p].astype(BF16)
        o_ref[1, :, :, lanes] = a[k1p:].astype(BF16)


def _fft1(w1, x, *, k1p):
    B, n1in, _, C = x.shape
    grid = (FFT_N2 // FFT_MG, B, C // FFT_CW)
    return pl.pallas_call(
        functools.partial(_fft1_kernel, k1p=k1p),
        grid=grid,
        in_specs=[
            _resident((2 * k1p, n1in)),
            pl.BlockSpec((None, n1in, FFT_MG, FFT_CW), lambda m, b, j: (b, 0, m, j)),
        ],
        out_specs=pl.BlockSpec((None, 2, k1p, FFT_MG, FFT_CW), lambda m, b, j: (b, 0, 0, m, j)),
        out_shape=jax.ShapeDtypeStruct((B, 2, k1p, FFT_N2, C), BF16),
        compiler_params=_cparams(3), name="fft1",
    )(w1, x)


def _pair_diag(a, b):
    z = jnp.zeros_like(a)
    return jnp.concatenate([jnp.concatenate([a, z], axis=1), jnp.concatenate([z, b], axis=1)], axis=0).astype(BF16)


def _fwd_pair(g_ref, gs_ref, fc_ref, fs_ref, a_ref, j):
    def mat(jj):
        return g_ref[...] * fc_ref[jj:jj + 1, :] + gs_ref[...] * fs_ref[jj:jj + 1, :]
    x = jnp.concatenate([a_ref[0, j], a_ref[1, j], a_ref[0, j + 1], a_ref[1, j + 1]], axis=0)
    return jnp.dot(_pair_diag(mat(j), mat(j + 1)), x, preferred_element_type=F32)


def _fspec_kernel(g_ref, gs_ref, fc_ref, fs_ref, a_ref, o_ref):
    n = FFT_N2
    for j in range(0, FFT_KB, 2):
        bf = _fwd_pair(g_ref, gs_ref, fc_ref, fs_ref, a_ref.at[0], j)
        bb = _fwd_pair(g_ref, gs_ref, fc_ref, fs_ref, a_ref.at[1], j)
        for t in range(2):
            re = slice(2 * t * n, (2 * t + 1) * n)
            im = slice((2 * t + 1) * n, (2 * t + 2) * n)
            o_ref[0, j + t] = (bf[re] + bb[re]).astype(BF16)
            o_ref[1, j + t] = (bf[im] - bb[im]).astype(BF16)


def _fspec(g, gs, fc, fs, a):
    _, _, k1p, _, C = a.shape
    row = pl.BlockSpec((FFT_KB, 2 * FFT_N2), lambda kb: (kb, 0))
    return pl.pallas_call(
        _fspec_kernel, grid=(k1p // FFT_KB,),
        in_specs=[_resident((2 * FFT_N2, 2 * FFT_N2)), _resident((2 * FFT_N2, 2 * FFT_N2)), row, row,
                  pl.BlockSpec((2, 2, FFT_KB, FFT_N2, C), lambda kb: (0, 0, kb, 0, 0))],
        out_specs=pl.BlockSpec((2, FFT_KB, FFT_N2, C), lambda kb: (0, kb, 0, 0)),
        out_shape=jax.ShapeDtypeStruct(a.shape[1:], BF16),
        compiler_params=_cparams(1), name="fspec",
    )(g, gs, fc, fs, a)


def _fft2_kernel(g_ref, gs_ref, gi_ref, gis_ref, fc_ref, fs_ref, tc_ref, ts_ref, a_ref, h_ref, o_ref):
    n = FFT_N2

    def inv(jj):
        tc, ts = tc_ref[jj], ts_ref[jj]
        return (gi_ref[...] * jnp.concatenate([tc, tc], axis=0)
                + gis_ref[...] * jnp.concatenate([-ts, ts], axis=0))

    for j in range(0, FFT_KB, 2):
        b = _fwd_pair(g_ref, gs_ref, fc_ref, fs_ref, a_ref, j)
        ys = []
        for t in range(2):
            br, bi = b[2 * t * n:(2 * t + 1) * n], b[(2 * t + 1) * n:(2 * t + 2) * n]
            hr, hi = h_ref[0, j + t].astype(F32), h_ref[1, j + t].astype(F32)
            ys += [br * hr - bi * hi, br * hi + bi * hr]
        y = jnp.concatenate(ys, axis=0).astype(BF16)
        p = jnp.dot(_pair_diag(inv(j), inv(j + 1)), y, preferred_element_type=F32)
        for t in range(2):
            o_ref[0, j + t] = p[2 * t * n:(2 * t + 1) * n].astype(BF16)
            o_ref[1, j + t] = p[(2 * t + 1) * n:(2 * t + 2) * n].astype(BF16)


def _fft2(g, gs, gi, gis, fc, fs, tc, ts, a, hspec):
    B, _, k1p, _, C = a.shape
    mat = _resident((2 * FFT_N2, 2 * FFT_N2))
    row = pl.BlockSpec((FFT_KB, 2 * FFT_N2), lambda kb, b: (kb, 0))
    col = pl.BlockSpec((FFT_KB, FFT_N2, LANES), lambda kb, b: (kb, 0, 0))
    return pl.pallas_call(
        _fft2_kernel, grid=(k1p // FFT_KB, B),
        in_specs=[
            mat, mat, mat, mat, row, row, col, col,
            pl.BlockSpec((None, 2, FFT_KB, FFT_N2, C), lambda kb, b: (b, 0, kb, 0, 0)),
            pl.BlockSpec((2, FFT_KB, FFT_N2, C), lambda kb, b: (0, kb, 0, 0)),
        ],
        out_specs=pl.BlockSpec((None, 2, FFT_KB, FFT_N2, C), lambda kb, b: (b, 0, kb, 0, 0)),
        out_shape=jax.ShapeDtypeStruct(a.shape, BF16),
        compiler_params=_cparams(2), name="fft2",
    )(g, gs, gi, gis, fc, fs, tc, ts, a, hspec)


def _ifft1_kernel(w_ref, a_ref, o_ref):
    for s in range(FFT_CW // LANES):
        lanes = slice(s * LANES, (s + 1) * LANES)
        a = jnp.concatenate([a_ref[0, :, :, lanes], a_ref[1, :, :, lanes]], axis=0)
        o_ref[:, :, lanes] = jnp.einsum('nk,kmc->nmc', w_ref[...], a, preferred_element_type=F32).astype(BF16)


def _ifft1(winv, a):
    B, _, k1p, _, C = a.shape
    n1h = winv.shape[0]
    return pl.pallas_call(
        _ifft1_kernel, grid=(FFT_N2 // FFT_MG, B, C // FFT_CW),
        in_specs=[
            _resident((n1h, 2 * k1p)),
            pl.BlockSpec((None, 2, k1p, FFT_MG, FFT_CW), lambda m, b, j: (b, 0, 0, m, j)),
        ],
        out_specs=pl.BlockSpec((None, n1h, FFT_MG, FFT_CW), lambda m, b, j: (b, 0, m, j)),
        out_shape=jax.ShapeDtypeStruct((B, n1h, FFT_N2, C), BF16),
        compiler_params=_cparams(3), name="ifft1",
    )(winv, a)


def _split_dot(a, b):
    ah = a.astype(BF16)
    al = (a - ah.astype(F32)).astype(BF16)
    bh = b.astype(BF16)
    bl = (b - bh.astype(F32)).astype(BF16)
    d = functools.partial(jnp.dot, preferred_element_type=F32)
    return d(ah, bh) + d(ah, bl) + d(al, bh)


def _filt_kernel(ft_ref, w1a_ref, w1b_ref, b1_ref, q1_ref, w2_ref, b2_ref, q2_ref, w3a_ref, w3b_ref,
                 dl_ref, db_ref, o_ref, *, seq_len, tl):
    i = pl.program_id(0)
    L = seq_len
    hl = tl // 2
    pre = _split_dot(ft_ref[:hl, :], w1a_ref[...]) + _split_dot(ft_ref[hl:, :], w1b_ref[...])
    h = jnp.sin(q1_ref[...] * (pre + b1_ref[...]))
    h = jnp.sin(q2_ref[...] * (_split_dot(h, w2_ref[...]) + b2_ref[...]))
    for half, w3_ref in enumerate((w3a_ref, w3b_ref)):
        hh = _split_dot(h, w3_ref[...])
        n = i * tl + half * hl + lax.broadcasted_iota(jnp.int32, (hl, D_HYENA), 0)
        decay = jnp.exp(-(n.astype(F32) / (L - 1)) * dl_ref[...])
        rows = slice(half * hl, (half + 1) * hl)
        o_ref[0, rows, :] = (hh[:, :D_HYENA] * decay + jnp.where(n == 0, db_ref[...], 0.0)).astype(BF16)
        o_ref[1, rows, :] = jnp.where(n == 0, 0.0, hh[:, D_HYENA:] * decay).astype(BF16)


def _filt(feat, w1a, w1b, b1, q1, w2, b2, q2, w3a, w3b, dl, db, *, seq_len, tl):
    return pl.pallas_call(
        functools.partial(_filt_kernel, seq_len=seq_len, tl=tl),
        grid=(seq_len // tl,),
        in_specs=[
            pl.BlockSpec((tl, FEAT_W), lambda i: (i, 0)),
            _resident((FEAT_W, LANES)), _resident((FEAT_W, LANES)), _resident((1, LANES)), _resident((1, LANES)),
            _resident((LANES, LANES)), _resident((1, LANES)), _resident((1, LANES)),
            _resident((LANES, 2 * D_HYENA)), _resident((LANES, 2 * D_HYENA)),
            _resident((1, D_HYENA)), _resident((1, D_HYENA)),
        ],
        out_specs=pl.BlockSpec((2, tl, D_HYENA), lambda i: (0, i, 0)),
        out_shape=jax.ShapeDtypeStruct((2, seq_len, D_HYENA), BF16),
        compiler_params=_cparams(1), name="filt",
    )(feat, w1a, w1b, b1, q1, w2, b2, q2, w3a, w3b, dl, db)


FEAT_W = 40


@functools.lru_cache(maxsize=None)
def _filter_features(seq_len):
    L = seq_len
    bands = (FILTER_EMB - 1) // 2
    pos = np.arange(L, dtype=np.float64)[:, None]
    t = pos / (L - 1)
    w = 2.0 * np.pi * pos / L
    f = np.linspace(1e-4, bands - 1, bands)[None]
    pad = np.zeros((L, FEAT_W - FILTER_EMB))
    return np.concatenate([t, np.cos(f * w), -np.sin(f * w), pad], axis=1).astype(np.float32)


FF_CHUNK = 256


def _ffn_kernel(hm_ref, hp_ref, hn_ref, am_ref, ap_ref, an_ref, xm_ref, xp_ref, xn_ref,
                ym_ref, yp_ref, yn_ref, p_ref, gh_ref, wo_ref, gf_ref, wu_ref, wc_ref, bc_ref,
                wd_ref, wg_ref, wp_ref, o_ref, act_ref, *, tm, tiles_per_seq):
    i = pl.program_id(0)
    first = (i % tiles_per_seq) == 0
    last = (i % tiles_per_seq) == tiles_per_seq - 1

    def stack(m_ref, p_ref, n_ref):
        p = p_ref[BF16_ROWS - SUBLANES:, :]
        n = n_ref[:SUBLANES, :]
        p = jnp.where(first, jnp.zeros_like(p), p)
        n = jnp.where(last, jnp.zeros_like(n), n)
        return jnp.concatenate([m_ref[...], n, p], axis=0)

    h = stack(hm_ref, hp_ref, hn_ref)
    a = stack(am_ref, ap_ref, an_ref)
    hy = stack(xm_ref, xp_ref, xn_ref).astype(F32) * stack(ym_ref, yp_ref, yn_ref).astype(F32)
    hyn = _rms(hy, gh_ref[...]).astype(BF16)
    h1 = (h + jnp.dot(a, wo_ref[:D_ATTN, :], preferred_element_type=F32)
          + jnp.dot(hyn, wo_ref[D_ATTN:, :], preferred_element_type=F32))
    n2 = _rms(h1, gf_ref[...]).astype(BF16)

    def conv(c0):
        u = jnp.dot(n2, wu_ref[:, c0:c0 + FF_CHUNK], preferred_element_type=F32)
        return _conv3(u, tm, wc_ref, bc_ref, c0, c0 + FF_CHUNK)

    for cidx in range(D_FF // FF_CHUNK):
        c0 = cidx * FF_CHUNK
        ua = conv(c0)
        ug = conv(D_FF + c0)
        act_ref[:, c0:c0 + FF_CHUNK] = (ug * jax.nn.sigmoid(ug) * ua).astype(BF16)

    h2 = h1[:tm] + jnp.dot(act_ref[...], wd_ref[...], preferred_element_type=F32)
    gate = jax.nn.sigmoid(jnp.dot(h2.astype(BF16), wg_ref[...], preferred_element_type=F32))
    pp = jnp.dot(p_ref[...].astype(BF16), wp_ref[...], preferred_element_type=F32)
    o_ref[...] = h2 + gate * pp


def _ffn(h, attn, x0, y, p, gh, wo, gf, wu, wc, bc, wd, wg, wp, *, seq_len, tm, layer):
    p_tile0 = layer * (h.shape[0] // tm)
    T = h.shape[0]
    tiles_per_seq = seq_len // tm
    r16 = tm // BF16_ROWS
    nb16 = T // BF16_ROWS
    main = lambda i: (i, 0)
    prev = lambda i: (jnp.maximum(i * r16 - 1, 0), 0)
    nxt = lambda i: (jnp.minimum((i + 1) * r16, nb16 - 1), 0)

    def trio(width):
        return [pl.BlockSpec((tm, width), main), pl.BlockSpec((BF16_ROWS, width), prev),
                pl.BlockSpec((BF16_ROWS, width), nxt)]

    in_specs = (trio(D_MODEL) + trio(D_ATTN) + trio(D_HYENA) + trio(D_HYENA) + [
        pl.BlockSpec((tm, D_PLE), lambda i: (p_tile0 + i, 0)),
        _resident((1, D_HYENA)),
        _resident((D_MODEL, D_MODEL), layer),
        _resident((1, D_MODEL)),
        _resident((D_MODEL, 2 * D_FF), layer),
        _resident((3, 2 * D_FF)),
        _resident((1, 2 * D_FF)),
        _resident((D_FF, D_MODEL), layer),
        _resident((D_MODEL, D_MODEL), layer),
        _resident((D_PLE, D_MODEL), layer),
    ])
    return pl.pallas_call(
        functools.partial(_ffn_kernel, tm=tm, tiles_per_seq=tiles_per_seq),
        grid=(T // tm,), in_specs=in_specs,
        out_specs=pl.BlockSpec((tm, D_MODEL), main),
        out_shape=jax.ShapeDtypeStruct((T, D_MODEL), F32),
        scratch_shapes=[pltpu.VMEM((tm, D_FF), BF16)],
        compiler_params=_cparams(1), name="ffn",
    )(h, h, h, attn, attn, attn, x0, x0, x0, y, y, y, p, gh, wo, gf, wu, wc, bc, wd, wg, wp)


def _rope_slabs(seq_len):
    half = ROPE_DIM // 2
    inv = ROPE_THETA ** (-jnp.arange(0, ROPE_DIM, 2, dtype=F32) / ROPE_DIM)
    ang = jnp.arange(seq_len, dtype=F32)[:, None] * inv[None]
    cos, sin = jnp.cos(ang), jnp.sin(ang)
    ones = jnp.ones((seq_len, HEAD_DIM - ROPE_DIM), F32)
    zeros = jnp.zeros((seq_len, HEAD_DIM - ROPE_DIM), F32)
    zh = jnp.zeros((seq_len, half), F32)
    rc = jnp.concatenate([cos, cos, ones], axis=1)
    rs1 = jnp.concatenate([-sin, zh, zeros], axis=1)
    rs2 = jnp.concatenate([zh, sin, zeros], axis=1)
    rep = LANES // HEAD_DIM
    return tuple(jnp.tile(a, (1, rep)) for a in (rc, rs1, rs2))


def _group_consts(seq_len):
    n1h, k1p = _fft_sizes(seq_len)
    w1, winv, g, gs, gi, gis = _dft_consts(seq_len)
    fc, fs, tc, ts = _twiddle_tables(seq_len)
    return dict(
        seq_len=seq_len, n1h=n1h, k1p=k1p,
        w1_half=jnp.asarray(w1[:, :n1h], BF16),
        winv=jnp.asarray(winv, BF16),
        g=jnp.asarray(g, F32), gs=jnp.asarray(gs, F32), gi=jnp.asarray(gi, F32), gis=jnp.asarray(gis, F32),
        fc=fc, fs=fs, tc=tc, ts=ts, rope=_rope_slabs(seq_len), feat=_filter_features(seq_len),
    )


def _pair(v):
    return jnp.concatenate([v, v])[None]


def _layer_weights(i, rms_mix, w_in, q_norm, k_norm, sink, w_short, b_short, filt_w1, filt_b1, filt_freq1,
                   filt_w2, filt_b2, filt_freq2, filt_w3, hyena_bias, norm_attn_out, norm_hyena_out,
                   w_out, rms_ffn, w_up, w_ffconv, b_ffconv, w_down, w_ple_gate, w_ple_proj):
    rep_q = D_ATTN // HEAD_DIM
    rep_k = D_KV // HEAD_DIM
    hid = FILTER_HID
    w1p = jnp.zeros((FEAT_W, hid), F32).at[:FILTER_EMB].set(filt_w1[i])
    zpad = jnp.zeros((FEAT_W, hid), F32)
    zsq = jnp.zeros((hid, hid), F32)
    z3 = jnp.zeros((hid, 2 * D_HYENA), F32)
    deltas = jnp.abs(jnp.linspace(MIN_DECAY, MAX_DECAY, D_HYENA, dtype=F32))
    return dict(
        g_mix=rms_mix[i][None], w_in=w_in,
        qg=(jnp.tile(q_norm[i], rep_q) * (HEAD_DIM ** -0.5))[None], kg=jnp.tile(k_norm[i], rep_k)[None],
        sink=sink[i], ws=w_short[i], bs=b_short[i][None],
        fw1a=jnp.concatenate([w1p, zpad], axis=1), fw1b=jnp.concatenate([zpad, w1p], axis=1),
        fb1=_pair(filt_b1[i]), fq1=_pair(filt_freq1[i]),
        fw2=jnp.block([[filt_w2[i], zsq], [zsq, filt_w2[i]]]),
        fb2=_pair(filt_b2[i]), fq2=_pair(filt_freq2[i]),
        fw3a=jnp.concatenate([filt_w3[i], z3], axis=0), fw3b=jnp.concatenate([z3, filt_w3[i]], axis=0),
        deltas=deltas[None], dbias=hyena_bias[i][None],
        g_attn=norm_attn_out[i][None], g_hy=norm_hyena_out[i][None],
        w_out=w_out, g_ffn=rms_ffn[i][None], w_up=w_up,
        wc=w_ffconv[i], bc=b_ffconv[i][None], w_down=w_down,
        w_gate=w_ple_gate, w_proj=w_ple_proj,
    )


def _layer(h, p_all, layer, lw, gc, bd, batch):
    L = gc["seq_len"]
    T = batch * L
    n1h, k1p = gc["n1h"], gc["k1p"]
    rc, rs1, rs2 = gc["rope"]
    tm = min(TILE_FFN, L)
    q, k, v, x0, vv = _inproj(h, lw["g_mix"], lw["w_in"], lw["qg"], lw["kg"], bd, rc, rs1, rs2,
                                lw["ws"], lw["bs"], seq_len=L, tm=min(TILE_INPROJ, L), layer=layer)
    attn = _attn(lw["sink"], q, k, v, lw["g_attn"], seq_len=L, tq=min(TILE_ATTN, L))
    filt = _filt(gc["feat"], lw["fw1a"], lw["fw1b"], lw["fb1"], lw["fq1"], lw["fw2"], lw["fb2"], lw["fq2"],
                 lw["fw3a"], lw["fw3b"], lw["deltas"], lw["dbias"], seq_len=L, tl=min(TILE_FILT, L))
    fa = _fft1(gc["w1_half"], filt.reshape(2, n1h, FFT_N2, D_HYENA), k1p=k1p)
    hspec = _fspec(gc["g"], gc["gs"], gc["fc"], gc["fs"], fa)
    a = _fft1(gc["w1_half"], vv.reshape(batch, n1h, FFT_N2, D_HYENA), k1p=k1p)
    a = _fft2(gc["g"], gc["gs"], gc["gi"], gc["gis"], gc["fc"], gc["fs"], gc["tc"], gc["ts"], a, hspec)
    y = _ifft1(gc["winv"], a).reshape(T, D_HYENA)
    return _ffn(h, attn, x0, y, p_all, lw["g_hy"], lw["w_out"], lw["g_ffn"], lw["w_up"], lw["wc"], lw["bc"],
                lw["w_down"], lw["w_gate"], lw["w_proj"], seq_len=L, tm=tm, layer=layer)


def _trunk(x, p, weights):
    batch, L, _ = x.shape
    gc = _group_consts(L)
    bd = jnp.asarray(np.kron(np.eye(MXU_W // HEAD_DIM), np.full((HEAD_DIM, HEAD_DIM), 1.0 / HEAD_DIM)), BF16)
    h = x.reshape(batch * L, D_MODEL)
    p_all = p.reshape(DEPTH * batch * L, D_PLE)
    for i in range(DEPTH):
        lw = _layer_weights(i, *weights)
        h = _layer(h, p_all, i, lw, gc, bd, batch)
    return h.reshape(batch, L, D_MODEL)


def kernel(x_prompt, x_sample, p_prompt, p_sample, rms_mix, w_in, q_norm, k_norm, sink, w_short, b_short, filt_w1, filt_b1, filt_freq1, filt_w2, filt_b2, filt_freq2, filt_w3, hyena_bias, norm_attn_out, norm_hyena_out, w_out, rms_ffn, w_up, w_ffconv, b_ffconv, w_down, w_ple_gate, w_ple_proj):
    w_in = jnp.concatenate([w_in[..., D_QKV:], w_in[..., :D_QKV]], axis=-1)
    w_in, w_out, w_up, w_down, w_ple_gate, w_ple_proj = (
        w.astype(BF16) for w in (w_in, w_out, w_up, w_down, w_ple_gate, w_ple_proj))
    weights = (rms_mix, w_in, q_norm, k_norm, sink, w_short, b_short, filt_w1, filt_b1, filt_freq1,
               filt_w2, filt_b2, filt_freq2, filt_w3, hyena_bias, norm_attn_out, norm_hyena_out,
               w_out, rms_ffn, w_up, w_ffconv, b_ffconv, w_down, w_ple_gate, w_ple_proj)
    y_prompt = _trunk(x_prompt, p_prompt, weights)
    y_sample = _trunk(x_sample, p_sample, weights)
    return (y_prompt, y_sample)
```

```python
import functools
import math

import numpy as np
import jax
import jax.numpy as jnp
from jax import lax
from jax.experimental import pallas as pl
from jax.experimental.pallas import tpu as pltpu

F32 = jnp.float32
BF16 = jnp.bfloat16

D_MODEL = 1024
DEPTH = 4
N_Q_HEADS = 8
N_KV_HEADS = 2
HEAD_DIM = 64
D_ATTN = N_Q_HEADS * HEAD_DIM
D_KV = N_KV_HEADS * HEAD_DIM
D_QKV = D_ATTN + 2 * D_KV
WINDOW = 128
BLOCK = 128
ROPE_THETA = 500000.0
ROPE_DIM = HEAD_DIM // 4
D_HYENA = 512
FILTER_EMB = 33
FILTER_HID = 64
FAST_DECAY_PCT = 0.3
SLOW_DECAY_PCT = 1.5
DECAY_TARGET = 1e-2
MAX_DECAY = math.log(DECAY_TARGET) / FAST_DECAY_PCT
MIN_DECAY = math.log(DECAY_TARGET) / SLOW_DECAY_PCT
D_IN = D_QKV + 3 * D_HYENA
D_FF = 2816
D_PLE = 256
EPS = 1e-6
MASKED = -1e30

LANES = 128
SUBLANES = 8
BF16_ROWS = 16
MXU_W = 256
Q_SLABS = N_Q_HEADS * LANES
COL_HYENA = 0
COL_Q = 3 * D_HYENA
COL_K = COL_Q + D_ATTN
COL_V = COL_K + D_KV
FFT_N2 = 64
FFT_MG = 32
FFT_CW = 128
FFT_KB = 24
TILE_INPROJ = 1024
TILE_ATTN = 2048
TILE_FFN = 512
TILE_FILT = 2048
VMEM_LIMIT = 56 * 1024 * 1024


def _cparams(n_axes):
    return pltpu.CompilerParams(dimension_semantics=("arbitrary",) * n_axes,
                                vmem_limit_bytes=VMEM_LIMIT)


def _resident(shape, layer=None):
    if layer is None:
        return pl.BlockSpec(shape, lambda *_: (0,) * len(shape), pipeline_mode=pl.Buffered(1))
    return pl.BlockSpec((None,) + tuple(shape), lambda *_: (layer,) + (0,) * len(shape),
                        pipeline_mode=pl.Buffered(1))


def _rms(x, g):
    ms = jnp.mean(x * x, axis=-1, keepdims=True)
    return x * lax.rsqrt(ms + EPS) * g


def _conv3(u, tm, w_ref, b_ref, c0, c1):
    rows = u.shape[0]
    up = pltpu.roll(u, 1, axis=0)[:tm]
    un = pltpu.roll(u, rows - 1, axis=0)[:tm]
    return up * w_ref[0:1, c0:c1] + u[:tm] * w_ref[1:2, c0:c1] + un * w_ref[2:3, c0:c1] + b_ref[:, c0:c1]


def _inproj_kernel(xm_ref, xp_ref, xn_ref, g_ref, w_ref, qg_ref, kg_ref, bd_ref,
                   rc_ref, rs1_ref, rs2_ref, ws_ref, bs_ref,
                   q_ref, k_ref, v_ref, x0_ref, vv_ref, *, tm, tiles_per_seq):
    i = pl.program_id(0)
    first = (i % tiles_per_seq) == 0
    last = (i % tiles_per_seq) == tiles_per_seq - 1
    xp = jnp.where(first, 0.0, xp_ref[...])
    xn = jnp.where(last, 0.0, xn_ref[...])
    x = jnp.concatenate([xm_ref[...], xn, xp], axis=0)
    n = _rms(x, g_ref[...]).astype(BF16)
    z = jnp.dot(n, w_ref[...], preferred_element_type=F32)

    bd = bd_ref[...]
    rc, rs1, rs2 = rc_ref[...], rs1_ref[...], rs2_ref[...]

    def norm_rope(zs, gain, bds):
        sq = (zs * zs).astype(BF16)
        step = bds.shape[0]
        ms = jnp.concatenate([jnp.dot(sq[:, c:c + step], bds, preferred_element_type=F32)
                              for c in range(0, zs.shape[1], step)], axis=1)
        y = zs * lax.rsqrt(ms + EPS) * gain
        outs = []
        for s in range(y.shape[1] // LANES):
            ys = y[:, s * LANES:(s + 1) * LANES]
            outs.append(ys * rc + pltpu.roll(ys, LANES - ROPE_DIM // 2, axis=1) * rs1
                        + pltpu.roll(ys, ROPE_DIM // 2, axis=1) * rs2)
        return outs

    u = _conv3(z[:, COL_HYENA:COL_HYENA + 3 * D_HYENA], tm, ws_ref, bs_ref, 0, 3 * D_HYENA)
    x0_ref[...] = u[:, :D_HYENA].astype(BF16)
    vv_ref[...] = (u[:, D_HYENA:2 * D_HYENA] * u[:, 2 * D_HYENA:]).astype(BF16)

    qs = norm_rope(z[:tm, COL_Q:COL_Q + D_ATTN], qg_ref[...], bd)
    low = lax.broadcasted_iota(jnp.int32, (tm, LANES), 1) < HEAD_DIM
    heads_per_kv = N_Q_HEADS // N_KV_HEADS
    for s, qv in enumerate(qs):
        qr = pltpu.roll(qv, HEAD_DIM, axis=1)
        for half in range(2):
            h = 2 * s + half
            src = qv if half == h // heads_per_kv else qr
            keep = low if h // heads_per_kv == 0 else jnp.logical_not(low)
            q_ref[:, h * LANES:(h + 1) * LANES] = jnp.where(keep, src, 0.0).astype(BF16)
    k_ref[...] = norm_rope(z[:tm, COL_K:COL_K + D_KV], kg_ref[...], bd[:D_KV, :D_KV])[0].astype(BF16)
    v_ref[...] = z[:tm, COL_V:COL_V + D_KV].astype(BF16)


def _inproj(x, g, w, qg, kg, bd, rc, rs1, rs2, ws, bs, *, seq_len, tm, layer):
    T = x.shape[0]
    tiles_per_seq = seq_len // tm
    r8 = tm // SUBLANES
    nb8 = T // SUBLANES
    rope_spec = pl.BlockSpec((tm, LANES), lambda i: (i % tiles_per_seq, 0))
    in_specs = [
        pl.BlockSpec((tm, D_MODEL), lambda i: (i, 0)),
        pl.BlockSpec((SUBLANES, D_MODEL), lambda i: (jnp.maximum(i * r8 - 1, 0), 0)),
        pl.BlockSpec((SUBLANES, D_MODEL), lambda i: (jnp.minimum((i + 1) * r8, nb8 - 1), 0)),
        _resident((1, D_MODEL)),
        _resident((D_MODEL, D_IN), layer),
        _resident((1, D_ATTN)),
        _resident((1, D_KV)),
        _resident((MXU_W, MXU_W)),
        rope_spec, rope_spec, rope_spec,
        _resident((3, 3 * D_HYENA)),
        _resident((1, 3 * D_HYENA)),
    ]
    widths = (Q_SLABS, D_KV, D_KV, D_HYENA, D_HYENA)
    out_specs = [pl.BlockSpec((tm, wd), lambda i: (i, 0)) for wd in widths]
    out_shape = [jax.ShapeDtypeStruct((T, wd), BF16) for wd in widths]
    return pl.pallas_call(
        functools.partial(_inproj_kernel, tm=tm, tiles_per_seq=tiles_per_seq),
        grid=(T // tm,), in_specs=in_specs, out_specs=out_specs, out_shape=out_shape,
        compiler_params=_cparams(1), name="inproj",
    )(x, x, x, g, w, qg, kg, bd, rc, rs1, rs2, ws, bs)


def _attn_kernel(sink_ref, q_ref, km_ref, kp_ref, kn_ref, vm_ref, vp_ref, vn_ref, g_ref,
                 o_ref, kbuf, vbuf, *, tq, tiles_per_seq):
    i = pl.program_id(0)
    first = (i % tiles_per_seq) == 0
    last = (i % tiles_per_seq) == tiles_per_seq - 1
    nblk = tq // BLOCK
    H = N_Q_HEADS
    G = N_Q_HEADS // N_KV_HEADS
    kbuf[0:BLOCK] = kp_ref[...]
    kbuf[BLOCK:BLOCK + tq] = km_ref[...]
    kbuf[BLOCK + tq:] = kn_ref[...]
    vbuf[0:BLOCK, :D_KV] = vp_ref[...]
    vbuf[BLOCK:BLOCK + tq, :D_KV] = vm_ref[...]
    vbuf[BLOCK + tq:, :D_KV] = vn_ref[...]
    vbuf[:, D_KV:] = jnp.ones((tq + 2 * BLOCK, LANES), BF16)
    r = lax.broadcasted_iota(jnp.int32, (BLOCK, 3 * BLOCK), 0)
    c = lax.broadcasted_iota(jnp.int32, (BLOCK, 3 * BLOCK), 1)
    d = c - BLOCK - r
    band = jnp.where((d >= -WINDOW) & (d <= WINDOW), 0.0, MASKED)
    low = lax.broadcasted_iota(jnp.int32, (BLOCK, LANES), 1) < HEAD_DIM
    rid = lax.broadcasted_iota(jnp.int32, (H * BLOCK, 1), 0)
    sk = jnp.zeros((H * BLOCK, 1), F32)
    for h in range(H):
        sk = jnp.where((rid >= h * BLOCK) & (rid < (h + 1) * BLOCK), sink_ref[h], sk)
    for j in range(nblk):
        bias = band
        if j == 0:
            bias = jnp.where(first & (c < BLOCK), MASKED, bias)
        if j == nblk - 1:
            bias = jnp.where(last & (c >= 2 * BLOCK), MASKED, bias)
        kb = kbuf[j * BLOCK:(j + 3) * BLOCK, :]
        vb = vbuf[j * BLOCK:(j + 3) * BLOCK, :]
        ql = jnp.concatenate([q_ref[j * BLOCK:(j + 1) * BLOCK, h * LANES:(h + 1) * LANES] for h in range(H)],
                             axis=0)
        s = lax.dot_general(ql, kb, (((1,), (1,)), ((), ())), preferred_element_type=F32)
        s = (s.reshape(H, BLOCK, 3 * BLOCK) + bias[None]).reshape(H * BLOCK, 3 * BLOCK)
        m = jnp.maximum(jnp.max(s, axis=-1, keepdims=True), sk)
        e = jnp.exp((s - m).astype(BF16))
        out = jnp.dot(e, vb, preferred_element_type=F32)
        res = out[:, :LANES] / (out[:, LANES:] + jnp.exp(sk - m))
        slabs = []
        for p in range(H // 2):
            ra = res[(2 * p) * BLOCK:(2 * p + 1) * BLOCK]
            rb = res[(2 * p + 1) * BLOCK:(2 * p + 2) * BLOCK]
            if (2 * p) // G == 0:
                slabs.append(jnp.where(low, ra, pltpu.roll(rb, HEAD_DIM, axis=1)))
            else:
                slabs.append(jnp.where(low, pltpu.roll(ra, HEAD_DIM, axis=1), rb))
        o = jnp.concatenate(slabs, axis=1)
        o_ref[j * BLOCK:(j + 1) * BLOCK, :] = _rms(o, g_ref[...]).astype(BF16)


def _attn(sink, q, k, v, g, *, seq_len, tq):
    T = q.shape[0]
    tiles_per_seq = seq_len // tq
    rb = tq // BLOCK
    nbb = T // BLOCK
    main = lambda i, s: (i, 0)
    prev = lambda i, s: (jnp.maximum(i * rb - 1, 0), 0)
    nxt = lambda i, s: (jnp.minimum((i + 1) * rb, nbb - 1), 0)
    kv_trio = [pl.BlockSpec((tq, D_KV), main), pl.BlockSpec((BLOCK, D_KV), prev), pl.BlockSpec((BLOCK, D_KV), nxt)]
    grid_spec = pltpu.PrefetchScalarGridSpec(
        num_scalar_prefetch=1, grid=(T // tq,),
        in_specs=[pl.BlockSpec((tq, Q_SLABS), main)] + kv_trio + kv_trio + [
            pl.BlockSpec((1, D_ATTN), lambda i, s: (0, 0))],
        out_specs=pl.BlockSpec((tq, D_ATTN), main),
        scratch_shapes=[pltpu.VMEM((tq + 2 * BLOCK, D_KV), BF16), pltpu.VMEM((tq + 2 * BLOCK, D_KV + LANES), BF16)],
    )
    return pl.pallas_call(
        functools.partial(_attn_kernel, tq=tq, tiles_per_seq=tiles_per_seq),
        grid_spec=grid_spec, out_shape=jax.ShapeDtypeStruct((T, D_ATTN), BF16),
        compiler_params=_cparams(1), name="attn",
    )(sink, q, k, k, k, v, v, v, g)


def _fft_sizes(seq_len):
    n1h = seq_len // FFT_N2
    k1p = -(-(n1h + 1) // FFT_KB) * FFT_KB
    return n1h, k1p


@functools.lru_cache(maxsize=None)
def _dft_consts(seq_len):
    n1h, k1p = _fft_sizes(seq_len)
    n1 = 2 * n1h
    k = np.arange(k1p)[:, None]
    valid = (k <= n1h)
    th = 2.0 * np.pi * ((k * np.arange(n1)[None, :]) % n1) / n1
    w1 = np.concatenate([np.cos(th) * valid, -np.sin(th) * valid], axis=0)
    scale = np.where((k == 0) | (k == n1h), 1.0, 2.0) * valid / n1
    thh = th[:, :n1h]
    winv = np.concatenate([np.cos(thh) * scale, -np.sin(thh) * scale], axis=0).T
    a = 2.0 * np.pi * ((np.arange(FFT_N2)[:, None] * np.arange(FFT_N2)[None, :]) % FFT_N2) / FFT_N2
    cm, sm = np.cos(a), np.sin(a)
    g = np.block([[cm, sm], [-sm, cm]])
    gi = np.block([[cm, -sm], [sm, cm]]) / FFT_N2
    gs = np.concatenate([g[:, FFT_N2:], g[:, :FFT_N2]], axis=1)
    gis = np.concatenate([gi[FFT_N2:], gi[:FFT_N2]], axis=0)
    return w1, winv, g, gs, gi, gis


def _twiddle_tables(seq_len):
    n1h, k1p = _fft_sizes(seq_len)
    n = 2 * seq_len
    k = jnp.arange(k1p, dtype=jnp.int32)[:, None]
    m = jnp.arange(FFT_N2, dtype=jnp.int32)[None, :]
    ph = ((k * m) % n).astype(F32) * (2.0 * math.pi / n)
    cs, sn = jnp.cos(ph), jnp.sin(ph)
    fc = jnp.concatenate([cs, cs], axis=1)
    fs = jnp.concatenate([-sn, sn], axis=1)
    tc = jnp.broadcast_to(cs[:, :, None], (k1p, FFT_N2, LANES))
    ts = jnp.broadcast_to(sn[:, :, None], (k1p, FFT_N2, LANES))
    return fc, fs, tc, ts


def _fft1_kernel(w_ref, x_ref, o_ref, *, k1p):
    for s in range(FFT_CW // LANES):
        lanes = slice(s * LANES, (s + 1) * LANES)
        a = jnp.einsum('kn,nmc->kmc', w_ref[...], x_ref[:, :, lanes], preferred_element_type=F32)
        o_ref[0, :, :, lanes] = a[:k1p].astype(BF16)
        o_ref[1, :, :, lanes] = a[k1p:].astype(BF16)


def _fft1(w1, x, *, k1p):
    B, n1in, _, C = x.shape
    grid = (FFT_N2 // FFT_MG, B, C // FFT_CW)
    return pl.pallas_call(
        functools.partial(_fft1_kernel, k1p=k1p),
        grid=grid,
        in_specs=[
            _resident((2 * k1p, n1in)),
            pl.BlockSpec((None, n1in, FFT_MG, FFT_CW), lambda m, b, j: (b, 0, m, j)),
        ],
        out_specs=pl.BlockSpec((None, 2, k1p, FFT_MG, FFT_CW), lambda m, b, j: (b, 0, 0, m, j)),
        out_shape=jax.ShapeDtypeStruct((B, 2, k1p, FFT_N2, C), BF16),
        compiler_params=_cparams(3), name="fft1",
    )(w1, x)


def _pair_diag(a, b):
    z = jnp.zeros_like(a)
    return jnp.concatenate([jnp.concatenate([a, z], axis=1), jnp.concatenate([z, b], axis=1)], axis=0).astype(BF16)


def _fwd_pair(g_ref, gs_ref, fc_ref, fs_ref, a_ref, j):
    def mat(jj):
        return g_ref[...] * fc_ref[jj:jj + 1, :] + gs_ref[...] * fs_ref[jj:jj + 1, :]
    x = jnp.concatenate([a_ref[0, j], a_ref[1, j], a_ref[0, j + 1], a_ref[1, j + 1]], axis=0)
    return jnp.dot(_pair_diag(mat(j), mat(j + 1)), x, preferred_element_type=F32)


def _fspec_kernel(g_ref, gs_ref, fc_ref, fs_ref, a_ref, o_ref):
    n = FFT_N2
    for j in range(0, FFT_KB, 2):
        bf = _fwd_pair(g_ref, gs_ref, fc_ref, fs_ref, a_ref.at[0], j)
        bb = _fwd_pair(g_ref, gs_ref, fc_ref, fs_ref, a_ref.at[1], j)
        for t in range(2):
            re = slice(2 * t * n, (2 * t + 1) * n)
            im = slice((2 * t + 1) * n, (2 * t + 2) * n)
            o_ref[0, j + t] = (bf[re] + bb[re]).astype(BF16)
            o_ref[1, j + t] = (bf[im] - bb[im]).astype(BF16)


def _fspec(g, gs, fc, fs, a):
    _, _, k1p, _, C = a.shape
    row = pl.BlockSpec((FFT_KB, 2 * FFT_N2), lambda kb: (kb, 0))
    return pl.pallas_call(
        _fspec_kernel, grid=(k1p // FFT_KB,),
        in_specs=[_resident((2 * FFT_N2, 2 * FFT_N2)), _resident((2 * FFT_N2, 2 * FFT_N2)), row, row,
                  pl.BlockSpec((2, 2, FFT_KB, FFT_N2, C), lambda kb: (0, 0, kb, 0, 0))],
        out_specs=pl.BlockSpec((2, FFT_KB, FFT_N2, C), lambda kb: (0, kb, 0, 0)),
        out_shape=jax.ShapeDtypeStruct(a.shape[1:], BF16),
        compiler_params=_cparams(1), name="fspec",
    )(g, gs, fc, fs, a)


def _fft2_kernel(g_ref, gs_ref, gi_ref, gis_ref, fc_ref, fs_ref, tc_ref, ts_ref, a_ref, h_ref, o_ref):
    n = FFT_N2

    def inv(jj):
        tc, ts = tc_ref[jj], ts_ref[jj]
        return (gi_ref[...] * jnp.concatenate([tc, tc], axis=0)
                + gis_ref[...] * jnp.concatenate([-ts, ts], axis=0))

    for j in range(0, FFT_KB, 2):
        b = _fwd_pair(g_ref, gs_ref, fc_ref, fs_ref, a_ref, j)
        ys = []
        for t in range(2):
            br, bi = b[2 * t * n:(2 * t + 1) * n], b[(2 * t + 1) * n:(2 * t + 2) * n]
            hr, hi = h_ref[0, j + t].astype(F32), h_ref[1, j + t].astype(F32)
            ys += [br * hr - bi * hi, br * hi + bi * hr]
        y = jnp.concatenate(ys, axis=0).astype(BF16)
        p = jnp.dot(_pair_diag(inv(j), inv(j + 1)), y, preferred_element_type=F32)
        for t in range(2):
            o_ref[0, j + t] = p[2 * t * n:(2 * t + 1) * n].astype(BF16)
            o_ref[1, j + t] = p[(2 * t + 1) * n:(2 * t + 2) * n].astype(BF16)


def _fft2(g, gs, gi, gis, fc, fs, tc, ts, a, hspec):
    B, _, k1p, _, C = a.shape
    mat = _resident((2 * FFT_N2, 2 * FFT_N2))
    row = pl.BlockSpec((FFT_KB, 2 * FFT_N2), lambda kb, b: (kb, 0))
    col = pl.BlockSpec((FFT_KB, FFT_N2, LANES), lambda kb, b: (kb, 0, 0))
    return pl.pallas_call(
        _fft2_kernel, grid=(k1p // FFT_KB, B),
        in_specs=[
            mat, mat, mat, mat, row, row, col, col,
            pl.BlockSpec((None, 2, FFT_KB, FFT_N2, C), lambda kb, b: (b, 0, kb, 0, 0)),
            pl.BlockSpec((2, FFT_KB, FFT_N2, C), lambda kb, b: (0, kb, 0, 0)),
        ],
        out_specs=pl.BlockSpec((None, 2, FFT_KB, FFT_N2, C), lambda kb, b: (b, 0, kb, 0, 0)),
        out_shape=jax.ShapeDtypeStruct(a.shape, BF16),
        compiler_params=_cparams(2), name="fft2",
    )(g, gs, gi, gis, fc, fs, tc, ts, a, hspec)


def _ifft1_kernel(w_ref, a_ref, o_ref):
    for s in range(FFT_CW // LANES):
        lanes = slice(s * LANES, (s + 1) * LANES)
        a = jnp.concatenate([a_ref[0, :, :, lanes], a_ref[1, :, :, lanes]], axis=0)
        o_ref[:, :, lanes] = jnp.einsum('nk,kmc->nmc', w_ref[...], a, preferred_element_type=F32).astype(BF16)


def _ifft1(winv, a):
    B, _, k1p, _, C = a.shape
    n1h = winv.shape[0]
    return pl.pallas_call(
        _ifft1_kernel, grid=(FFT_N2 // FFT_MG, B, C // FFT_CW),
        in_specs=[
            _resident((n1h, 2 * k1p)),
            pl.BlockSpec((None, 2, k1p, FFT_MG, FFT_CW), lambda m, b, j: (b, 0, 0, m, j)),
        ],
        out_specs=pl.BlockSpec((None, n1h, FFT_MG, FFT_CW), lambda m, b, j: (b, 0, m, j)),
        out_shape=jax.ShapeDtypeStruct((B, n1h, FFT_N2, C), BF16),
        compiler_params=_cparams(3), name="ifft1",
    )(winv, a)


def _split_dot(a, b):
    ah = a.astype(BF16)
    al = (a - ah.astype(F32)).astype(BF16)
    bh = b.astype(BF16)
    bl = (b - bh.astype(F32)).astype(BF16)
    d = functools.partial(jnp.dot, preferred_element_type=F32)
    return d(ah, bh) + d(ah, bl) + d(al, bh)


def _filt_kernel(ft_ref, w1a_ref, w1b_ref, b1_ref, q1_ref, w2_ref, b2_ref, q2_ref, w3a_ref, w3b_ref,
                 dl_ref, db_ref, o_ref, *, seq_len, tl):
    i = pl.program_id(0)
    L = seq_len
    hl = tl // 2
    pre = _split_dot(ft_ref[:hl, :], w1a_ref[...]) + _split_dot(ft_ref[hl:, :], w1b_ref[...])
    h = jnp.sin(q1_ref[...] * (pre + b1_ref[...]))
    h = jnp.sin(q2_ref[...] * (_split_dot(h, w2_ref[...]) + b2_ref[...]))
    for half, w3_ref in enumerate((w3a_ref, w3b_ref)):
        hh = _split_dot(h, w3_ref[...])
        n = i * tl + half * hl + lax.broadcasted_iota(jnp.int32, (hl, D_HYENA), 0)
        decay = jnp.exp(-(n.astype(F32) / (L - 1)) * dl_ref[...])
        rows = slice(half * hl, (half + 1) * hl)
        o_ref[0, rows, :] = (hh[:, :D_HYENA] * decay + jnp.where(n == 0, db_ref[...], 0.0)).astype(BF16)
        o_ref[1, rows, :] = jnp.where(n == 0, 0.0, hh[:, D_HYENA:] * decay).astype(BF16)


def _filt(feat, w1a, w1b, b1, q1, w2, b2, q2, w3a, w3b, dl, db, *, seq_len, tl):
    return pl.pallas_call(
        functools.partial(_filt_kernel, seq_len=seq_len, tl=tl),
        grid=(seq_len // tl,),
        in_specs=[
            pl.BlockSpec((tl, FEAT_W), lambda i: (i, 0)),
            _resident((FEAT_W, LANES)), _resident((FEAT_W, LANES)), _resident((1, LANES)), _resident((1, LANES)),
            _resident((LANES, LANES)), _resident((1, LANES)), _resident((1, LANES)),
            _resident((LANES, 2 * D_HYENA)), _resident((LANES, 2 * D_HYENA)),
            _resident((1, D_HYENA)), _resident((1, D_HYENA)),
        ],
        out_specs=pl.BlockSpec((2, tl, D_HYENA), lambda i: (0, i, 0)),
        out_shape=jax.ShapeDtypeStruct((2, seq_len, D_HYENA), BF16),
        compiler_params=_cparams(1), name="filt",
    )(feat, w1a, w1b, b1, q1, w2, b2, q2, w3a, w3b, dl, db)


FEAT_W = 40


@functools.lru_cache(maxsize=None)
def _filter_features(seq_len):
    L = seq_len
    bands = (FILTER_EMB - 1) // 2
    pos = np.arange(L, dtype=np.float64)[:, None]
    t = pos / (L - 1)
    w = 2.0 * np.pi * pos / L
    f = np.linspace(1e-4, bands - 1, bands)[None]
    pad = np.zeros((L, FEAT_W - FILTER_EMB))
    return np.concatenate([t, np.cos(f * w), -np.sin(f * w), pad], axis=1).astype(np.float32)


FF_CHUNK = 256


def _ffn_kernel(hm_ref, hp_ref, hn_ref, am_ref, ap_ref, an_ref, xm_ref, xp_ref, xn_ref,
                ym_ref, yp_ref, yn_ref, p_ref, gh_ref, wo_ref, gf_ref, wu_ref, wc_ref, bc_ref,
                wd_ref, wg_ref, wp_ref, o_ref, act_ref, *, tm, tiles_per_seq):
    i = pl.program_id(0)
    first = (i % tiles_per_seq) == 0
    last = (i % tiles_per_seq) == tiles_per_seq - 1

    def stack(m_ref, p_ref, n_ref):
        p = p_ref[BF16_ROWS - SUBLANES:, :]
        n = n_ref[:SUBLANES, :]
        p = jnp.where(first, jnp.zeros_like(p), p)
        n = jnp.where(last, jnp.zeros_like(n), n)
        return jnp.concatenate([m_ref[...], n, p], axis=0)

    h = stack(hm_ref, hp_ref, hn_ref)
    a = stack(am_ref, ap_ref, an_ref)
    hy = stack(xm_ref, xp_ref, xn_ref).astype(F32) * stack(ym_ref, yp_ref, yn_ref).astype(F32)
    hyn = _rms(hy, gh_ref[...]).astype(BF16)
    h1 = (h + jnp.dot(a, wo_ref[:D_ATTN, :], preferred_element_type=F32)
          + jnp.dot(hyn, wo_ref[D_ATTN:, :], preferred_element_type=F32))
    n2 = _rms(h1, gf_ref[...]).astype(BF16)

    def conv(c0):
        u = jnp.dot(n2, wu_ref[:, c0:c0 + FF_CHUNK], preferred_element_type=F32)
        return _conv3(u, tm, wc_ref, bc_ref, c0, c0 + FF_CHUNK)

    for cidx in range(D_FF // FF_CHUNK):
        c0 = cidx * FF_CHUNK
        ua = conv(c0)
        ug = conv(D_FF + c0)
        act_ref[:, c0:c0 + FF_CHUNK] = (ug * jax.nn.sigmoid(ug) * ua).astype(BF16)

    h2 = h1[:tm] + jnp.dot(act_ref[...], wd_ref[...], preferred_element_type=F32)
    gate = jax.nn.sigmoid(jnp.dot(h2.astype(BF16), wg_ref[...], preferred_element_type=F32))
    pp = jnp.dot(p_ref[...].astype(BF16), wp_ref[...], preferred_element_type=F32)
    o_ref[...] = h2 + gate * pp


def _ffn(h, attn, x0, y, p, gh, wo, gf, wu, wc, bc, wd, wg, wp, *, seq_len, tm, layer):
    p_tile0 = layer * (h.shape[0] // tm)
    T = h.shape[0]
    tiles_per_seq = seq_len // tm
    r16 = tm // BF16_ROWS
    nb16 = T // BF16_ROWS
    main = lambda i: (i, 0)
    prev = lambda i: (jnp.maximum(i * r16 - 1, 0), 0)
    nxt = lambda i: (jnp.minimum((i + 1) * r16, nb16 - 1), 0)

    def trio(width):
        return [pl.BlockSpec((tm, width), main), pl.BlockSpec((BF16_ROWS, width), prev),
                pl.BlockSpec((BF16_ROWS, width), nxt)]

    in_specs = (trio(D_MODEL) + trio(D_ATTN) + trio(D_HYENA) + trio(D_HYENA) + [
        pl.BlockSpec((tm, D_PLE), lambda i: (p_tile0 + i, 0)),
        _resident((1, D_HYENA)),
        _resident((D_MODEL, D_MODEL), layer),
        _resident((1, D_MODEL)),
        _resident((D_MODEL, 2 * D_FF), layer),
        _resident((3, 2 * D_FF)),
        _resident((1, 2 * D_FF)),
        _resident((D_FF, D_MODEL), layer),
        _resident((D_MODEL, D_MODEL), layer),
        _resident((D_PLE, D_MODEL), layer),
    ])
    return pl.pallas_call(
        functools.partial(_ffn_kernel, tm=tm, tiles_per_seq=tiles_per_seq),
        grid=(T // tm,), in_specs=in_specs,
        out_specs=pl.BlockSpec((tm, D_MODEL), main),
        out_shape=jax.ShapeDtypeStruct((T, D_MODEL), F32),
        scratch_shapes=[pltpu.VMEM((tm, D_FF), BF16)],
        compiler_params=_cparams(1), name="ffn",
    )(h, h, h, attn, attn, attn, x0, x0, x0, y, y, y, p, gh, wo, gf, wu, wc, bc, wd, wg, wp)


def _rope_slabs(seq_len):
    half = ROPE_DIM // 2
    inv = ROPE_THETA ** (-jnp.arange(0, ROPE_DIM, 2, dtype=F32) / ROPE_DIM)
    ang = jnp.arange(seq_len, dtype=F32)[:, None] * inv[None]
    cos, sin = jnp.cos(ang), jnp.sin(ang)
    ones = jnp.ones((seq_len, HEAD_DIM - ROPE_DIM), F32)
    zeros = jnp.zeros((seq_len, HEAD_DIM - ROPE_DIM), F32)
    zh = jnp.zeros((seq_len, half), F32)
    rc = jnp.concatenate([cos, cos, ones], axis=1)
    rs1 = jnp.concatenate([-sin, zh, zeros], axis=1)
    rs2 = jnp.concatenate([zh, sin, zeros], axis=1)
    rep = LANES // HEAD_DIM
    return tuple(jnp.tile(a, (1, rep)) for a in (rc, rs1, rs2))


def _group_consts(seq_len):
    n1h, k1p = _fft_sizes(seq_len)
    w1, winv, g, gs, gi, gis = _dft_consts(seq_len)
    fc, fs, tc, ts = _twiddle_tables(seq_len)
    return dict(
        seq_len=seq_len, n1h=n1h, k1p=k1p,
        w1_half=jnp.asarray(w1[:, :n1h], BF16),
        winv=jnp.asarray(winv, BF16),
        g=jnp.asarray(g, F32), gs=jnp.asarray(gs, F32), gi=jnp.asarray(gi, F32), gis=jnp.asarray(gis, F32),
        fc=fc, fs=fs, tc=tc, ts=ts, rope=_rope_slabs(seq_len), feat=_filter_features(seq_len),
    )


def _pair(v):
    return jnp.concatenate([v, v])[None]


def _layer_weights(i, rms_mix, w_in, q_norm, k_norm, sink, w_short, b_short, filt_w1, filt_b1, filt_freq1,
                   filt_w2, filt_b2, filt_freq2, filt_w3, hyena_bias, norm_attn_out, norm_hyena_out,
                   w_out, rms_ffn, w_up, w_ffconv, b_ffconv, w_down, w_ple_gate, w_ple_proj):
    rep_q = D_ATTN // HEAD_DIM
    rep_k = D_KV // HEAD_DIM
    hid = FILTER_HID
    w1p = jnp.zeros((FEAT_W, hid), F32).at[:FILTER_EMB].set(filt_w1[i])
    zpad = jnp.zeros((FEAT_W, hid), F32)
    zsq = jnp.zeros((hid, hid), F32)
    z3 = jnp.zeros((hid, 2 * D_HYENA), F32)
    deltas = jnp.abs(jnp.linspace(MIN_DECAY, MAX_DECAY, D_HYENA, dtype=F32))
    return dict(
        g_mix=rms_mix[i][None], w_in=w_in,
        qg=(jnp.tile(q_norm[i], rep_q) * (HEAD_DIM ** -0.5))[None], kg=jnp.tile(k_norm[i], rep_k)[None],
        sink=sink[i], ws=w_short[i], bs=b_short[i][None],
        fw1a=jnp.concatenate([w1p, zpad], axis=1), fw1b=jnp.concatenate([zpad, w1p], axis=1),
        fb1=_pair(filt_b1[i]), fq1=_pair(filt_freq1[i]),
        fw2=jnp.block([[filt_w2[i], zsq], [zsq, filt_w2[i]]]),
        fb2=_pair(filt_b2[i]), fq2=_pair(filt_freq2[i]),
        fw3a=jnp.concatenate([filt_w3[i], z3], axis=0), fw3b=jnp.concatenate([z3, filt_w3[i]], axis=0),
        deltas=deltas[None], dbias=hyena_bias[i][None],
        g_attn=norm_attn_out[i][None], g_hy=norm_hyena_out[i][None],
        w_out=w_out, g_ffn=rms_ffn[i][None], w_up=w_up,
        wc=w_ffconv[i], bc=b_ffconv[i][None], w_down=w_down,
        w_gate=w_ple_gate, w_proj=w_ple_proj,
    )


def _layer(h, p_all, layer, lw, gc, bd, batch):
    L = gc["seq_len"]
    T = batch * L
    n1h, k1p = gc["n1h"], gc["k1p"]
    rc, rs1, rs2 = gc["rope"]
    tm = min(TILE_FFN, L)
    q, k, v, x0, vv = _inproj(h, lw["g_mix"], lw["w_in"], lw["qg"], lw["kg"], bd, rc, rs1, rs2,
                                lw["ws"], lw["bs"], seq_len=L, tm=min(TILE_INPROJ, L), layer=layer)
    attn = _attn(lw["sink"], q, k, v, lw["g_attn"], seq_len=L, tq=min(TILE_ATTN, L))
    filt = _filt(gc["feat"], lw["fw1a"], lw["fw1b"], lw["fb1"], lw["fq1"], lw["fw2"], lw["fb2"], lw["fq2"],
                 lw["fw3a"], lw["fw3b"], lw["deltas"], lw["dbias"], seq_len=L, tl=min(TILE_FILT, L))
    fa = _fft1(gc["w1_half"], filt.reshape(2, n1h, FFT_N2, D_HYENA), k1p=k1p)
    hspec = _fspec(gc["g"], gc["gs"], gc["fc"], gc["fs"], fa)
    a = _fft1(gc["w1_half"], vv.reshape(batch, n1h, FFT_N2, D_HYENA), k1p=k1p)
    a = _fft2(gc["g"], gc["gs"], gc["gi"], gc["gis"], gc["fc"], gc["fs"], gc["tc"], gc["ts"], a, hspec)
    y = _ifft1(gc["winv"], a).reshape(T, D_HYENA)
    return _ffn(h, attn, x0, y, p_all, lw["g_hy"], lw["w_out"], lw["g_ffn"], lw["w_up"], lw["wc"], lw["bc"],
                lw["w_down"], lw["w_gate"], lw["w_proj"], seq_len=L, tm=tm, layer=layer)


def _trunk(x, p, weights):
    batch, L, _ = x.shape
    gc = _group_consts(L)
    bd = jnp.asarray(np.kron(np.eye(MXU_W // HEAD_DIM), np.full((HEAD_DIM, HEAD_DIM), 1.0 / HEAD_DIM)), BF16)
    h = x.reshape(batch * L, D_MODEL)
    p_all = p.reshape(DEPTH * batch * L, D_PLE)
    for i in range(DEPTH):
        lw = _layer_weights(i, *weights)
        h = _layer(h, p_all, i, lw, gc, bd, batch)
    return h.reshape(batch, L, D_MODEL)


def kernel(x_prompt, x_sample, p_prompt, p_sample, rms_mix, w_in, q_norm, k_norm, sink, w_short, b_short, filt_w1, filt_b1, filt_freq1, filt_w2, filt_b2, filt_freq2, filt_w3, hyena_bias, norm_attn_out, norm_hyena_out, w_out, rms_ffn, w_up, w_ffconv, b_ffconv, w_down, w_ple_gate, w_ple_proj):
    w_in = jnp.concatenate([w_in[..., D_QKV:], w_in[..., :D_QKV]], axis=-1)
    w_in, w_out, w_up, w_down, w_ple_gate, w_ple_proj = (
        w.astype(BF16) for w in (w_in, w_out, w_up, w_down, w_ple_gate, w_ple_proj))
    weights = (rms_mix, w_in, q_norm, k_norm, sink, w_short, b_short, filt_w1, filt_b1, filt_freq1,
               filt_w2, filt_b2, filt_freq2, filt_w3, hyena_bias, norm_attn_out, norm_hyena_out,
               w_out, rms_ffn, w_up, w_ffconv, b_ffconv, w_down, w_ple_gate, w_ple_proj)
    y_prompt = _trunk(x_prompt, p_prompt, weights)
    y_sample = _trunk(x_sample, p_sample, weights)
    return (y_prompt, y_sample)
```

```python
import functools
import math

import numpy as np
import jax
import jax.numpy as jnp
from jax import lax
from jax.experimental import pallas as pl
from jax.experimental.pallas import tpu as pltpu

F32 = jnp.float32
BF16 = jnp.bfloat16

D_MODEL = 1024
DEPTH = 4
N_Q_HEADS = 8
N_KV_HEADS = 2
HEAD_DIM = 64
D_ATTN = N_Q_HEADS * HEAD_DIM
D_KV = N_KV_HEADS * HEAD_DIM
D_QKV = D_ATTN + 2 * D_KV
WINDOW = 128
BLOCK = 128
ROPE_THETA = 500000.0
ROPE_DIM = HEAD_DIM // 4
D_HYENA = 512
FILTER_EMB = 33
FILTER_HID = 64
FAST_DECAY_PCT = 0.3
SLOW_DECAY_PCT = 1.5
DECAY_TARGET = 1e-2
MAX_DECAY = math.log(DECAY_TARGET) / FAST_DECAY_PCT
MIN_DECAY = math.log(DECAY_TARGET) / SLOW_DECAY_PCT
D_IN = D_QKV + 3 * D_HYENA
D_FF = 2816
D_PLE = 256
EPS = 1e-6
MASKED = -1e30

LANES = 128
SUBLANES = 8
BF16_ROWS = 16
MXU_W = 256
Q_SLABS = N_Q_HEADS * LANES
COL_HYENA = 0
COL_Q = 3 * D_HYENA
COL_K = COL_Q + D_ATTN
COL_V = COL_K + D_KV
FFT_N2 = 64
FFT_MG = 32
FFT_CW = 128
FFT_KB = 24
TILE_INPROJ = 1024
TILE_ATTN = 2048
TILE_FFN = 512
TILE_FILT = 2048
VMEM_LIMIT = 56 * 1024 * 1024


def _cparams(n_axes):
    return pltpu.CompilerParams(dimension_semantics=("arbitrary",) * n_axes,
                                vmem_limit_bytes=VMEM_LIMIT)


def _resident(shape, layer=None):
    if layer is None:
        return pl.BlockSpec(shape, lambda *_: (0,) * len(shape), pipeline_mode=pl.Buffered(1))
    return pl.BlockSpec((None,) + tuple(shape), lambda *_: (layer,) + (0,) * len(shape),
                        pipeline_mode=pl.Buffered(1))


def _rms(x, g):
    ms = jnp.mean(x * x, axis=-1, keepdims=True)
    return x * lax.rsqrt(ms + EPS) * g


def _conv3(u, tm, w_ref, b_ref, c0, c1):
    rows = u.shape[0]
    up = pltpu.roll(u, 1, axis=0)[:tm]
    un = pltpu.roll(u, rows - 1, axis=0)[:tm]
    return up * w_ref[0:1, c0:c1] + u[:tm] * w_ref[1:2, c0:c1] + un * w_ref[2:3, c0:c1] + b_ref[:, c0:c1]


def _inproj_kernel(xm_ref, xp_ref, xn_ref, g_ref, w_ref, qg_ref, kg_ref, bd_ref,
                   rc_ref, rs1_ref, rs2_ref, ws_ref, bs_ref,
                   q_ref, k_ref, v_ref, x0_ref, vv_ref, *, tm, tiles_per_seq):
    i = pl.program_id(0)
    first = (i % tiles_per_seq) == 0
    last = (i % tiles_per_seq) == tiles_per_seq - 1
    xp = jnp.where(first, 0.0, xp_ref[...])
    xn = jnp.where(last, 0.0, xn_ref[...])
    x = jnp.concatenate([xm_ref[...], xn, xp], axis=0)
    n = _rms(x, g_ref[...]).astype(BF16)
    z = jnp.dot(n, w_ref[...], preferred_element_type=F32)

    bd = bd_ref[...]
    rc, rs1, rs2 = rc_ref[...], rs1_ref[...], rs2_ref[...]

    def norm_rope(zs, gain, bds):
        sq = (zs * zs).astype(BF16)
        step = bds.shape[0]
        ms = jnp.concatenate([jnp.dot(sq[:, c:c + step], bds, preferred_element_type=F32)
                              for c in range(0, zs.shape[1], step)], axis=1)
        y = zs * lax.rsqrt(ms + EPS) * gain
        outs = []
        for s in range(y.shape[1] // LANES):
            ys = y[:, s * LANES:(s + 1) * LANES]
            outs.append(ys * rc + pltpu.roll(ys, LANES - ROPE_DIM // 2, axis=1) * rs1
                        + pltpu.roll(ys, ROPE_DIM // 2, axis=1) * rs2)
        return outs

    u = _conv3(z[:, COL_HYENA:COL_HYENA + 3 * D_HYENA], tm, ws_ref, bs_ref, 0, 3 * D_HYENA)
    x0_ref[...] = u[:, :D_HYENA].astype(BF16)
    vv_ref[...] = (u[:, D_HYENA:2 * D_HYENA] * u[:, 2 * D_HYENA:]).astype(BF16)

    qs = norm_rope(z[:tm, COL_Q:COL_Q + D_ATTN], qg_ref[...], bd)
    low = lax.broadcasted_iota(jnp.int32, (tm, LANES), 1) < HEAD_DIM
    heads_per_kv = N_Q_HEADS // N_KV_HEADS
    for s, qv in enumerate(qs):
        qr = pltpu.roll(qv, HEAD_DIM, axis=1)
        for half in range(2):
            h = 2 * s + half
            src = qv if half == h // heads_per_kv else qr
            keep = low if h // heads_per_kv == 0 else jnp.logical_not(low)
            q_ref[:, h * LANES:(h + 1) * LANES] = jnp.where(keep, src, 0.0).astype(BF16)
    k_ref[...] = norm_rope(z[:tm, COL_K:COL_K + D_KV], kg_ref[...], bd[:D_KV, :D_KV])[0].astype(BF16)
    v_ref[...] = z[:tm, COL_V:COL_V + D_KV].astype(BF16)


def _inproj(x, g, w, qg, kg, bd, rc, rs1, rs2, ws, bs, *, seq_len, tm, layer):
    T = x.shape[0]
    tiles_per_seq = seq_len // tm
    r8 = tm // SUBLANES
    nb8 = T // SUBLANES
    rope_spec = pl.BlockSpec((tm, LANES), lambda i: (i % tiles_per_seq, 0))
    in_specs = [
        pl.BlockSpec((tm, D_MODEL), lambda i: (i, 0)),
        pl.BlockSpec((SUBLANES, D_MODEL), lambda i: (jnp.maximum(i * r8 - 1, 0), 0)),
        pl.BlockSpec((SUBLANES, D_MODEL), lambda i: (jnp.minimum((i + 1) * r8, nb8 - 1), 0)),
        _resident((1, D_MODEL)),
        _resident((D_MODEL, D_IN), layer),
        _resident((1, D_ATTN)),
        _resident((1, D_KV)),
        _resident((MXU_W, MXU_W)),
        rope_spec, rope_spec, rope_spec,
        _resident((3, 3 * D_HYENA)),
        _resident((1, 3 * D_HYENA)),
    ]
    widths = (Q_SLABS, D_KV, D_KV, D_HYENA, D_HYENA)
    out_specs = [pl.BlockSpec((tm, wd), lambda i: (i, 0)) for wd in widths]
    out_shape = [jax.ShapeDtypeStruct((T, wd), BF16) for wd in widths]
    return pl.pallas_call(
        functools.partial(_inproj_kernel, tm=tm, tiles_per_seq=tiles_per_seq),
        grid=(T // tm,), in_specs=in_specs, out_specs=out_specs, out_shape=out_shape,
        compiler_params=_cparams(1), name="inproj",
    )(x, x, x, g, w, qg, kg, bd, rc, rs1, rs2, ws, bs)


def _attn_kernel(sink_ref, q_ref, km_ref, kp_ref, kn_ref, vm_ref, vp_ref, vn_ref, g_ref,
                 o_ref, kbuf, vbuf, *, tq, tiles_per_seq):
    i = pl.program_id(0)
    first = (i % tiles_per_seq) == 0
    last = (i % tiles_per_seq) == tiles_per_seq - 1
    nblk = tq // BLOCK
    H = N_Q_HEADS
    G = N_Q_HEADS // N_KV_HEADS
    kbuf[0:BLOCK] = kp_ref[...]
    kbuf[BLOCK:BLOCK + tq] = km_ref[...]
    kbuf[BLOCK + tq:] = kn_ref[...]
    vbuf[0:BLOCK, :D_KV] = vp_ref[...]
    vbuf[BLOCK:BLOCK + tq, :D_KV] = vm_ref[...]
    vbuf[BLOCK + tq:, :D_KV] = vn_ref[...]
    vbuf[:, D_KV:] = jnp.ones((tq + 2 * BLOCK, LANES), BF16)
    r = lax.broadcasted_iota(jnp.int32, (BLOCK, 3 * BLOCK), 0)
    c = lax.broadcasted_iota(jnp.int32, (BLOCK, 3 * BLOCK), 1)
    d = c - BLOCK - r
    band = jnp.where((d >= -WINDOW) & (d <= WINDOW), 0.0, MASKED)
    low = lax.broadcasted_iota(jnp.int32, (BLOCK, LANES), 1) < HEAD_DIM
    rid = lax.broadcasted_iota(jnp.int32, (H * BLOCK, 1), 0)
    sk = jnp.zeros((H * BLOCK, 1), F32)
    for h in range(H):
        sk = jnp.where((rid >= h * BLOCK) & (rid < (h + 1) * BLOCK), sink_ref[h], sk)
    for j in range(nblk):
        bias = band
        if j == 0:
            bias = jnp.where(first & (c < BLOCK), MASKED, bias)
        if j == nblk - 1:
            bias = jnp.where(last & (c >= 2 * BLOCK), MASKED, bias)
        kb = kbuf[j * BLOCK:(j + 3) * BLOCK, :]
        vb = vbuf[j * BLOCK:(j + 3) * BLOCK, :]
        ql = jnp.concatenate([q_ref[j * BLOCK:(j + 1) * BLOCK, h * LANES:(h + 1) * LANES] for h in range(H)],
                             axis=0)
        s = lax.dot_general(ql, kb, (((1,), (1,)), ((), ())), preferred_element_type=F32)
        s = (s.reshape(H, BLOCK, 3 * BLOCK) + bias[None]).reshape(H * BLOCK, 3 * BLOCK)
        m = jnp.maximum(jnp.max(s, axis=-1, keepdims=True), sk)
        e = jnp.exp((s - m).astype(BF16))
        out = jnp.dot(e, vb, preferred_element_type=F32)
        res = out[:, :LANES] / (out[:, LANES:] + jnp.exp(sk - m))
        slabs = []
        for p in range(H // 2):
            ra = res[(2 * p) * BLOCK:(2 * p + 1) * BLOCK]
            rb = res[(2 * p + 1) * BLOCK:(2 * p + 2) * BLOCK]
            if (2 * p) // G == 0:
                slabs.append(jnp.where(low, ra, pltpu.roll(rb, HEAD_DIM, axis=1)))
            else:
                slabs.append(jnp.where(low, pltpu.roll(ra, HEAD_DIM, axis=1), rb))
        o = jnp.concatenate(slabs, axis=1)
        o_ref[j * BLOCK:(j + 1) * BLOCK, :] = _rms(o, g_ref[...]).astype(BF16)


def _attn(sink, q, k, v, g, *, seq_len, tq):
    T = q.shape[0]
    tiles_per_seq = seq_len // tq
    rb = tq // BLOCK
    nbb = T // BLOCK
    main = lambda i, s: (i, 0)
    prev = lambda i, s: (jnp.maximum(i * rb - 1, 0), 0)
    nxt = lambda i, s: (jnp.minimum((i + 1) * rb, nbb - 1), 0)
    kv_trio = [pl.BlockSpec((tq, D_KV), main), pl.BlockSpec((BLOCK, D_KV), prev), pl.BlockSpec((BLOCK, D_KV), nxt)]
    grid_spec = pltpu.PrefetchScalarGridSpec(
        num_scalar_prefetch=1, grid=(T // tq,),
        in_specs=[pl.BlockSpec((tq, Q_SLABS), main)] + kv_trio + kv_trio + [
            pl.BlockSpec((1, D_ATTN), lambda i, s: (0, 0))],
        out_specs=pl.BlockSpec((tq, D_ATTN), main),
        scratch_shapes=[pltpu.VMEM((tq + 2 * BLOCK, D_KV), BF16), pltpu.VMEM((tq + 2 * BLOCK, D_KV + LANES), BF16)],
    )
    return pl.pallas_call(
        functools.partial(_attn_kernel, tq=tq, tiles_per_seq=tiles_per_seq),
        grid_spec=grid_spec, out_shape=jax.ShapeDtypeStruct((T, D_ATTN), BF16),
        compiler_params=_cparams(1), name="attn",
    )(sink, q, k, k, k, v, v, v, g)


def _fft_sizes(seq_len):
    n1h = seq_len // FFT_N2
    k1p = -(-(n1h + 1) // FFT_KB) * FFT_KB
    return n1h, k1p


@functools.lru_cache(maxsize=None)
def _dft_consts(seq_len):
    n1h, k1p = _fft_sizes(seq_len)
    n1 = 2 * n1h
    k = np.arange(k1p)[:, None]
    valid = (k <= n1h)
    th = 2.0 * np.pi * ((k * np.arange(n1)[None, :]) % n1) / n1
    w1 = np.concatenate([np.cos(th) * valid, -np.sin(th) * valid], axis=0)
    scale = np.where((k == 0) | (k == n1h), 1.0, 2.0) * valid / n1
    thh = th[:, :n1h]
    winv = np.concatenate([np.cos(thh) * scale, -np.sin(thh) * scale], axis=0).T
    a = 2.0 * np.pi * ((np.arange(FFT_N2)[:, None] * np.arange(FFT_N2)[None, :]) % FFT_N2) / FFT_N2
    cm, sm = np.cos(a), np.sin(a)
    g = np.block([[cm, sm], [-sm, cm]])
    gi = np.block([[cm, -sm], [sm, cm]]) / FFT_N2
    gs = np.concatenate([g[:, FFT_N2:], g[:, :FFT_N2]], axis=1)
    gis = np.concatenate([gi[FFT_N2:], gi[:FFT_N2]], axis=0)
    return w1, winv, g, gs, gi, gis


def _twiddle_tables(seq_len):
    n1h, k1p = _fft_sizes(seq_len)
    n = 2 * seq_len
    k = jnp.arange(k1p, dtype=jnp.int32)[:, None]
    m = jnp.arange(FFT_N2, dtype=jnp.int32)[None, :]
    ph = ((k * m) % n).astype(F32) * (2.0 * math.pi / n)
    cs, sn = jnp.cos(ph), jnp.sin(ph)
    fc = jnp.concatenate([cs, cs], axis=1)
    fs = jnp.concatenate([-sn, sn], axis=1)
    tc = jnp.broadcast_to(cs[:, :, None], (k1p, FFT_N2, LANES))
    ts = jnp.broadcast_to(sn[:, :, None], (k1p, FFT_N2, LANES))
    return fc, fs, tc, ts


def _fft1_kernel(w_ref, x_ref, o_ref, *, k1p):
    for s in range(FFT_CW // LANES):
        lanes = slice(s * LANES, (s + 1) * LANES)
        a = jnp.einsum('kn,nmc->kmc', w_ref[...], x_ref[:, :, lanes], preferred_element_type=F32)
        o_ref[0, :, :, lanes] = a[:k1p].astype(BF16)
        o_ref[1, :, :, lanes] = a[k1p:].astype(BF16)


def _fft1(w1, x, *, k1p):
    B, n1in, _, C = x.shape
    grid = (FFT_N2 // FFT_MG, B, C // FFT_CW)
    return pl.pallas_call(
        functools.partial(_fft1_kernel, k1p=k1p),
        grid=grid,
        in_specs=[
            _resident((2 * k1p, n1in)),
            pl.BlockSpec((None, n1in, FFT_MG, FFT_CW), lambda m, b, j: (b, 0, m, j)),
        ],
        out_specs=pl.BlockSpec((None, 2, k1p, FFT_MG, FFT_CW), lambda m, b, j: (b, 0, 0, m, j)),
        out_shape=jax.ShapeDtypeStruct((B, 2, k1p, FFT_N2, C), BF16),
        compiler_params=_cparams(3), name="fft1",
    )(w1, x)


def _pair_diag(a, b):
    z = jnp.zeros_like(a)
    return jnp.concatenate([jnp.concatenate([a, z], axis=1), jnp.concatenate([z, b], axis=1)], axis=0).astype(BF16)


def _fwd_pair(g_ref, gs_ref, fc_ref, fs_ref, a_ref, j):
    def mat(jj):
        return g_ref[...] * fc_ref[jj:jj + 1, :] + gs_ref[...] * fs_ref[jj:jj + 1, :]
    x = jnp.concatenate([a_ref[0, j], a_ref[1, j], a_ref[0, j + 1], a_ref[1, j + 1]], axis=0)
    return jnp.dot(_pair_diag(mat(j), mat(j + 1)), x, preferred_element_type=F32)


def _fspec_kernel(g_ref, gs_ref, fc_ref, fs_ref, a_ref, o_ref):
    n = FFT_N2
    for j in range(0, FFT_KB, 2):
        bf = _fwd_pair(g_ref, gs_ref, fc_ref, fs_ref, a_ref.at[0], j)
        bb = _fwd_pair(g_ref, gs_ref, fc_ref, fs_ref, a_ref.at[1], j)
        for t in range(2):
            re = slice(2 * t * n, (2 * t + 1) * n)
            im = slice((2 * t + 1) * n, (2 * t + 2) * n)
            o_ref[0, j + t] = (bf[re] + bb[re]).astype(BF16)
            o_ref[1, j + t] = (bf[im] - bb[im]).astype(BF16)


def _fspec(g, gs, fc, fs, a):
    _, _, k1p, _, C = a.shape
    row = pl.BlockSpec((FFT_KB, 2 * FFT_N2), lambda kb: (kb, 0))
    return pl.pallas_call(
        _fspec_kernel, grid=(k1p // FFT_KB,),
        in_specs=[_resident((2 * FFT_N2, 2 * FFT_N2)), _resident((2 * FFT_N2, 2 * FFT_N2)), row, row,
                  pl.BlockSpec((2, 2, FFT_KB, FFT_N2, C), lambda kb: (0, 0, kb, 0, 0))],
        out_specs=pl.BlockSpec((2, FFT_KB, FFT_N2, C), lambda kb: (0, kb, 0, 0)),
        out_shape=jax.ShapeDtypeStruct(a.shape[1:], BF16),
        compiler_params=_cparams(1), name="fspec",
    )(g, gs, fc, fs, a)


def _fft2_kernel(g_ref, gs_ref, gi_ref, gis_ref, fc_ref, fs_ref, tc_ref, ts_ref, a_ref, h_ref, o_ref):
    n = FFT_N2

    def inv(jj):
        tc, ts = tc_ref[jj], ts_ref[jj]
        return (gi_ref[...] * jnp.concatenate([tc, tc], axis=0)
                + gis_ref[...] * jnp.concatenate([-ts, ts], axis=0))

    for j in range(0, FFT_KB, 2):
        b = _fwd_pair(g_ref, gs_ref, fc_ref, fs_ref, a_ref, j)
        ys = []
        for t in range(2):
            br, bi = b[2 * t * n:(2 * t + 1) * n], b[(2 * t + 1) * n:(2 * t + 2) * n]
            hr, hi = h_ref[0, j + t].astype(F32), h_ref[1, j + t].astype(F32)
            ys += [br * hr - bi * hi, br * hi + bi * hr]
        y = jnp.concatenate(ys, axis=0).astype(BF16)
        p = jnp.dot(_pair_diag(inv(j), inv(j + 1)), y, preferred_element_type=F32)
        for t in range(2):
            o_ref[0, j + t] = p[2 * t * n:(2 * t + 1) * n].astype(BF16)
            o_ref[1, j + t] = p[(2 * t + 1) * n:(2 * t + 2) * n].astype(BF16)


def _fft2(g, gs, gi, gis, fc, fs, tc, ts, a, hspec):
    B, _, k1p, _, C = a.shape
    mat = _resident((2 * FFT_N2, 2 * FFT_N2))
    row = pl.BlockSpec((FFT_KB, 2 * FFT_N2), lambda kb, b: (kb, 0))
    col = pl.BlockSpec((FFT_KB, FFT_N2, LANES), lambda kb, b: (kb, 0, 0))
    return pl.pallas_call(
        _fft2_kernel, grid=(k1p // FFT_KB, B),
        in_specs=[
            mat, mat, mat, mat, row, row, col, col,
            pl.BlockSpec((None, 2, FFT_KB, FFT_N2, C), lambda kb, b: (b, 0, kb, 0, 0)),
            pl.BlockSpec((2, FFT_KB, FFT_N2, C), lambda kb, b: (0, kb, 0, 0)),
        ],
        out_specs=pl.BlockSpec((None, 2, FFT_KB, FFT_N2, C), lambda kb, b: (b, 0, kb, 0, 0)),
        out_shape=jax.ShapeDtypeStruct(a.shape, BF16),
        compiler_params=_cparams(2), name="fft2",
    )(g, gs, gi, gis, fc, fs, tc, ts, a, hspec)


def _ifft1_kernel(w_ref, a_ref, o_ref):
    for s in range(FFT_CW // LANES):
        lanes = slice(s * LANES, (s + 1) * LANES)
        a = jnp.concatenate([a_ref[0, :, :, lanes], a_ref[1, :, :, lanes]], axis=0)
        o_ref[:, :, lanes] = jnp.einsum('nk,kmc->nmc', w_ref[...], a, preferred_element_type=F32).astype(BF16)


def _ifft1(winv, a):
    B, _, k1p, _, C = a.shape
    n1h = winv.shape[0]
    return pl.pallas_call(
        _ifft1_kernel, grid=(FFT_N2 // FFT_MG, B, C // FFT_CW),
        in_specs=[
            _resident((n1h, 2 * k1p)),
            pl.BlockSpec((None, 2, k1p, FFT_MG, FFT_CW), lambda m, b, j: (b, 0, 0, m, j)),
        ],
        out_specs=pl.BlockSpec((None, n1h, FFT_MG, FFT_CW), lambda m, b, j: (b, 0, m, j)),
        out_shape=jax.ShapeDtypeStruct((B, n1h, FFT_N2, C), BF16),
        compiler_params=_cparams(3), name="ifft1",
    )(winv, a)


def _split_dot(a, b):
    ah = a.astype(BF16)
    al = (a - ah.astype(F32)).astype(BF16)
    bh = b.astype(BF16)
    bl = (b - bh.astype(F32)).astype(BF16)
    d = functools.partial(jnp.dot, preferred_element_type=F32)
    return d(ah, bh) + d(ah, bl) + d(al, bh)


def _filt_kernel(ft_ref, w1a_ref, w1b_ref, b1_ref, q1_ref, w2_ref, b2_ref, q2_ref, w3a_ref, w3b_ref,
                 dl_ref, db_ref, o_ref, *, seq_len, tl):
    i = pl.program_id(0)
    L = seq_len
    hl = tl // 2
    pre = _split_dot(ft_ref[:hl, :], w1a_ref[...]) + _split_dot(ft_ref[hl:, :], w1b_ref[...])
    h = jnp.sin(q1_ref[...] * (pre + b1_ref[...]))
    h = jnp.sin(q2_ref[...] * (_split_dot(h, w2_ref[...]) + b2_ref[...]))
    for half, w3_ref in enumerate((w3a_ref, w3b_ref)):
        hh = _split_dot(h, w3_ref[...])
        n = i * tl + half * hl + lax.broadcasted_iota(jnp.int32, (hl, D_HYENA), 0)
        decay = jnp.exp(-(n.astype(F32) / (L - 1)) * dl_ref[...])
        rows = slice(half * hl, (half + 1) * hl)
        o_ref[0, rows, :] = (hh[:, :D_HYENA] * decay + jnp.where(n == 0, db_ref[...], 0.0)).astype(BF16)
        o_ref[1, rows, :] = jnp.where(n == 0, 0.0, hh[:, D_HYENA:] * decay).astype(BF16)


def _filt(feat, w1a, w1b, b1, q1, w2, b2, q2, w3a, w3b, dl, db, *, seq_len, tl):
    return pl.pallas_call(
        functools.partial(_filt_kernel, seq_len=seq_len, tl=tl),
        grid=(seq_len // tl,),
        in_specs=[
            pl.BlockSpec((tl, FEAT_W), lambda i: (i, 0)),
            _resident((FEAT_W, LANES)), _resident((FEAT_W, LANES)), _resident((1, LANES)), _resident((1, LANES)),
            _resident((LANES, LANES)), _resident((1, LANES)), _resident((1, LANES)),
            _resident((LANES, 2 * D_HYENA)), _resident((LANES, 2 * D_HYENA)),
            _resident((1, D_HYENA)), _resident((1, D_HYENA)),
        ],
        out_specs=pl.BlockSpec((2, tl, D_HYENA), lambda i: (0, i, 0)),
        out_shape=jax.ShapeDtypeStruct((2, seq_len, D_HYENA), BF16),
        compiler_params=_cparams(1), name="filt",
    )(feat, w1a, w1b, b1, q1, w2, b2, q2, w3a, w3b, dl, db)


FEAT_W = 40


@functools.lru_cache(maxsize=None)
def _filter_features(seq_len):
    L = seq_len
    bands = (FILTER_EMB - 1) // 2
    pos = np.arange(L, dtype=np.float64)[:, None]
    t = pos / (L - 1)
    w = 2.0 * np.pi * pos / L
    f = np.linspace(1e-4, bands - 1, bands)[None]
    pad = np.zeros((L, FEAT_W - FILTER_EMB))
    return np.concatenate([t, np.cos(f * w), -np.sin(f * w), pad], axis=1).astype(np.float32)


FF_CHUNK = 256


def _ffn_kernel(hm_ref, hp_ref, hn_ref, am_ref, ap_ref, an_ref, xm_ref, xp_ref, xn_ref,
                ym_ref, yp_ref, yn_ref, p_ref, gh_ref, wo_ref, gf_ref, wu_ref, wc_ref, bc_ref,
                wd_ref, wg_ref, wp_ref, o_ref, act_ref, *, tm, tiles_per_seq):
    i = pl.program_id(0)
    first = (i % tiles_per_seq) == 0
    last = (i % tiles_per_seq) == tiles_per_seq - 1

    def stack(m_ref, p_ref, n_ref):
        p = p_ref[BF16_ROWS - SUBLANES:, :]
        n = n_ref[:SUBLANES, :]
        p = jnp.where(first, jnp.zeros_like(p), p)
        n = jnp.where(last, jnp.zeros_like(n), n)
        return jnp.concatenate([m_ref[...], n, p], axis=0)

    h = stack(hm_ref, hp_ref, hn_ref)
    a = stack(am_ref, ap_ref, an_ref)
    hy = stack(xm_ref, xp_ref, xn_ref).astype(F32) * stack(ym_ref, yp_ref, yn_ref).astype(F32)
    hyn = _rms(hy, gh_ref[...]).astype(BF16)
    h1 = (h + jnp.dot(a, wo_ref[:D_ATTN, :], preferred_element_type=F32)
          + jnp.dot(hyn, wo_ref[D_ATTN:, :], preferred_element_type=F32))
    n2 = _rms(h1, gf_ref[...]).astype(BF16)

    def conv(c0):
        u = jnp.dot(n2, wu_ref[:, c0:c0 + FF_CHUNK], preferred_element_type=F32)
        return _conv3(u, tm, wc_ref, bc_ref, c0, c0 + FF_CHUNK)

    for cidx in range(D_FF // FF_CHUNK):
        c0 = cidx * FF_CHUNK
        ua = conv(c0)
        ug = conv(D_FF + c0)
        act_ref[:, c0:c0 + FF_CHUNK] = (ug * jax.nn.sigmoid(ug) * ua).astype(BF16)

    h2 = h1[:tm] + jnp.dot(act_ref[...], wd_ref[...], preferred_element_type=F32)
    gate = jax.nn.sigmoid(jnp.dot(h2.astype(BF16), wg_ref[...], preferred_element_type=F32))
    pp = jnp.dot(p_ref[...].astype(BF16), wp_ref[...], preferred_element_type=F32)
    o_ref[...] = h2 + gate * pp


def _ffn(h, attn, x0, y, p, gh, wo, gf, wu, wc, bc, wd, wg, wp, *, seq_len, tm, layer):
    p_tile0 = layer * (h.shape[0] // tm)
    T = h.shape[0]
    tiles_per_seq = seq_len // tm
    r16 = tm // BF16_ROWS
    nb16 = T // BF16_ROWS
    main = lambda i: (i, 0)
    prev = lambda i: (jnp.maximum(i * r16 - 1, 0), 0)
    nxt = lambda i: (jnp.minimum((i + 1) * r16, nb16 - 1), 0)

    def trio(width):
        return [pl.BlockSpec((tm, width), main), pl.BlockSpec((BF16_ROWS, width), prev),
                pl.BlockSpec((BF16_ROWS, width), nxt)]

    in_specs = (trio(D_MODEL) + trio(D_ATTN) + trio(D_HYENA) + trio(D_HYENA) + [
        pl.BlockSpec((tm, D_PLE), lambda i: (p_tile0 + i, 0)),
        _resident((1, D_HYENA)),
        _resident((D_MODEL, D_MODEL), layer),
        _resident((1, D_MODEL)),
        _resident((D_MODEL, 2 * D_FF), layer),
        _resident((3, 2 * D_FF)),
        _resident((1, 2 * D_FF)),
        _resident((D_FF, D_MODEL), layer),
        _resident((D_MODEL, D_MODEL), layer),
        _resident((D_PLE, D_MODEL), layer),
    ])
    return pl.pallas_call(
        functools.partial(_ffn_kernel, tm=tm, tiles_per_seq=tiles_per_seq),
        grid=(T // tm,), in_specs=in_specs,
        out_specs=pl.BlockSpec((tm, D_MODEL), main),
        out_shape=jax.ShapeDtypeStruct((T, D_MODEL), F32),
        scratch_shapes=[pltpu.VMEM((tm, D_FF), BF16)],
        compiler_params=_cparams(1), name="ffn",
    )(h, h, h, attn, attn, attn, x0, x0, x0, y, y, y, p, gh, wo, gf, wu, wc, bc, wd, wg, wp)


def _rope_slabs(seq_len):
    half = ROPE_DIM // 2
    inv = ROPE_THETA ** (-jnp.arange(0, ROPE_DIM, 2, dtype=F32) / ROPE_DIM)
    ang = jnp.arange(seq_len, dtype=F32)[:, None] * inv[None]
    cos, sin = jnp.cos(ang), jnp.sin(ang)
    ones = jnp.ones((seq_len, HEAD_DIM - ROPE_DIM), F32)
    zeros = jnp.zeros((seq_len, HEAD_DIM - ROPE_DIM), F32)
    zh = jnp.zeros((seq_len, half), F32)
    rep = LANES // HEAD_DIM
    rc = jnp.concatenate([cos, cos, ones] * rep, axis=1)
    rs1 = jnp.concatenate([-sin, zh, zeros] * rep, axis=1)
    rs2 = jnp.concatenate([zh, sin, zeros] * rep, axis=1)
    return rc, rs1, rs2


def _group_consts(seq_len, rope):
    n1h, k1p = _fft_sizes(seq_len)
    w1, winv, g, gs, gi, gis = _dft_consts(seq_len)
    fc, fs, tc, ts = _twiddle_tables(seq_len)
    return dict(
        seq_len=seq_len, n1h=n1h, k1p=k1p,
        w1_half=jnp.asarray(w1[:, :n1h], BF16),
        winv=jnp.asarray(winv, BF16),
        g=jnp.asarray(g, F32), gs=jnp.asarray(gs, F32), gi=jnp.asarray(gi, F32), gis=jnp.asarray(gis, F32),
        fc=fc, fs=fs, tc=tc, ts=ts, rope=rope, feat=_filter_features(seq_len),
    )


def _pair(v):
    return jnp.concatenate([v, v])[None]


def _layer_weights(i, rms_mix, w_in, q_norm, k_norm, sink, w_short, b_short, filt_w1, filt_b1, filt_freq1,
                   filt_w2, filt_b2, filt_freq2, filt_w3, hyena_bias, norm_attn_out, norm_hyena_out,
                   w_out, rms_ffn, w_up, w_ffconv, b_ffconv, w_down, w_ple_gate, w_ple_proj):
    rep_q = D_ATTN // HEAD_DIM
    rep_k = D_KV // HEAD_DIM
    hid = FILTER_HID
    w1p = jnp.zeros((FEAT_W, hid), F32).at[:FILTER_EMB].set(filt_w1[i])
    zpad = jnp.zeros((FEAT_W, hid), F32)
    zsq = jnp.zeros((hid, hid), F32)
    z3 = jnp.zeros((hid, 2 * D_HYENA), F32)
    deltas = jnp.abs(jnp.linspace(MIN_DECAY, MAX_DECAY, D_HYENA, dtype=F32))
    return dict(
        g_mix=rms_mix[i][None], w_in=w_in,
        qg=(jnp.tile(q_norm[i], rep_q) * (HEAD_DIM ** -0.5))[None], kg=jnp.tile(k_norm[i], rep_k)[None],
        sink=sink[i], ws=w_short[i], bs=b_short[i][None],
        fw1a=jnp.concatenate([w1p, zpad], axis=1), fw1b=jnp.concatenate([zpad, w1p], axis=1),
        fb1=_pair(filt_b1[i]), fq1=_pair(filt_freq1[i]),
        fw2=jnp.block([[filt_w2[i], zsq], [zsq, filt_w2[i]]]),
        fb2=_pair(filt_b2[i]), fq2=_pair(filt_freq2[i]),
        fw3a=jnp.concatenate([filt_w3[i], z3], axis=0), fw3b=jnp.concatenate([z3, filt_w3[i]], axis=0),
        deltas=deltas[None], dbias=hyena_bias[i][None],
        g_attn=norm_attn_out[i][None], g_hy=norm_hyena_out[i][None],
        w_out=w_out, g_ffn=rms_ffn[i][None], w_up=w_up,
        wc=w_ffconv[i], bc=b_ffconv[i][None], w_down=w_down,
        w_gate=w_ple_gate, w_proj=w_ple_proj,
    )


def _layer(h, p_all, layer, lw, gc, bd, batch):
    L = gc["seq_len"]
    T = batch * L
    n1h, k1p = gc["n1h"], gc["k1p"]
    rc, rs1, rs2 = gc["rope"]
    tm = min(TILE_FFN, L)
    q, k, v, x0, vv = _inproj(h, lw["g_mix"], lw["w_in"], lw["qg"], lw["kg"], bd, rc, rs1, rs2,
                                lw["ws"], lw["bs"], seq_len=L, tm=min(TILE_INPROJ, L), layer=layer)
    attn = _attn(lw["sink"], q, k, v, lw["g_attn"], seq_len=L, tq=min(TILE_ATTN, L))
    filt = _filt(gc["feat"], lw["fw1a"], lw["fw1b"], lw["fb1"], lw["fq1"], lw["fw2"], lw["fb2"], lw["fq2"],
                 lw["fw3a"], lw["fw3b"], lw["deltas"], lw["dbias"], seq_len=L, tl=min(TILE_FILT, L))
    fa = _fft1(gc["w1_half"], filt.reshape(2, n1h, FFT_N2, D_HYENA), k1p=k1p)
    hspec = _fspec(gc["g"], gc["gs"], gc["fc"], gc["fs"], fa)
    a = _fft1(gc["w1_half"], vv.reshape(batch, n1h, FFT_N2, D_HYENA), k1p=k1p)
    a = _fft2(gc["g"], gc["gs"], gc["gi"], gc["gis"], gc["fc"], gc["fs"], gc["tc"], gc["ts"], a, hspec)
    y = _ifft1(gc["winv"], a).reshape(T, D_HYENA)
    return _ffn(h, attn, x0, y, p_all, lw["g_hy"], lw["w_out"], lw["g_ffn"], lw["w_up"], lw["wc"], lw["bc"],
                lw["w_down"], lw["w_gate"], lw["w_proj"], seq_len=L, tm=tm, layer=layer)


def _trunk(x, p, weights, rope):
    batch, L, _ = x.shape
    gc = _group_consts(L, rope)
    bd = jnp.asarray(np.kron(np.eye(MXU_W // HEAD_DIM), np.full((HEAD_DIM, HEAD_DIM), 1.0 / HEAD_DIM)), BF16)
    h = x.reshape(batch * L, D_MODEL)
    p_all = p.reshape(DEPTH * batch * L, D_PLE)
    for i in range(DEPTH):
        lw = _layer_weights(i, *weights)
        h = _layer(h, p_all, i, lw, gc, bd, batch)
    return h.reshape(batch, L, D_MODEL)


def kernel(x_prompt, x_sample, p_prompt, p_sample, rms_mix, w_in, q_norm, k_norm, sink, w_short, b_short, filt_w1, filt_b1, filt_freq1, filt_w2, filt_b2, filt_freq2, filt_w3, hyena_bias, norm_attn_out, norm_hyena_out, w_out, rms_ffn, w_up, w_ffconv, b_ffconv, w_down, w_ple_gate, w_ple_proj):
    w_in = jnp.concatenate([w_in[..., D_QKV:], w_in[..., :D_QKV]], axis=-1)
    w_in, w_out, w_up, w_down, w_ple_gate, w_ple_proj = (
        w.astype(BF16) for w in (w_in, w_out, w_up, w_down, w_ple_gate, w_ple_proj))
    weights = (rms_mix, w_in, q_norm, k_norm, sink, w_short, b_short, filt_w1, filt_b1, filt_freq1,
               filt_w2, filt_b2, filt_freq2, filt_w3, hyena_bias, norm_attn_out, norm_hyena_out,
               w_out, rms_ffn, w_up, w_ffconv, b_ffconv, w_down, w_ple_gate, w_ple_proj)
    rope = _rope_slabs(max(x_prompt.shape[1], x_sample.shape[1]))
    y_prompt = _trunk(x_prompt, p_prompt, weights, rope)
    y_sample = _trunk(x_sample, p_sample, weights, rope)
    return (y_prompt, y_sample)
```

```python
import functools
import math

import numpy as np
import jax
import jax.numpy as jnp
from jax import lax
from jax.experimental import pallas as pl
from jax.experimental.pallas import tpu as pltpu

F32 = jnp.float32
BF16 = jnp.bfloat16

D_MODEL = 1024
DEPTH = 4
N_Q_HEADS = 8
N_KV_HEADS = 2
HEAD_DIM = 64
D_ATTN = N_Q_HEADS * HEAD_DIM
D_KV = N_KV_HEADS * HEAD_DIM
D_QKV = D_ATTN + 2 * D_KV
WINDOW = 128
BLOCK = 128
ROPE_THETA = 500000.0
ROPE_DIM = HEAD_DIM // 4
D_HYENA = 512
FILTER_EMB = 33
FILTER_HID = 64
FAST_DECAY_PCT = 0.3
SLOW_DECAY_PCT = 1.5
DECAY_TARGET = 1e-2
MAX_DECAY = math.log(DECAY_TARGET) / FAST_DECAY_PCT
MIN_DECAY = math.log(DECAY_TARGET) / SLOW_DECAY_PCT
D_IN = D_QKV + 3 * D_HYENA
D_FF = 2816
D_PLE = 256
EPS = 1e-6
MASKED = -1e30

LANES = 128
SUBLANES = 8
BF16_ROWS = 16
MXU_W = 256
Q_SLABS = N_Q_HEADS * LANES
COL_HYENA = 0
COL_Q = 3 * D_HYENA
COL_K = COL_Q + D_ATTN
COL_V = COL_K + D_KV
FFT_N2 = 64
FFT_MG = 32
FFT_CW = 128
FFT_KB = 24
TILE_INPROJ = 1024
TILE_ATTN = 2048
TILE_FFN = 512
TILE_FILT = 2048
VMEM_LIMIT = 56 * 1024 * 1024


def _cparams(n_axes):
    return pltpu.CompilerParams(dimension_semantics=("arbitrary",) * n_axes,
                                vmem_limit_bytes=VMEM_LIMIT)


def _resident(shape, layer=None):
    if layer is None:
        return pl.BlockSpec(shape, lambda *_: (0,) * len(shape), pipeline_mode=pl.Buffered(1))
    return pl.BlockSpec((None,) + tuple(shape), lambda *_: (layer,) + (0,) * len(shape),
                        pipeline_mode=pl.Buffered(1))


def _rms(x, g):
    ms = jnp.mean(x * x, axis=-1, keepdims=True)
    return x * lax.rsqrt(ms + EPS) * g


def _conv3(u, tm, w_ref, b_ref, c0, c1):
    rows = u.shape[0]
    up = pltpu.roll(u, 1, axis=0)[:tm]
    un = pltpu.roll(u, rows - 1, axis=0)[:tm]
    return up * w_ref[0:1, c0:c1] + u[:tm] * w_ref[1:2, c0:c1] + un * w_ref[2:3, c0:c1] + b_ref[:, c0:c1]


def _inproj_kernel(xm_ref, xp_ref, xn_ref, g_ref, w_ref, qg_ref, kg_ref, bd_ref,
                   rc_ref, rs1_ref, rs2_ref, ws_ref, bs_ref,
                   q_ref, k_ref, v_ref, x0_ref, vv_ref, *, tm, tiles_per_seq):
    i = pl.program_id(0)
    first = (i % tiles_per_seq) == 0
    last = (i % tiles_per_seq) == tiles_per_seq - 1
    xp = jnp.where(first, 0.0, xp_ref[...])
    xn = jnp.where(last, 0.0, xn_ref[...])
    x = jnp.concatenate([xm_ref[...], xn, xp], axis=0)
    n = _rms(x, g_ref[...]).astype(BF16)
    z = jnp.dot(n, w_ref[...], preferred_element_type=F32)

    bd = bd_ref[...]
    rc, rs1, rs2 = rc_ref[...], rs1_ref[...], rs2_ref[...]

    def norm_rope(zs, gain, bds):
        sq = (zs * zs).astype(BF16)
        step = bds.shape[0]
        ms = jnp.concatenate([jnp.dot(sq[:, c:c + step], bds, preferred_element_type=F32)
                              for c in range(0, zs.shape[1], step)], axis=1)
        y = zs * lax.rsqrt(ms + EPS) * gain
        outs = []
        for s in range(y.shape[1] // LANES):
            ys = y[:, s * LANES:(s + 1) * LANES]
            outs.append(ys * rc + pltpu.roll(ys, LANES - ROPE_DIM // 2, axis=1) * rs1
                        + pltpu.roll(ys, ROPE_DIM // 2, axis=1) * rs2)
        return outs

    u = _conv3(z[:, COL_HYENA:COL_HYENA + 3 * D_HYENA], tm, ws_ref, bs_ref, 0, 3 * D_HYENA)
    x0_ref[...] = u[:, :D_HYENA].astype(BF16)
    vv_ref[...] = (u[:, D_HYENA:2 * D_HYENA] * u[:, 2 * D_HYENA:]).astype(BF16)

    qs = norm_rope(z[:tm, COL_Q:COL_Q + D_ATTN], qg_ref[...], bd)
    low = lax.broadcasted_iota(jnp.int32, (tm, LANES), 1) < HEAD_DIM
    heads_per_kv = N_Q_HEADS // N_KV_HEADS
    for s, qv in enumerate(qs):
        qr = pltpu.roll(qv, HEAD_DIM, axis=1)
        for half in range(2):
            h = 2 * s + half
            src = qv if half == h // heads_per_kv else qr
            keep = low if h // heads_per_kv == 0 else jnp.logical_not(low)
            q_ref[:, h * LANES:(h + 1) * LANES] = jnp.where(keep, src, 0.0).astype(BF16)
    k_ref[...] = norm_rope(z[:tm, COL_K:COL_K + D_KV], kg_ref[...], bd[:D_KV, :D_KV])[0].astype(BF16)
    v_ref[...] = z[:tm, COL_V:COL_V + D_KV].astype(BF16)


def _inproj(x, g, w, qg, kg, bd, rc, rs1, rs2, ws, bs, *, seq_len, tm, layer):
    T = x.shape[0]
    tiles_per_seq = seq_len // tm
    r8 = tm // SUBLANES
    nb8 = T // SUBLANES
    rope_spec = pl.BlockSpec((tm, LANES), lambda i: (i % tiles_per_seq, 0))
    in_specs = [
        pl.BlockSpec((tm, D_MODEL), lambda i: (i, 0)),
        pl.BlockSpec((SUBLANES, D_MODEL), lambda i: (jnp.maximum(i * r8 - 1, 0), 0)),
        pl.BlockSpec((SUBLANES, D_MODEL), lambda i: (jnp.minimum((i + 1) * r8, nb8 - 1), 0)),
        _resident((1, D_MODEL)),
        _resident((D_MODEL, D_IN), layer),
        _resident((1, D_ATTN)),
        _resident((1, D_KV)),
        _resident((MXU_W, MXU_W)),
        rope_spec, rope_spec, rope_spec,
        _resident((3, 3 * D_HYENA)),
        _resident((1, 3 * D_HYENA)),
    ]
    widths = (Q_SLABS, D_KV, D_KV, D_HYENA, D_HYENA)
    out_specs = [pl.BlockSpec((tm, wd), lambda i: (i, 0)) for wd in widths]
    out_shape = [jax.ShapeDtypeStruct((T, wd), BF16) for wd in widths]
    return pl.pallas_call(
        functools.partial(_inproj_kernel, tm=tm, tiles_per_seq=tiles_per_seq),
        grid=(T // tm,), in_specs=in_specs, out_specs=out_specs, out_shape=out_shape,
        compiler_params=_cparams(1), name="inproj",
    )(x, x, x, g, w, qg, kg, bd, rc, rs1, rs2, ws, bs)


def _attn_kernel(sink_ref, q_ref, km_ref, kp_ref, kn_ref, vm_ref, vp_ref, vn_ref, g_ref,
                 o_ref, kbuf, vbuf, *, tq, tiles_per_seq):
    i = pl.program_id(0)
    first = (i % tiles_per_seq) == 0
    last = (i % tiles_per_seq) == tiles_per_seq - 1
    nblk = tq // BLOCK
    H = N_Q_HEADS
    G = N_Q_HEADS // N_KV_HEADS
    kbuf[0:BLOCK] = kp_ref[...]
    kbuf[BLOCK:BLOCK + tq] = km_ref[...]
    kbuf[BLOCK + tq:] = kn_ref[...]
    vbuf[0:BLOCK, :D_KV] = vp_ref[...]
    vbuf[BLOCK:BLOCK + tq, :D_KV] = vm_ref[...]
    vbuf[BLOCK + tq:, :D_KV] = vn_ref[...]
    vbuf[:, D_KV:] = jnp.ones((tq + 2 * BLOCK, LANES), BF16)
    r = lax.broadcasted_iota(jnp.int32, (BLOCK, 3 * BLOCK), 0)
    c = lax.broadcasted_iota(jnp.int32, (BLOCK, 3 * BLOCK), 1)
    d = c - BLOCK - r
    band = jnp.where((d >= -WINDOW) & (d <= WINDOW), 0.0, MASKED)
    low = lax.broadcasted_iota(jnp.int32, (BLOCK, LANES), 1) < HEAD_DIM
    rid = lax.broadcasted_iota(jnp.int32, (H * BLOCK, 1), 0)
    sk = jnp.zeros((H * BLOCK, 1), F32)
    for h in range(H):
        sk = jnp.where((rid >= h * BLOCK) & (rid < (h + 1) * BLOCK), sink_ref[h], sk)
    for j in range(nblk):
        bias = band
        if j == 0:
            bias = jnp.where(first & (c < BLOCK), MASKED, bias)
        if j == nblk - 1:
            bias = jnp.where(last & (c >= 2 * BLOCK), MASKED, bias)
        kb = kbuf[j * BLOCK:(j + 3) * BLOCK, :]
        vb = vbuf[j * BLOCK:(j + 3) * BLOCK, :]
        ql = jnp.concatenate([q_ref[j * BLOCK:(j + 1) * BLOCK, h * LANES:(h + 1) * LANES] for h in range(H)],
                             axis=0)
        s = lax.dot_general(ql, kb, (((1,), (1,)), ((), ())), preferred_element_type=F32)
        s = (s.reshape(H, BLOCK, 3 * BLOCK) + bias[None]).reshape(H * BLOCK, 3 * BLOCK)
        m = jnp.maximum(jnp.max(s, axis=-1, keepdims=True), sk)
        e = jnp.exp((s - m).astype(BF16))
        out = jnp.dot(e, vb, preferred_element_type=F32)
        res = out[:, :LANES] / (out[:, LANES:] + jnp.exp(sk - m))
        slabs = []
        for p in range(H // 2):
            ra = res[(2 * p) * BLOCK:(2 * p + 1) * BLOCK]
            rb = res[(2 * p + 1) * BLOCK:(2 * p + 2) * BLOCK]
            if (2 * p) // G == 0:
                slabs.append(jnp.where(low, ra, pltpu.roll(rb, HEAD_DIM, axis=1)))
            else:
                slabs.append(jnp.where(low, pltpu.roll(ra, HEAD_DIM, axis=1), rb))
        o = jnp.concatenate(slabs, axis=1)
        o_ref[j * BLOCK:(j + 1) * BLOCK, :] = _rms(o, g_ref[...]).astype(BF16)


def _attn(sink, q, k, v, g, *, seq_len, tq):
    T = q.shape[0]
    tiles_per_seq = seq_len // tq
    rb = tq // BLOCK
    nbb = T // BLOCK
    main = lambda i, s: (i, 0)
    prev = lambda i, s: (jnp.maximum(i * rb - 1, 0), 0)
    nxt = lambda i, s: (jnp.minimum((i + 1) * rb, nbb - 1), 0)
    kv_trio = [pl.BlockSpec((tq, D_KV), main), pl.BlockSpec((BLOCK, D_KV), prev), pl.BlockSpec((BLOCK, D_KV), nxt)]
    grid_spec = pltpu.PrefetchScalarGridSpec(
        num_scalar_prefetch=1, grid=(T // tq,),
        in_specs=[pl.BlockSpec((tq, Q_SLABS), main)] + kv_trio + kv_trio + [
            pl.BlockSpec((1, D_ATTN), lambda i, s: (0, 0))],
        out_specs=pl.BlockSpec((tq, D_ATTN), main),
        scratch_shapes=[pltpu.VMEM((tq + 2 * BLOCK, D_KV), BF16), pltpu.VMEM((tq + 2 * BLOCK, D_KV + LANES), BF16)],
    )
    return pl.pallas_call(
        functools.partial(_attn_kernel, tq=tq, tiles_per_seq=tiles_per_seq),
        grid_spec=grid_spec, out_shape=jax.ShapeDtypeStruct((T, D_ATTN), BF16),
        compiler_params=_cparams(1), name="attn",
    )(sink, q, k, k, k, v, v, v, g)


def _fft_sizes(seq_len):
    n1h = seq_len // FFT_N2
    k1p = -(-(n1h + 1) // FFT_KB) * FFT_KB
    return n1h, k1p


@functools.lru_cache(maxsize=None)
def _dft_consts(seq_len):
    n1h, k1p = _fft_sizes(seq_len)
    n1 = 2 * n1h
    k = np.arange(k1p)[:, None]
    valid = (k <= n1h)
    th = 2.0 * np.pi * ((k * np.arange(n1)[None, :]) % n1) / n1
    w1 = np.concatenate([np.cos(th) * valid, -np.sin(th) * valid], axis=0)
    scale = np.where((k == 0) | (k == n1h), 1.0, 2.0) * valid / n1
    thh = th[:, :n1h]
    winv = np.concatenate([np.cos(thh) * scale, -np.sin(thh) * scale], axis=0).T
    a = 2.0 * np.pi * ((np.arange(FFT_N2)[:, None] * np.arange(FFT_N2)[None, :]) % FFT_N2) / FFT_N2
    cm, sm = np.cos(a), np.sin(a)
    g = np.block([[cm, sm], [-sm, cm]])
    gi = np.block([[cm, -sm], [sm, cm]]) / FFT_N2
    gs = np.concatenate([g[:, FFT_N2:], g[:, :FFT_N2]], axis=1)
    gis = np.concatenate([gi[FFT_N2:], gi[:FFT_N2]], axis=0)
    return w1, winv, g, gs, gi, gis


def _twiddle_tables(seq_len):
    n1h, k1p = _fft_sizes(seq_len)
    n = 2 * seq_len
    k = jnp.arange(k1p, dtype=jnp.int32)[:, None]
    m = jnp.arange(FFT_N2, dtype=jnp.int32)[None, :]
    ph = ((k * m) % n).astype(F32) * (2.0 * math.pi / n)
    cs, sn = jnp.cos(ph), jnp.sin(ph)
    fc = jnp.concatenate([cs, cs], axis=1)
    fs = jnp.concatenate([-sn, sn], axis=1)
    tc = jnp.broadcast_to(cs[:, :, None], (k1p, FFT_N2, LANES))
    ts = jnp.broadcast_to(sn[:, :, None], (k1p, FFT_N2, LANES))
    return fc, fs, tc, ts


def _fft1_kernel(w_ref, x_ref, o_ref, *, k1p):
    for s in range(FFT_CW // LANES):
        lanes = slice(s * LANES, (s + 1) * LANES)
        a = jnp.einsum('kn,nmc->kmc', w_ref[...], x_ref[:, :, lanes], preferred_element_type=F32)
        o_ref[0, :, :, lanes] = a[:k1p].astype(BF16)
        o_ref[1, :, :, lanes] = a[k1p:].astype(BF16)


def _fft1(w1, x, *, k1p):
    B, n1in, _, C = x.shape
    grid = (FFT_N2 // FFT_MG, B, C // FFT_CW)
    return pl.pallas_call(
        functools.partial(_fft1_kernel, k1p=k1p),
        grid=grid,
        in_specs=[
            _resident((2 * k1p, n1in)),
            pl.BlockSpec((None, n1in, FFT_MG, FFT_CW), lambda m, b, j: (b, 0, m, j)),
        ],
        out_specs=pl.BlockSpec((None, 2, k1p, FFT_MG, FFT_CW), lambda m, b, j: (b, 0, 0, m, j)),
        out_shape=jax.ShapeDtypeStruct((B, 2, k1p, FFT_N2, C), BF16),
        compiler_params=_cparams(3), name="fft1",
    )(w1, x)


def _pair_diag(a, b):
    z = jnp.zeros_like(a)
    return jnp.concatenate([jnp.concatenate([a, z], axis=1), jnp.concatenate([z, b], axis=1)], axis=0).astype(BF16)


def _fwd_pair(g_ref, gs_ref, fc_ref, fs_ref, a_ref, j):
    def mat(jj):
        return g_ref[...] * fc_ref[jj:jj + 1, :] + gs_ref[...] * fs_ref[jj:jj + 1, :]
    x = jnp.concatenate([a_ref[0, j], a_ref[1, j], a_ref[0, j + 1], a_ref[1, j + 1]], axis=0)
    return jnp.dot(_pair_diag(mat(j), mat(j + 1)), x, preferred_element_type=F32)


def _fspec_kernel(g_ref, gs_ref, fc_ref, fs_ref, a_ref, o_ref):
    n = FFT_N2
    for j in range(0, FFT_KB, 2):
        bf = _fwd_pair(g_ref, gs_ref, fc_ref, fs_ref, a_ref.at[0], j)
        bb = _fwd_pair(g_ref, gs_ref, fc_ref, fs_ref, a_ref.at[1], j)
        for t in range(2):
            re = slice(2 * t * n, (2 * t + 1) * n)
            im = slice((2 * t + 1) * n, (2 * t + 2) * n)
            o_ref[0, j + t] = (bf[re] + bb[re]).astype(BF16)
            o_ref[1, j + t] = (bf[im] - bb[im]).astype(BF16)


def _fspec(g, gs, fc, fs, a):
    _, _, _, k1p, _, C = a.shape
    row = pl.BlockSpec((FFT_KB, 2 * FFT_N2), lambda l, kb: (kb, 0))
    return pl.pallas_call(
        _fspec_kernel, grid=(DEPTH, k1p // FFT_KB),
        in_specs=[_resident((2 * FFT_N2, 2 * FFT_N2)), _resident((2 * FFT_N2, 2 * FFT_N2)), row, row,
                  pl.BlockSpec((None, 2, 2, FFT_KB, FFT_N2, C), lambda l, kb: (l, 0, 0, kb, 0, 0))],
        out_specs=pl.BlockSpec((None, 2, FFT_KB, FFT_N2, C), lambda l, kb: (l, 0, kb, 0, 0)),
        out_shape=jax.ShapeDtypeStruct((DEPTH,) + a.shape[2:], BF16),
        compiler_params=_cparams(2), name="fspec",
    )(g, gs, fc, fs, a)


def _fft2_kernel(g_ref, gs_ref, gi_ref, gis_ref, fc_ref, fs_ref, tc_ref, ts_ref, a_ref, h_ref, o_ref):
    n = FFT_N2

    def inv(jj):
        tc, ts = tc_ref[jj], ts_ref[jj]
        return (gi_ref[...] * jnp.concatenate([tc, tc], axis=0)
                + gis_ref[...] * jnp.concatenate([-ts, ts], axis=0))

    for j in range(0, FFT_KB, 2):
        b = _fwd_pair(g_ref, gs_ref, fc_ref, fs_ref, a_ref, j)
        ys = []
        for t in range(2):
            br, bi = b[2 * t * n:(2 * t + 1) * n], b[(2 * t + 1) * n:(2 * t + 2) * n]
            hr, hi = h_ref[0, j + t].astype(F32), h_ref[1, j + t].astype(F32)
            ys += [br * hr - bi * hi, br * hi + bi * hr]
        y = jnp.concatenate(ys, axis=0).astype(BF16)
        p = jnp.dot(_pair_diag(inv(j), inv(j + 1)), y, preferred_element_type=F32)
        for t in range(2):
            o_ref[0, j + t] = p[2 * t * n:(2 * t + 1) * n].astype(BF16)
            o_ref[1, j + t] = p[(2 * t + 1) * n:(2 * t + 2) * n].astype(BF16)


def _fft2(g, gs, gi, gis, fc, fs, tc, ts, a, hspec, *, layer):
    B, _, k1p, _, C = a.shape
    mat = _resident((2 * FFT_N2, 2 * FFT_N2))
    row = pl.BlockSpec((FFT_KB, 2 * FFT_N2), lambda kb, b: (kb, 0))
    col = pl.BlockSpec((FFT_KB, FFT_N2, LANES), lambda kb, b: (kb, 0, 0))
    return pl.pallas_call(
        _fft2_kernel, grid=(k1p // FFT_KB, B),
        in_specs=[
            mat, mat, mat, mat, row, row, col, col,
            pl.BlockSpec((None, 2, FFT_KB, FFT_N2, C), lambda kb, b: (b, 0, kb, 0, 0)),
            pl.BlockSpec((None, 2, FFT_KB, FFT_N2, C), lambda kb, b: (layer, 0, kb, 0, 0)),
        ],
        out_specs=pl.BlockSpec((None, 2, FFT_KB, FFT_N2, C), lambda kb, b: (b, 0, kb, 0, 0)),
        out_shape=jax.ShapeDtypeStruct(a.shape, BF16),
        compiler_params=_cparams(2), name="fft2",
    )(g, gs, gi, gis, fc, fs, tc, ts, a, hspec)


def _ifft1_kernel(w_ref, a_ref, o_ref):
    for s in range(FFT_CW // LANES):
        lanes = slice(s * LANES, (s + 1) * LANES)
        a = jnp.concatenate([a_ref[0, :, :, lanes], a_ref[1, :, :, lanes]], axis=0)
        o_ref[:, :, lanes] = jnp.einsum('nk,kmc->nmc', w_ref[...], a, preferred_element_type=F32).astype(BF16)


def _ifft1(winv, a):
    B, _, k1p, _, C = a.shape
    n1h = winv.shape[0]
    return pl.pallas_call(
        _ifft1_kernel, grid=(FFT_N2 // FFT_MG, B, C // FFT_CW),
        in_specs=[
            _resident((n1h, 2 * k1p)),
            pl.BlockSpec((None, 2, k1p, FFT_MG, FFT_CW), lambda m, b, j: (b, 0, 0, m, j)),
        ],
        out_specs=pl.BlockSpec((None, n1h, FFT_MG, FFT_CW), lambda m, b, j: (b, 0, m, j)),
        out_shape=jax.ShapeDtypeStruct((B, n1h, FFT_N2, C), BF16),
        compiler_params=_cparams(3), name="ifft1",
    )(winv, a)


def _split_dot(a, b):
    ah = a.astype(BF16)
    al = (a - ah.astype(F32)).astype(BF16)
    bh = b.astype(BF16)
    bl = (b - bh.astype(F32)).astype(BF16)
    d = functools.partial(jnp.dot, preferred_element_type=F32)
    return d(ah, bh) + d(ah, bl) + d(al, bh)


def _filt_kernel(ft_ref, w1a_ref, w1b_ref, b1_ref, q1_ref, w2_ref, b2_ref, q2_ref, w3a_ref, w3b_ref,
                 dl_ref, db_ref, o_ref, *, seq_len, tl):
    i = pl.program_id(1)
    L = seq_len
    hl = tl // 2
    pre = _split_dot(ft_ref[:hl, :], w1a_ref[...]) + _split_dot(ft_ref[hl:, :], w1b_ref[...])
    h = jnp.sin(q1_ref[...] * (pre + b1_ref[...]))
    h = jnp.sin(q2_ref[...] * (_split_dot(h, w2_ref[...]) + b2_ref[...]))
    for half, w3_ref in enumerate((w3a_ref, w3b_ref)):
        hh = _split_dot(h, w3_ref[...])
        n = i * tl + half * hl + lax.broadcasted_iota(jnp.int32, (hl, D_HYENA), 0)
        decay = jnp.exp(-(n.astype(F32) / (L - 1)) * dl_ref[...])
        rows = slice(half * hl, (half + 1) * hl)
        o_ref[0, rows, :] = (hh[:, :D_HYENA] * decay + jnp.where(n == 0, db_ref[...], 0.0)).astype(BF16)
        o_ref[1, rows, :] = jnp.where(n == 0, 0.0, hh[:, D_HYENA:] * decay).astype(BF16)


def _filt(feat, w1a, w1b, b1, q1, w2, b2, q2, w3a, w3b, dl, db, *, seq_len, tl):
    def per_layer(*shape):
        return pl.BlockSpec((None,) + shape, lambda l, i: (l,) + (0,) * len(shape))

    return pl.pallas_call(
        functools.partial(_filt_kernel, seq_len=seq_len, tl=tl),
        grid=(DEPTH, seq_len // tl),
        in_specs=[
            pl.BlockSpec((tl, FEAT_W), lambda l, i: (i, 0)),
            per_layer(FEAT_W, LANES), per_layer(FEAT_W, LANES), per_layer(1, LANES), per_layer(1, LANES),
            per_layer(LANES, LANES), per_layer(1, LANES), per_layer(1, LANES),
            per_layer(LANES, 2 * D_HYENA), per_layer(LANES, 2 * D_HYENA),
            _resident((1, D_HYENA)), per_layer(1, D_HYENA),
        ],
        out_specs=pl.BlockSpec((None, 2, tl, D_HYENA), lambda l, i: (l, 0, i, 0)),
        out_shape=jax.ShapeDtypeStruct((DEPTH, 2, seq_len, D_HYENA), BF16),
        compiler_params=_cparams(2), name="filt",
    )(feat, w1a, w1b, b1, q1, w2, b2, q2, w3a, w3b, dl, db)


FEAT_W = 40


@functools.lru_cache(maxsize=None)
def _filter_features(seq_len):
    L = seq_len
    bands = (FILTER_EMB - 1) // 2
    pos = np.arange(L, dtype=np.float64)[:, None]
    t = pos / (L - 1)
    w = 2.0 * np.pi * pos / L
    f = np.linspace(1e-4, bands - 1, bands)[None]
    pad = np.zeros((L, FEAT_W - FILTER_EMB))
    return np.concatenate([t, np.cos(f * w), -np.sin(f * w), pad], axis=1).astype(np.float32)


FF_CHUNK = 256


def _ffn_kernel(hm_ref, hp_ref, hn_ref, am_ref, ap_ref, an_ref, xm_ref, xp_ref, xn_ref,
                ym_ref, yp_ref, yn_ref, p_ref, gh_ref, wo_ref, gf_ref, wu_ref, wc_ref, bc_ref,
                wd_ref, wg_ref, wp_ref, o_ref, act_ref, *, tm, tiles_per_seq):
    i = pl.program_id(0)
    first = (i % tiles_per_seq) == 0
    last = (i % tiles_per_seq) == tiles_per_seq - 1

    def stack(m_ref, p_ref, n_ref):
        p = p_ref[BF16_ROWS - SUBLANES:, :]
        n = n_ref[:SUBLANES, :]
        p = jnp.where(first, jnp.zeros_like(p), p)
        n = jnp.where(last, jnp.zeros_like(n), n)
        return jnp.concatenate([m_ref[...], n, p], axis=0)

    h = stack(hm_ref, hp_ref, hn_ref)
    a = stack(am_ref, ap_ref, an_ref)
    hy = stack(xm_ref, xp_ref, xn_ref).astype(F32) * stack(ym_ref, yp_ref, yn_ref).astype(F32)
    hyn = _rms(hy, gh_ref[...]).astype(BF16)
    h1 = (h + jnp.dot(a, wo_ref[:D_ATTN, :], preferred_element_type=F32)
          + jnp.dot(hyn, wo_ref[D_ATTN:, :], preferred_element_type=F32))
    n2 = _rms(h1, gf_ref[...]).astype(BF16)

    def conv(c0):
        u = jnp.dot(n2, wu_ref[:, c0:c0 + FF_CHUNK], preferred_element_type=F32)
        return _conv3(u, tm, wc_ref, bc_ref, c0, c0 + FF_CHUNK)

    for cidx in range(D_FF // FF_CHUNK):
        c0 = cidx * FF_CHUNK
        ua = conv(c0)
        ug = conv(D_FF + c0)
        act_ref[:, c0:c0 + FF_CHUNK] = (ug * jax.nn.sigmoid(ug) * ua).astype(BF16)

    h2 = h1[:tm] + jnp.dot(act_ref[...], wd_ref[...], preferred_element_type=F32)
    gate = jax.nn.sigmoid(jnp.dot(h2.astype(BF16), wg_ref[...], preferred_element_type=F32))
    pp = jnp.dot(p_ref[...].astype(BF16), wp_ref[...], preferred_element_type=F32)
    o_ref[...] = h2 + gate * pp


def _ffn(h, attn, x0, y, p, gh, wo, gf, wu, wc, bc, wd, wg, wp, *, seq_len, tm, layer):
    p_tile0 = layer * (h.shape[0] // tm)
    T = h.shape[0]
    tiles_per_seq = seq_len // tm
    r16 = tm // BF16_ROWS
    nb16 = T // BF16_ROWS
    main = lambda i: (i, 0)
    prev = lambda i: (jnp.maximum(i * r16 - 1, 0), 0)
    nxt = lambda i: (jnp.minimum((i + 1) * r16, nb16 - 1), 0)

    def trio(width):
        return [pl.BlockSpec((tm, width), main), pl.BlockSpec((BF16_ROWS, width), prev),
                pl.BlockSpec((BF16_ROWS, width), nxt)]

    in_specs = (trio(D_MODEL) + trio(D_ATTN) + trio(D_HYENA) + trio(D_HYENA) + [
        pl.BlockSpec((tm, D_PLE), lambda i: (p_tile0 + i, 0)),
        _resident((1, D_HYENA)),
        _resident((D_MODEL, D_MODEL), layer),
        _resident((1, D_MODEL)),
        _resident((D_MODEL, 2 * D_FF), layer),
        _resident((3, 2 * D_FF)),
        _resident((1, 2 * D_FF)),
        _resident((D_FF, D_MODEL), layer),
        _resident((D_MODEL, D_MODEL), layer),
        _resident((D_PLE, D_MODEL), layer),
    ])
    return pl.pallas_call(
        functools.partial(_ffn_kernel, tm=tm, tiles_per_seq=tiles_per_seq),
        grid=(T // tm,), in_specs=in_specs,
        out_specs=pl.BlockSpec((tm, D_MODEL), main),
        out_shape=jax.ShapeDtypeStruct((T, D_MODEL), F32),
        scratch_shapes=[pltpu.VMEM((tm, D_FF), BF16)],
        compiler_params=_cparams(1), name="ffn",
    )(h, h, h, attn, attn, attn, x0, x0, x0, y, y, y, p, gh, wo, gf, wu, wc, bc, wd, wg, wp)


def _rope_slabs(seq_len):
    half = ROPE_DIM // 2
    inv = ROPE_THETA ** (-jnp.arange(0, ROPE_DIM, 2, dtype=F32) / ROPE_DIM)
    ang = jnp.arange(seq_len, dtype=F32)[:, None] * inv[None]
    cos, sin = jnp.cos(ang), jnp.sin(ang)
    ones = jnp.ones((seq_len, HEAD_DIM - ROPE_DIM), F32)
    zeros = jnp.zeros((seq_len, HEAD_DIM - ROPE_DIM), F32)
    zh = jnp.zeros((seq_len, half), F32)
    rc = jnp.concatenate([cos, cos, ones], axis=1)
    rs1 = jnp.concatenate([-sin, zh, zeros], axis=1)
    rs2 = jnp.concatenate([zh, sin, zeros], axis=1)
    rep = LANES // HEAD_DIM
    return tuple(jnp.tile(a, (1, rep)) for a in (rc, rs1, rs2))


def _group_consts(seq_len):
    n1h, k1p = _fft_sizes(seq_len)
    w1, winv, g, gs, gi, gis = _dft_consts(seq_len)
    fc, fs, tc, ts = _twiddle_tables(seq_len)
    return dict(
        seq_len=seq_len, n1h=n1h, k1p=k1p,
        w1_half=jnp.asarray(w1[:, :n1h], BF16),
        winv=jnp.asarray(winv, BF16),
        g=jnp.asarray(g, F32), gs=jnp.asarray(gs, F32), gi=jnp.asarray(gi, F32), gis=jnp.asarray(gis, F32),
        fc=fc, fs=fs, tc=tc, ts=ts, rope=_rope_slabs(seq_len), feat=_filter_features(seq_len),
    )


def _pair(v):
    return jnp.concatenate([v, v])[None]


def _layer_weights(i, rms_mix, w_in, q_norm, k_norm, sink, w_short, b_short, filt_w1, filt_b1, filt_freq1,
                   filt_w2, filt_b2, filt_freq2, filt_w3, hyena_bias, norm_attn_out, norm_hyena_out,
                   w_out, rms_ffn, w_up, w_ffconv, b_ffconv, w_down, w_ple_gate, w_ple_proj):
    rep_q = D_ATTN // HEAD_DIM
    rep_k = D_KV // HEAD_DIM
    hid = FILTER_HID
    w1p = jnp.zeros((FEAT_W, hid), F32).at[:FILTER_EMB].set(filt_w1[i])
    zpad = jnp.zeros((FEAT_W, hid), F32)
    zsq = jnp.zeros((hid, hid), F32)
    z3 = jnp.zeros((hid, 2 * D_HYENA), F32)
    deltas = jnp.abs(jnp.linspace(MIN_DECAY, MAX_DECAY, D_HYENA, dtype=F32))
    return dict(
        g_mix=rms_mix[i][None], w_in=w_in,
        qg=(jnp.tile(q_norm[i], rep_q) * (HEAD_DIM ** -0.5))[None], kg=jnp.tile(k_norm[i], rep_k)[None],
        sink=sink[i], ws=w_short[i], bs=b_short[i][None],
        fw1a=jnp.concatenate([w1p, zpad], axis=1), fw1b=jnp.concatenate([zpad, w1p], axis=1),
        fb1=_pair(filt_b1[i]), fq1=_pair(filt_freq1[i]),
        fw2=jnp.block([[filt_w2[i], zsq], [zsq, filt_w2[i]]]),
        fb2=_pair(filt_b2[i]), fq2=_pair(filt_freq2[i]),
        fw3a=jnp.concatenate([filt_w3[i], z3], axis=0), fw3b=jnp.concatenate([z3, filt_w3[i]], axis=0),
        deltas=deltas[None], dbias=hyena_bias[i][None],
        g_attn=norm_attn_out[i][None], g_hy=norm_hyena_out[i][None],
        w_out=w_out, g_ffn=rms_ffn[i][None], w_up=w_up,
        wc=w_ffconv[i], bc=b_ffconv[i][None], w_down=w_down,
        w_gate=w_ple_gate, w_proj=w_ple_proj,
    )


_FILTER_KEYS = ("fw1a", "fw1b", "fb1", "fq1", "fw2", "fb2", "fq2", "fw3a", "fw3b", "dbias")


def _filter_spectra(lws, gc):
    L = gc["seq_len"]
    n1h, k1p = gc["n1h"], gc["k1p"]
    st = {key: jnp.stack([lw[key] for lw in lws]) for key in _FILTER_KEYS}
    filt = _filt(gc["feat"], st["fw1a"], st["fw1b"], st["fb1"], st["fq1"], st["fw2"], st["fb2"], st["fq2"],
                 st["fw3a"], st["fw3b"], lws[0]["deltas"], st["dbias"], seq_len=L, tl=min(TILE_FILT, L))
    fa = _fft1(gc["w1_half"], filt.reshape(DEPTH * 2, n1h, FFT_N2, D_HYENA), k1p=k1p)
    return _fspec(gc["g"], gc["gs"], gc["fc"], gc["fs"], fa.reshape(DEPTH, 2, 2, k1p, FFT_N2, D_HYENA))


def _layer(h, p_all, layer, lw, gc, bd, batch, hspec):
    L = gc["seq_len"]
    T = batch * L
    n1h, k1p = gc["n1h"], gc["k1p"]
    rc, rs1, rs2 = gc["rope"]
    tm = min(TILE_FFN, L)
    q, k, v, x0, vv = _inproj(h, lw["g_mix"], lw["w_in"], lw["qg"], lw["kg"], bd, rc, rs1, rs2,
                                lw["ws"], lw["bs"], seq_len=L, tm=min(TILE_INPROJ, L), layer=layer)
    attn = _attn(lw["sink"], q, k, v, lw["g_attn"], seq_len=L, tq=min(TILE_ATTN, L))
    a = _fft1(gc["w1_half"], vv.reshape(batch, n1h, FFT_N2, D_HYENA), k1p=k1p)
    a = _fft2(gc["g"], gc["gs"], gc["gi"], gc["gis"], gc["fc"], gc["fs"], gc["tc"], gc["ts"], a, hspec,
              layer=layer)
    y = _ifft1(gc["winv"], a).reshape(T, D_HYENA)
    return _ffn(h, attn, x0, y, p_all, lw["g_hy"], lw["w_out"], lw["g_ffn"], lw["w_up"], lw["wc"], lw["bc"],
                lw["w_down"], lw["w_gate"], lw["w_proj"], seq_len=L, tm=tm, layer=layer)


def _trunk(x, p, weights):
    batch, L, _ = x.shape
    gc = _group_consts(L)
    bd = jnp.asarray(np.kron(np.eye(MXU_W // HEAD_DIM), np.full((HEAD_DIM, HEAD_DIM), 1.0 / HEAD_DIM)), BF16)
    h = x.reshape(batch * L, D_MODEL)
    p_all = p.reshape(DEPTH * batch * L, D_PLE)
    lws = [_layer_weights(i, *weights) for i in range(DEPTH)]
    hspec = _filter_spectra(lws, gc)
    for i in range(DEPTH):
        h = _layer(h, p_all, i, lws[i], gc, bd, batch, hspec)
    return h.reshape(batch, L, D_MODEL)


def kernel(x_prompt, x_sample, p_prompt, p_sample, rms_mix, w_in, q_norm, k_norm, sink, w_short, b_short, filt_w1, filt_b1, filt_freq1, filt_w2, filt_b2, filt_freq2, filt_w3, hyena_bias, norm_attn_out, norm_hyena_out, w_out, rms_ffn, w_up, w_ffconv, b_ffconv, w_down, w_ple_gate, w_ple_proj):
    w_in = jnp.concatenate([w_in[..., D_QKV:], w_in[..., :D_QKV]], axis=-1)
    w_in, w_out, w_up, w_down, w_ple_gate, w_ple_proj = (
        w.astype(BF16) for w in (w_in, w_out, w_up, w_down, w_ple_gate, w_ple_proj))
    weights = (rms_mix, w_in, q_norm, k_norm, sink, w_short, b_short, filt_w1, filt_b1, filt_freq1,
               filt_w2, filt_b2, filt_freq2, filt_w3, hyena_bias, norm_attn_out, norm_hyena_out,
               w_out, rms_ffn, w_up, w_ffconv, b_ffconv, w_down, w_ple_gate, w_ple_proj)
    y_prompt = _trunk(x_prompt, p_prompt, weights)
    y_sample = _trunk(x_sample, p_sample, weights)
    return (y_prompt, y_sample)
```
